```python
import math
import jax, jax.numpy as jnp
from jax import lax
import numpy as np

D_MODEL = 1024
BATCH = 8
SEQ = 4096
DEPTH = 1

MOBA_HEADS = 8
MOBA_HEAD_DIM = 64
MOBA_WIDTH = MOBA_HEADS * MOBA_HEAD_DIM
MOBA_BLOCK = 256
MOBA_TOPK = 3
MOBA_Q_CHUNK = 32
MOBA_ROT_DIMS = MOBA_HEAD_DIM // 4

MLA_HEADS = 8
MLA_Q_RANK = 256
MLA_KV_RANK = 128
MLA_NOPE_DIM = 64
MLA_ROPE_DIM = 32
MLA_V_DIM = 64
MLA_WIDTH = MLA_HEADS * MLA_V_DIM
MLA_Q_BLOCK = 128

D_MIX = MOBA_WIDTH + MLA_WIDTH
IN_SPLITS = (MOBA_WIDTH, MOBA_WIDTH, MOBA_WIDTH, MLA_Q_RANK, MLA_KV_RANK, MLA_ROPE_DIM)
D_IN = sum(IN_SPLITS)

ROPE_THETA = 500000.0

N_GROUPS = 4
EXPERTS_PER_GROUP = 8
N_EXPERTS = N_GROUPS * EXPERTS_PER_GROUP
EXPERT_TOPK = 2
D_EXPERT = 256

EPS = 1e-6
NEG = -1e30

kernel_name = "hybrid_moba_mla_hiermoe_block"


def rms_norm(x, g):
    xf = x.astype(jnp.float32)
    y = xf * lax.rsqrt(jnp.mean(xf * xf, axis=-1, keepdims=True) + EPS)
    return (y * g.astype(jnp.float32)).astype(x.dtype)


def rope_tables(positions, dim):
    half = dim // 2
    inv_freq = jnp.exp(-math.log(ROPE_THETA) * jnp.arange(half, dtype=jnp.float32) * (2.0 / dim))
    ang = positions.astype(jnp.float32)[..., None] * inv_freq
    return jnp.cos(ang), jnp.sin(ang)


def apply_rope(x, cos, sin):
    half = x.shape[-1] // 2
    cos = cos[:, :, None, :]
    sin = sin[:, :, None, :]
    xf = x.astype(jnp.float32)
    x1, x2 = xf[..., :half], xf[..., half:]
    return jnp.concatenate([x1 * cos - x2 * sin, x2 * cos + x1 * sin], axis=-1).astype(x.dtype)


def gather_blocks(blocks, idx):
    return jax.vmap(jax.vmap(lambda blk, ix: blk[ix]))(blocks, idx)


def moba_attention(q, k, v):
    B, S, H, D = q.shape
    nb = -(-S // MOBA_BLOCK)
    pad = nb * MOBA_BLOCK - S
    k_sel = min(MOBA_TOPK, nb)
    scale = D ** -0.5
    qh = q.transpose(0, 2, 1, 3)
    kh = jnp.pad(k.transpose(0, 2, 1, 3), ((0, 0), (0, 0), (0, pad), (0, 0)))
    vh = jnp.pad(v.transpose(0, 2, 1, 3), ((0, 0), (0, 0), (0, pad), (0, 0)))
    k_blk = kh.reshape(B, H, nb, MOBA_BLOCK, D)
    v_blk = vh.reshape(B, H, nb, MOBA_BLOCK, D)
    k_mean = jnp.mean(k_blk.astype(jnp.float32), axis=3).astype(q.dtype)
    blk_ids = jnp.arange(nb)

    def chunk(ci):
        t0 = ci * MOBA_Q_CHUNK
        qc = lax.dynamic_slice_in_dim(qh, t0, MOBA_Q_CHUNK, axis=2)
        qpos = t0 + jnp.arange(MOBA_Q_CHUNK)
        own = t0 // MOBA_BLOCK
        gate = jnp.einsum('bhqd,bhnd->bhqn', qc, k_mean, preferred_element_type=jnp.float32)
        gate = jnp.where((blk_ids < own)[None, None, None, :], gate, NEG)
        _, idx = lax.top_k(gate, k_sel)
        valid = idx < own
        kg = gather_blocks(k_blk, idx)
        vg = gather_blocks(v_blk, idx)
        s_past = jnp.einsum('bhqd,bhqnld->bhqnl', qc, kg, preferred_element_type=jnp.float32) * scale
        s_past = jnp.where(valid[..., None], s_past, NEG).reshape(B, H, MOBA_Q_CHUNK, k_sel * MOBA_BLOCK)
        k_own = lax.dynamic_slice_in_dim(kh, own * MOBA_BLOCK, MOBA_BLOCK, axis=2)
        v_own = lax.dynamic_slice_in_dim(vh, own * MOBA_BLOCK, MOBA_BLOCK, axis=2)
        kpos = own * MOBA_BLOCK + jnp.arange(MOBA_BLOCK)
        s_own = jnp.einsum('bhqd,bhld->bhql', qc, k_own, preferred_element_type=jnp.float32) * scale
        s_own = jnp.where((kpos[None, :] <= qpos[:, None])[None, None], s_own, NEG)
        p = jax.nn.softmax(jnp.concatenate([s_past, s_own], axis=-1), axis=-1)
        n_past = k_sel * MOBA_BLOCK
        p_past = p[..., :n_past].reshape(B, H, MOBA_Q_CHUNK, k_sel, MOBA_BLOCK).astype(v.dtype)
        p_own = p[..., n_past:].astype(v.dtype)
        return (jnp.einsum('bhqnl,bhqnld->bhqd', p_past, vg)
                + jnp.einsum('bhql,bhld->bhqd', p_own, v_own))

    out = lax.map(chunk, jnp.arange(S // MOBA_Q_CHUNK))
    return out.transpose(1, 0, 3, 2, 4).reshape(B, S, H * D)


def mla_attention(c_q, c_kv, k_pe, cos_r, sin_r, g_cq, w_uq, g_ckv, w_ukv):
    B, S, _ = c_q.shape
    q = (rms_norm(c_q, g_cq) @ w_uq).reshape(B, S, MLA_HEADS, MLA_NOPE_DIM + MLA_ROPE_DIM)
    q_nope = q[..., :MLA_NOPE_DIM]
    q_pe = apply_rope(q[..., MLA_NOPE_DIM:], cos_r, sin_r)
    kv = (rms_norm(c_kv, g_ckv) @ w_ukv).reshape(B, S, MLA_HEADS, MLA_NOPE_DIM + MLA_V_DIM)
    k_nope = kv[..., :MLA_NOPE_DIM].transpose(0, 2, 1, 3)
    v = kv[..., MLA_NOPE_DIM:].transpose(0, 2, 1, 3)
    k_pe = apply_rope(k_pe[:, :, None, :], cos_r, sin_r)[:, :, 0, :]
    q_nope = q_nope.transpose(0, 2, 1, 3)
    q_pe = q_pe.transpose(0, 2, 1, 3)
    scale = (MLA_NOPE_DIM + MLA_ROPE_DIM) ** -0.5
    kpos = jnp.arange(S)

    def qblock(bi):
        t0 = bi * MLA_Q_BLOCK
        qn = lax.dynamic_slice_in_dim(q_nope, t0, MLA_Q_BLOCK, axis=2)
        qr = lax.dynamic_slice_in_dim(q_pe, t0, MLA_Q_BLOCK, axis=2)
        s = (jnp.einsum('bhqd,bhkd->bhqk', qn, k_nope, preferred_element_type=jnp.float32)
             + jnp.einsum('bhqr,bkr->bhqk', qr, k_pe, preferred_element_type=jnp.float32)) * scale
        qpos = t0 + jnp.arange(MLA_Q_BLOCK)
        s = jnp.where((kpos[None, :] <= qpos[:, None])[None, None], s, NEG)
        p = jax.nn.softmax(s, axis=-1).astype(v.dtype)
        return jnp.einsum('bhqk,bhkd->bhqd', p, v)

    out = lax.map(qblock, jnp.arange(S // MLA_Q_BLOCK))
    return out.transpose(1, 0, 3, 2, 4).reshape(B, S, MLA_HEADS * MLA_V_DIM)


def hier_moe(h, w_group, b_group, w_expert, b_expert, w_e_gate, w_e_up, w_e_down):
    B, S, D = h.shape
    t = h.reshape(B * S, D)
    T = t.shape[0]
    g_logits = (t @ w_group).astype(jnp.float32)
    p_group = jax.nn.softmax(g_logits, axis=-1)
    g_idx = jnp.argmax(g_logits + b_group.astype(jnp.float32), axis=-1)
    g_prob = jnp.take_along_axis(p_group, g_idx[:, None], axis=1)[:, 0]
    e_logits = (t @ w_expert).astype(jnp.float32).reshape(T, N_GROUPS, EXPERTS_PER_GROUP)
    e_sel = jnp.take_along_axis(e_logits, g_idx[:, None, None], axis=1)[:, 0]
    b_sel = b_expert.astype(jnp.float32).reshape(N_GROUPS, EXPERTS_PER_GROUP)[g_idx]
    p_exp = jax.nn.softmax(e_sel, axis=-1)
    _, top_i = lax.top_k(e_sel + b_sel, EXPERT_TOPK)
    top_p = jnp.take_along_axis(p_exp, top_i, axis=-1)
    top_w = top_p / jnp.sum(top_p, axis=-1, keepdims=True) * g_prob[:, None]
    expert_id = g_idx[:, None] * EXPERTS_PER_GROUP + top_i
    gates = jnp.sum(jax.nn.one_hot(expert_id, N_EXPERTS, dtype=jnp.float32) * top_w[..., None], axis=1)
    out = jnp.zeros((T, D), jnp.float32)
    for e in range(N_EXPERTS):
        hid = jax.nn.silu(t @ w_e_gate[e]) * (t @ w_e_up[e])
        out = out + gates[:, e:e + 1] * (hid @ w_e_down[e]).astype(jnp.float32)
    return out.astype(h.dtype).reshape(B, S, D)


def setup_inputs(seed: int = 0) -> dict:
    key = jax.random.key(seed)
    ks = jax.random.split(key, 24)
    f32 = jnp.float32

    def nrm(k, shape, fan_in, mult=1.0):
        return jax.random.normal(k, shape, f32) * (mult * fan_in ** -0.5)

    def gain(k, shape):
        return 1.0 + 0.05 * jax.random.normal(k, shape, f32)

    L = DEPTH
    return {
        "x": jax.random.normal(ks[0], (BATCH, SEQ, D_MODEL), f32),
        "c": jax.random.normal(ks[1], (BATCH, D_MODEL), f32),
        "positions": jnp.tile(jnp.arange(SEQ, dtype=jnp.int32)[None, :], (BATCH, 1)),
        "w_ada": nrm(ks[2], (L, D_MODEL, 6 * D_MODEL), D_MODEL, 0.5),
        "b_ada": 0.02 * jax.random.normal(ks[3], (L, 6 * D_MODEL), f32),
        "g_pre_attn": gain(ks[4], (L, D_MODEL)),
        "g_post_attn": gain(ks[5], (L, D_MODEL)),
        "g_pre_ffn": gain(ks[6], (L, D_MODEL)),
        "g_post_ffn": gain(ks[7], (L, D_MODEL)),
        "w_in": nrm(ks[8], (L, D_MODEL, D_IN), D_MODEL),
        "g_cq": gain(ks[9], (L, MLA_Q_RANK)),
        "w_uq": nrm(ks[10], (L, MLA_Q_RANK, MLA_HEADS * (MLA_NOPE_DIM + MLA_ROPE_DIM)), MLA_Q_RANK),
        "g_ckv": gain(ks[11], (L, MLA_KV_RANK)),
        "w_ukv": nrm(ks[12], (L, MLA_KV_RANK, MLA_HEADS * (MLA_NOPE_DIM + MLA_V_DIM)), MLA_KV_RANK),
        "g_out_moba": gain(ks[13], (L, MOBA_WIDTH)),
        "g_out_mla": gain(ks[14], (L, MLA_WIDTH)),
        "w_o": nrm(ks[15], (L, D_MIX, D_MODEL), D_MIX),
        "w_group": nrm(ks[16], (L, D_MODEL, N_GROUPS), D_MODEL),
        "b_group": 0.01 * jax.random.normal(ks[17], (L, N_GROUPS), f32),
        "w_expert": nrm(ks[18], (L, D_MODEL, N_EXPERTS), D_MODEL),
        "b_expert": 0.01 * jax.random.normal(ks[19], (L, N_EXPERTS), f32),
        "w_e_gate": nrm(ks[20], (L, N_EXPERTS, D_MODEL, D_EXPERT), D_MODEL),
        "w_e_up": nrm(ks[21], (L, N_EXPERTS, D_MODEL, D_EXPERT), D_MODEL),
        "w_e_down": nrm(ks[22], (L, N_EXPERTS, D_EXPERT, D_MODEL), D_EXPERT),
    }


def reference(x, c, positions, w_ada, b_ada, g_pre_attn, g_post_attn, g_pre_ffn, g_post_ffn,
              w_in, g_cq, w_uq, g_ckv, w_ukv, g_out_moba, g_out_mla, w_o,
              w_group, b_group, w_expert, b_expert, w_e_gate, w_e_up, w_e_down):
    B, S, D = x.shape
    cos_a, sin_a = rope_tables(positions, MOBA_ROT_DIMS)
    cos_m, sin_m = rope_tables(positions, MLA_ROPE_DIM)
    split_at = np.cumsum(IN_SPLITS)[:-1].tolist()
    c_act = jax.nn.silu(c)
    for l in range(DEPTH):
        mod = c_act @ w_ada[l] + b_ada[l]
        sh_a, sc_a, gt_a, sh_f, sc_f, gt_f = [m[:, None, :] for m in jnp.split(mod, 6, axis=-1)]

        h = rms_norm(x, g_pre_attn[l]) * (1.0 + sc_a) + sh_a
        z = h @ w_in[l]
        q_a, k_a, v_a, c_q, c_kv, k_pe = jnp.split(z, split_at, axis=-1)
        q_a = q_a.reshape(B, S, MOBA_HEADS, MOBA_HEAD_DIM)
        k_a = k_a.reshape(B, S, MOBA_HEADS, MOBA_HEAD_DIM)
        v_a = v_a.reshape(B, S, MOBA_HEADS, MOBA_HEAD_DIM)
        q_a = jnp.concatenate([apply_rope(q_a[..., :MOBA_ROT_DIMS], cos_a, sin_a), q_a[..., MOBA_ROT_DIMS:]], axis=-1)
        k_a = jnp.concatenate([apply_rope(k_a[..., :MOBA_ROT_DIMS], cos_a, sin_a), k_a[..., MOBA_ROT_DIMS:]], axis=-1)
        o_moba = moba_attention(q_a, k_a, v_a)
        o_mla = mla_attention(c_q, c_kv, k_pe, cos_m, sin_m, g_cq[l], w_uq[l], g_ckv[l], w_ukv[l])
        o = jnp.concatenate([rms_norm(o_moba, g_out_moba[l]), rms_norm(o_mla, g_out_mla[l])], axis=-1) @ w_o[l]
        x = x + gt_a * rms_norm(o, g_post_attn[l])

        h = rms_norm(x, g_pre_ffn[l]) * (1.0 + sc_f) + sh_f
        y = hier_moe(h, w_group[l], b_group[l], w_expert[l], b_expert[l], w_e_gate[l], w_e_up[l], w_e_down[l])
        x = x + gt_f * rms_norm(y, g_post_ffn[l])
    return x
```

```python
import functools
import math

import jax
import jax.numpy as jnp
import numpy as np
from jax import lax
from jax.experimental import pallas as pl
from jax.experimental.pallas import tpu as pltpu

F32 = jnp.float32
BF16 = jnp.bfloat16

MOBA_HEADS = 8
MOBA_HEAD_DIM = 64
MOBA_WIDTH = MOBA_HEADS * MOBA_HEAD_DIM
MOBA_BLOCK = 256
MOBA_TOPK = 3
MOBA_ROT_DIMS = MOBA_HEAD_DIM // 4
MLA_HEADS = 8
MLA_Q_RANK = 256
MLA_KV_RANK = 128
MLA_NOPE_DIM = 64
MLA_ROPE_DIM = 32
MLA_V_DIM = 64
MLA_WIDTH = MLA_HEADS * MLA_V_DIM
ROPE_THETA = 500000.0
N_GROUPS = 4
EXPERTS_PER_GROUP = 8
N_EXPERTS = N_GROUPS * EXPERTS_PER_GROUP
D_EXPERT = 256
EPS = 1e-6
NEG = -1e30

LANES = 128
HEADS_PER_STEP = LANES // MOBA_HEAD_DIM
N_PAIRS = MOBA_HEADS // HEADS_PER_STEP
VMEM_LIMIT = 56 * 1024 * 1024

TM_IN = 512
TQ_MLA = 256
TM_MOE = 256
TM_CMB = 256


def _cparams(sem):
    return pltpu.CompilerParams(dimension_semantics=sem, vmem_limit_bytes=VMEM_LIMIT)


def _rms(x, g):
    return x * lax.rsqrt(jnp.mean(x * x, axis=-1, keepdims=True) + EPS) * g


def _adaln_kernel(c_ref, w_ref, b_ref, o_ref):
    c = c_ref[...]
    ca = c * jax.nn.sigmoid(c)
    o_ref[...] = jnp.dot(ca, w_ref[...], preferred_element_type=F32,
                         precision=lax.Precision.HIGHEST) + b_ref[...]


def _adaln(c, w, b):
    B, D = c.shape
    N = w.shape[1]
    tn = 1536 if N % 1536 == 0 else N
    return pl.pallas_call(
        _adaln_kernel,
        out_shape=jax.ShapeDtypeStruct((B, N), F32),
        grid=(N // tn,),
        in_specs=[pl.BlockSpec((B, D), lambda j: (0, 0)),
                  pl.BlockSpec((D, tn), lambda j: (0, j)),
                  pl.BlockSpec((1, tn), lambda j: (0, j))],
        out_specs=pl.BlockSpec((B, tn), lambda j: (0, j)),
        compiler_params=_cparams(("arbitrary",)),
        name="adaln",
    )(c, w, b.reshape(1, N))


def _rope128(xc, tab_ref, shift):
    c = tab_ref[0, :, 0:LANES]
    s1 = tab_ref[0, :, LANES:2 * LANES]
    s2 = tab_ref[0, :, 2 * LANES:3 * LANES]
    return xc * c + pltpu.roll(xc, LANES - shift, axis=1) * s1 + pltpu.roll(xc, shift, axis=1) * s2


def _inproj_kernel(x_ref, mod_ref, gpre_ref, win_ref, ra_ref, rm_ref, gcq_ref, wuqn_ref, wuqp_ref,
                   gckv_ref, wukv_ref,
                   qa_ref, ka_ref, va_ref, kmean_ref, qcat_ref, kcat_ref, vm_ref):
    tm = x_ref.shape[1]
    x = x_ref[0]
    sh = mod_ref[0, 0:1, :]
    sc = mod_ref[0, 1:2, :]
    h = _rms(x, gpre_ref[...]) * (1.0 + sc) + sh
    z = jnp.dot(h.astype(BF16), win_ref[...], preferred_element_type=F32)

    moba_scale = MOBA_HEAD_DIM ** -0.5
    half_a = MOBA_ROT_DIMS // 2
    n_chunks = MOBA_WIDTH // LANES
    for cidx in range(n_chunks):
        lo = cidx * LANES
        q = _rope128(z[:, lo:lo + LANES], ra_ref, half_a)
        qa_ref[0, :, lo:lo + LANES] = (q * moba_scale).astype(BF16)
        k = _rope128(z[:, MOBA_WIDTH + lo:MOBA_WIDTH + lo + LANES], ra_ref, half_a)
        ka_ref[0, :, lo:lo + LANES] = k.astype(BF16)
        for r in range(tm // MOBA_BLOCK):
            kmean_ref[0, r, :, lo:lo + LANES] = jnp.mean(
                k[r * MOBA_BLOCK:(r + 1) * MOBA_BLOCK], axis=0, keepdims=True)
    va_ref[0] = z[:, 2 * MOBA_WIDTH:3 * MOBA_WIDTH].astype(BF16)

    o_cq = 3 * MOBA_WIDTH
    o_ckv = o_cq + MLA_Q_RANK
    o_kpe = o_ckv + MLA_KV_RANK
    mla_scale = (MLA_NOPE_DIM + MLA_ROPE_DIM) ** -0.5
    half_m = MLA_ROPE_DIM // 2

    cqn = _rms(z[:, o_cq:o_cq + MLA_Q_RANK], gcq_ref[...]).astype(BF16)
    qn = jnp.dot(cqn, wuqn_ref[...], preferred_element_type=F32) * mla_scale
    qp = jnp.dot(cqn, wuqp_ref[...], preferred_element_type=F32)
    lane = lax.broadcasted_iota(jnp.int32, (tm, LANES), 1)
    heads_per_chunk = LANES // MLA_ROPE_DIM
    qp_chunks = [_rope128(qp[:, g * LANES:(g + 1) * LANES], rm_ref, half_m) * mla_scale
                 for g in range(MLA_HEADS // heads_per_chunk)]
    for hd in range(MLA_HEADS):
        qcat_ref[0, hd, :, 0:LANES] = qn[:, hd * LANES:(hd + 1) * LANES].astype(BF16)
        sub = hd % heads_per_chunk
        keep = (lane >= sub * MLA_ROPE_DIM) & (lane < (sub + 1) * MLA_ROPE_DIM)
        qcat_ref[0, hd, :, LANES:2 * LANES] = jnp.where(keep, qp_chunks[hd // heads_per_chunk], 0.0).astype(BF16)

    ckvn = _rms(z[:, o_ckv:o_ckv + MLA_KV_RANK], gckv_ref[...]).astype(BF16)
    kv = jnp.dot(ckvn, wukv_ref[...], preferred_element_type=F32)
    kpe = _rope128(z[:, o_kpe:o_kpe + LANES], rm_ref, half_m).astype(BF16)
    for p in range(N_PAIRS):
        kcat_ref[0, p, :, 0:LANES] = kv[:, p * LANES:(p + 1) * LANES].astype(BF16)
        kcat_ref[0, p, :, LANES:2 * LANES] = kpe
    vm_ref[0] = kv[:, MLA_WIDTH:2 * MLA_WIDTH].astype(BF16)


def _inproj(x, mod3, g_pre, w_in_p, rope_a, rope_m, g_cq, w_uq_n, w_uq_p, g_ckv, w_ukv_p):
    B, S, D = x.shape
    tm = min(TM_IN, S)
    nb = S // MOBA_BLOCK
    rb = tm // MOBA_BLOCK
    const2 = lambda b, i: (0, 0)
    tok3 = lambda b, i: (b, i, 0)
    outs = pl.pallas_call(
        _inproj_kernel,
        out_shape=[
            jax.ShapeDtypeStruct((B, S, MOBA_WIDTH), BF16),
            jax.ShapeDtypeStruct((B, S, MOBA_WIDTH), BF16),
            jax.ShapeDtypeStruct((B, S, MOBA_WIDTH), BF16),
            jax.ShapeDtypeStruct((B, nb, 1, MOBA_WIDTH), F32),
            jax.ShapeDtypeStruct((B, MLA_HEADS, S, 2 * LANES), BF16),
            jax.ShapeDtypeStruct((B, N_PAIRS, S, 2 * LANES), BF16),
            jax.ShapeDtypeStruct((B, S, MLA_WIDTH), BF16),
        ],
        grid=(B, S // tm),
        in_specs=[
            pl.BlockSpec((1, tm, D), tok3),
            pl.BlockSpec((1, 6, D), lambda b, i: (b, 0, 0)),
            pl.BlockSpec((1, D), const2),
            pl.BlockSpec(w_in_p.shape, const2),
            pl.BlockSpec((1, tm, 3 * LANES), tok3),
            pl.BlockSpec((1, tm, 3 * LANES), tok3),
            pl.BlockSpec((1, MLA_Q_RANK), const2),
            pl.BlockSpec(w_uq_n.shape, const2),
            pl.BlockSpec(w_uq_p.shape, const2),
            pl.BlockSpec((1, MLA_KV_RANK), const2),
            pl.BlockSpec(w_ukv_p.shape, const2),
        ],
        out_specs=[
            pl.BlockSpec((1, tm, MOBA_WIDTH), tok3),
            pl.BlockSpec((1, tm, MOBA_WIDTH), tok3),
            pl.BlockSpec((1, tm, MOBA_WIDTH), tok3),
            pl.BlockSpec((1, rb, 1, MOBA_WIDTH), lambda b, i: (b, i, 0, 0)),
            pl.BlockSpec((1, MLA_HEADS, tm, 2 * LANES), lambda b, i: (b, 0, i, 0)),
            pl.BlockSpec((1, N_PAIRS, tm, 2 * LANES), lambda b, i: (b, 0, i, 0)),
            pl.BlockSpec((1, tm, MLA_WIDTH), tok3),
        ],
        compiler_params=_cparams(("arbitrary", "arbitrary")),
        name="inproj",
    )(x, mod3, g_pre, w_in_p, rope_a, rope_m, g_cq, w_uq_n, w_uq_p, g_ckv, w_ukv_p)
    return outs


def _nt_dot(a, b, **kw):
    return lax.dot_general(a, b, (((1,), (1,)), ((), ())), preferred_element_type=F32, **kw)


def _softmax_first(s, v, m_ref, l_ref, acc_ref, idx):
    m = jnp.max(s, axis=-1, keepdims=True)
    p = jnp.exp(s - m)
    m_ref[idx] = m
    l_ref[idx] = jnp.sum(p, axis=-1, keepdims=True)
    acc_ref[idx] = jnp.dot(p.astype(BF16), v, preferred_element_type=F32)


def _softmax_update(s, v, m_ref, l_ref, acc_ref, idx):
    m_old = m_ref[idx]
    m_new = jnp.maximum(m_old, jnp.max(s, axis=-1, keepdims=True))
    alpha = jnp.exp(m_old - m_new)
    p = jnp.exp(s - m_new)
    m_ref[idx] = m_new
    l_ref[idx] = alpha * l_ref[idx] + jnp.sum(p, axis=-1, keepdims=True)
    acc_ref[idx] = alpha * acc_ref[idx] + jnp.dot(p.astype(BF16), v, preferred_element_type=F32)


def _moba_kernel(q_ref, k_ref, v_ref, kmean_ref, o_ref, qh_ref, bias_ref, m_ref, l_ref, acc_ref):
    L = MOBA_BLOCK
    nb = kmean_ref.shape[1]
    i = pl.program_id(2)
    q = q_ref[0]
    lane = lax.broadcasted_iota(jnp.int32, (L, LANES), 1)
    kmean = kmean_ref[0, :, 0, :]
    blk = lax.broadcasted_iota(jnp.int32, (L, nb), 1)
    past = blk < i

    for hd in range(HEADS_PER_STEP):
        own = (lane >= hd * MOBA_HEAD_DIM) & (lane < (hd + 1) * MOBA_HEAD_DIM)
        qh = jnp.where(own, q, jnp.zeros_like(q))
        qh_ref[hd] = qh
        gate = _nt_dot(qh.astype(F32), kmean, precision=lax.Precision.HIGHEST)
        gate = jnp.where(past, gate, -jnp.inf)
        rank = jnp.zeros((L, nb), jnp.int32)
        for j in range(nb):
            gj = gate[:, j:j + 1]
            ahead = (gj > gate) | ((gj == gate) & (j < blk))
            rank = rank + ahead.astype(jnp.int32)
        bias = jnp.where(past & (rank < MOBA_TOPK), 0.0, NEG)
        for j in range(nb):
            bias_ref[hd, j] = jnp.broadcast_to(bias[:, j:j + 1], (L, LANES))

    row = lax.broadcasted_iota(jnp.int32, (L, L), 0)
    col = lax.broadcasted_iota(jnp.int32, (L, L), 1)
    start = pl.multiple_of(i * L, L)
    k_own = k_ref[0, pl.ds(start, L), :]
    v_own = v_ref[0, pl.ds(start, L), :]
    for hd in range(HEADS_PER_STEP):
        s = _nt_dot(qh_ref[hd], k_own)
        s = jnp.where(col <= row, s, NEG)
        _softmax_first(s, v_own, m_ref, l_ref, acc_ref, hd)

    def body(j, carry):
        off = pl.multiple_of(j * L, L)
        kj = k_ref[0, pl.ds(off, L), :]
        vj = v_ref[0, pl.ds(off, L), :]
        for hd in range(HEADS_PER_STEP):
            b = bias_ref[hd, j]
            s = _nt_dot(qh_ref[hd], kj) + jnp.concatenate([b] * (L // LANES), axis=1)
            _softmax_update(s, vj, m_ref, l_ref, acc_ref, hd)
        return carry

    lax.fori_loop(0, i, body, 0)

    o0 = acc_ref[0] / l_ref[0]
    o1 = acc_ref[1] / l_ref[1]
    o_ref[0] = jnp.where(lane < MOBA_HEAD_DIM, o0, o1)


def _moba(qa, ka, va, kmean):
    B, S, _ = qa.shape
    L = MOBA_BLOCK
    nb = S // L
    return pl.pallas_call(
        _moba_kernel,
        out_shape=jax.ShapeDtypeStruct((B, S, MOBA_WIDTH), F32),
        grid=(B, N_PAIRS, nb),
        in_specs=[
            pl.BlockSpec((1, L, LANES), lambda b, p, i: (b, i, p)),
            pl.BlockSpec((1, S, LANES), lambda b, p, i: (b, 0, p)),
            pl.BlockSpec((1, S, LANES), lambda b, p, i: (b, 0, p)),
            pl.BlockSpec((1, nb, 1, LANES), lambda b, p, i: (b, 0, 0, p)),
        ],
        out_specs=pl.BlockSpec((1, L, LANES), lambda b, p, i: (b, i, p)),
        scratch_shapes=[
            pltpu.VMEM((HEADS_PER_STEP, L, LANES), BF16),
            pltpu.VMEM((HEADS_PER_STEP, nb, L, LANES), F32),
            pltpu.VMEM((HEADS_PER_STEP, L, 1), F32),
            pltpu.VMEM((HEADS_PER_STEP, L, 1), F32),
            pltpu.VMEM((HEADS_PER_STEP, L, LANES), F32),
        ],
        compiler_params=_cparams(("arbitrary", "arbitrary", "arbitrary")),
        name="moba",
    )(qa, ka, va, kmean)


def _mla_kernel(q_ref, k_ref, v_ref, o_ref, m_ref, l_ref, acc_ref):
    tq = q_ref.shape[2]
    i = pl.program_id(2)
    q2 = q_ref[0].reshape(HEADS_PER_STEP * tq, 2 * LANES)
    row = lax.broadcasted_iota(jnp.int32, (HEADS_PER_STEP * tq, tq), 0)
    col = lax.broadcasted_iota(jnp.int32, (HEADS_PER_STEP * tq, tq), 1)
    qrow = jnp.where(row >= tq, row - tq, row)

    start = pl.multiple_of(i * tq, tq)
    s = _nt_dot(q2, k_ref[0, 0, pl.ds(start, tq), :])
    s = jnp.where(col <= qrow, s, NEG)
    _softmax_first(s, v_ref[0, pl.ds(start, tq), :], m_ref, l_ref, acc_ref, slice(None))

    def body(j, carry):
        off = pl.multiple_of(j * tq, tq)
        sj = _nt_dot(q2, k_ref[0, 0, pl.ds(off, tq), :])
        _softmax_update(sj, v_ref[0, pl.ds(off, tq), :], m_ref, l_ref, acc_ref, slice(None))
        return carry

    lax.fori_loop(0, i, body, 0)

    o = acc_ref[...] / l_ref[...]
    lane = lax.broadcasted_iota(jnp.int32, (tq, LANES), 1)
    o_ref[0] = jnp.where(lane < MLA_V_DIM, o[:tq], o[tq:])


def _mla(qcat, kcat, vm):
    B, _, S, W = qcat.shape
    tq = min(TQ_MLA, S)
    return pl.pallas_call(
        _mla_kernel,
        out_shape=jax.ShapeDtypeStruct((B, S, MLA_WIDTH), F32),
        grid=(B, N_PAIRS, S // tq),
        in_specs=[
            pl.BlockSpec((1, HEADS_PER_STEP, tq, W), lambda b, p, i: (b, p, i, 0)),
            pl.BlockSpec((1, 1, S, W), lambda b, p, i: (b, p, 0, 0)),
            pl.BlockSpec((1, S, LANES), lambda b, p, i: (b, 0, p)),
        ],
        out_specs=pl.BlockSpec((1, tq, LANES), lambda b, p, i: (b, i, p)),
        scratch_shapes=[
            pltpu.VMEM((HEADS_PER_STEP * tq, 1), F32),
            pltpu.VMEM((HEADS_PER_STEP * tq, 1), F32),
            pltpu.VMEM((HEADS_PER_STEP * tq, LANES), F32),
        ],
        compiler_params=_cparams(("arbitrary", "arbitrary", "arbitrary")),
        name="mla",
    )(qcat, kcat, vm)


def _postattn_kernel(x_ref, om_ref, ol_ref, mod_ref, gom_ref, gol_ref, wo_ref, gpost_ref, gpre_ref,
                     wr_ref, br_ref, x1_ref, h2_ref, ri_ref, rw_ref):
    tm = x_ref.shape[1]
    gt_a = mod_ref[0, 2:3, :]
    sh_f = mod_ref[0, 3:4, :]
    sc_f = mod_ref[0, 4:5, :]
    a = jnp.concatenate([_rms(om_ref[0], gom_ref[...]), _rms(ol_ref[0], gol_ref[...])], axis=-1)
    o = jnp.dot(a.astype(BF16), wo_ref[...], preferred_element_type=F32)
    x1 = x_ref[0] + gt_a * _rms(o, gpost_ref[...])
    x1_ref[0] = x1
    h2 = _rms(x1, gpre_ref[...]) * (1.0 + sc_f) + sh_f
    h2_ref[0] = h2

    logit = jnp.dot(h2, wr_ref[...], preferred_element_type=F32, precision=lax.Precision.HIGHEST)
    biased = logit + br_ref[...]
    lane = lax.broadcasted_iota(jnp.int32, (tm, LANES), 1)
    big = jnp.int32(LANES)

    def first_argmax(v):
        mx = jnp.max(v, axis=-1, keepdims=True)
        return jnp.min(jnp.where(v == mx, lane, big), axis=-1, keepdims=True)

    def pick(v, idx):
        return jnp.sum(jnp.where(lane == idx, v, 0.0), axis=-1, keepdims=True)

    gmask = (lane >= N_EXPERTS) & (lane < N_EXPERTS + N_GROUPS)
    gl = jnp.where(gmask, logit, -jnp.inf)
    pg = jnp.exp(gl - jnp.max(gl, axis=-1, keepdims=True))
    g_lane = first_argmax(jnp.where(gmask, biased, -jnp.inf))
    g_prob = pick(pg, g_lane) / jnp.sum(pg, axis=-1, keepdims=True)
    e_lo = (g_lane - N_EXPERTS) * EXPERTS_PER_GROUP
    emask = (lane >= e_lo) & (lane < e_lo + EXPERTS_PER_GROUP)
    el = jnp.where(emask, logit, -jnp.inf)
    pe = jnp.exp(el - jnp.max(el, axis=-1, keepdims=True))
    eb = jnp.where(emask, biased, -jnp.inf)
    i1 = first_argmax(eb)
    i2 = first_argmax(jnp.where(lane == i1, -jnp.inf, eb))
    p1 = pick(pe, i1)
    p2 = pick(pe, i2)
    denom = p1 + p2
    w1 = p1 / denom * g_prob
    w2 = p2 / denom * g_prob
    ri_ref[0] = jnp.where(lane == 0, i1, jnp.where(lane == 1, i2, 0))
    rw_ref[0] = jnp.where(lane == 0, w1, jnp.where(lane == 1, w2, 0.0))


def _postattn(x, o_moba, o_mla, mod3, g_om, g_ol, w_o, g_post, g_pre, w_r, b_r):
    B, S, D = x.shape
    tm = min(TM_IN, S)
    const2 = lambda b, i: (0, 0)
    tok3 = lambda b, i: (b, i, 0)
    return pl.pallas_call(
        _postattn_kernel,
        out_shape=[
            jax.ShapeDtypeStruct((B, S, D), F32),
            jax.ShapeDtypeStruct((B, S, D), F32),
            jax.ShapeDtypeStruct((B, S, LANES), jnp.int32),
            jax.ShapeDtypeStruct((B, S, LANES), F32),
        ],
        grid=(B, S // tm),
        in_specs=[
            pl.BlockSpec((1, tm, D), tok3),
            pl.BlockSpec((1, tm, MOBA_WIDTH), tok3),
            pl.BlockSpec((1, tm, MLA_WIDTH), tok3),
            pl.BlockSpec((1, 6, D), lambda b, i: (b, 0, 0)),
            pl.BlockSpec((1, MOBA_WIDTH), const2),
            pl.BlockSpec((1, MLA_WIDTH), const2),
            pl.BlockSpec(w_o.shape, const2),
            pl.BlockSpec((1, D), const2),
            pl.BlockSpec((1, D), const2),
            pl.BlockSpec(w_r.shape, const2),
            pl.BlockSpec((1, LANES), const2),
        ],
        out_specs=[
            pl.BlockSpec((1, tm, D), tok3),
            pl.BlockSpec((1, tm, D), tok3),
            pl.BlockSpec((1, tm, LANES), tok3),
            pl.BlockSpec((1, tm, LANES), tok3),
        ],
        compiler_params=_cparams(("arbitrary", "arbitrary")),
        name="postattn",
    )(x, o_moba, o_mla, mod3, g_om, g_ol, w_o, g_post, g_pre, w_r, b_r)


GATHER_UNROLL = 8


def _start_row_gather(idx_ref, base, n_rows, src_hbm, dst_ref, sem):
    def body(r, carry):
        row = idx_ref[base + r]
        pltpu.make_async_copy(src_hbm.at[pl.ds(row, 1), :], dst_ref.at[pl.ds(r, 1), :], sem).start()
        return carry
    lax.fori_loop(0, n_rows, body, 0, unroll=GATHER_UNROLL)


def _wait_row_gather(n_rows, src_hbm, dst_ref, sem):
    pltpu.make_async_copy(src_hbm.at[pl.ds(0, n_rows), :], dst_ref, sem).wait()


def _moe_kernel(te_ref, nu_ref, tok_ref, h_hbm, wgu_ref, wd_ref, y_ref, xbuf, sem):
    tm = y_ref.shape[0]
    i = pl.program_id(0)
    n_used = nu_ref[0]
    slot = i % 2

    @pl.when((i == 0) & (n_used > 0))
    def _():
        _start_row_gather(tok_ref, 0, tm, h_hbm, xbuf.at[0], sem.at[0])

    @pl.when(i + 1 < n_used)
    def _():
        _start_row_gather(tok_ref, (i + 1) * tm, tm, h_hbm, xbuf.at[1 - slot], sem.at[1 - slot])

    @pl.when(i < n_used)
    def _():
        _wait_row_gather(tm, h_hbm, xbuf.at[slot], sem.at[slot])
        xb = xbuf[slot].astype(BF16)
        gu = jnp.dot(xb, wgu_ref[0], preferred_element_type=F32)
        g = gu[:, :D_EXPERT]
        u = gu[:, D_EXPERT:]
        hid = g * jax.nn.sigmoid(g) * u
        y_ref[...] = jnp.dot(hid.astype(BF16), wd_ref[0], preferred_element_type=F32)

    @pl.when(i >= n_used)
    def _():
        y_ref[...] = jnp.zeros_like(y_ref)


def _moe(tile_expert, n_used, tok_sorted, h2, w_gu, w_d):
    T, D = h2.shape
    n_tiles = tile_expert.shape[0]
    tm = TM_MOE
    grid_spec = pltpu.PrefetchScalarGridSpec(
        num_scalar_prefetch=3,
        grid=(n_tiles,),
        in_specs=[
            pl.BlockSpec(memory_space=pl.ANY),
            pl.BlockSpec((1, D, 2 * D_EXPERT), lambda i, te, nu, tok: (te[i], 0, 0)),
            pl.BlockSpec((1, D_EXPERT, D), lambda i, te, nu, tok: (te[i], 0, 0)),
        ],
        out_specs=pl.BlockSpec((tm, D), lambda i, te, nu, tok: (i, 0)),
        scratch_shapes=[pltpu.VMEM((2, tm, D), F32), pltpu.SemaphoreType.DMA((2,))],
    )
    return pl.pallas_call(
        _moe_kernel,
        out_shape=jax.ShapeDtypeStruct((n_tiles * tm, D), F32),
        grid_spec=grid_spec,
        compiler_params=_cparams(("arbitrary",)),
        name="moe",
    )(tile_expert, n_used, tok_sorted, h2, w_gu, w_d)


def _combine_kernel(pos_ref, y_hbm, x1_ref, rw_ref, mod_ref, g_ref, o_ref, ybuf, sem):
    tm = x1_ref.shape[1]
    n_j = pl.num_programs(1)
    step = pl.program_id(0) * n_j + pl.program_id(1)
    n_steps = pl.num_programs(0) * n_j
    slot = step % 2

    @pl.when(step == 0)
    def _():
        _start_row_gather(pos_ref, 0, 2 * tm, y_hbm, ybuf.at[0], sem.at[0])

    @pl.when(step + 1 < n_steps)
    def _():
        _start_row_gather(pos_ref, (step + 1) * 2 * tm, 2 * tm, y_hbm, ybuf.at[1 - slot], sem.at[1 - slot])

    _wait_row_gather(2 * tm, y_hbm, ybuf.at[slot], sem.at[slot])
    w1 = rw_ref[0, :, 0:1]
    w2 = rw_ref[0, :, 1:2]
    y = w1 * ybuf[slot, 0:tm, :] + w2 * ybuf[slot, tm:2 * tm, :]
    gt_f = mod_ref[0, 5:6, :]
    o_ref[0] = x1_ref[0] + gt_f * _rms(y, g_ref[...])


def _combine(pos, y_sorted, x1, route_w, mod3, g_post):
    B, S, D = x1.shape
    tm = min(TM_CMB, S)
    grid_spec = pltpu.PrefetchScalarGridSpec(
        num_scalar_prefetch=1,
        grid=(B, S // tm),
        in_specs=[
            pl.BlockSpec(memory_space=pl.ANY),
            pl.BlockSpec((1, tm, D), lambda b, j, pos: (b, j, 0)),
            pl.BlockSpec((1, tm, LANES), lambda b, j, pos: (b, j, 0)),
            pl.BlockSpec((1, 6, D), lambda b, j, pos: (b, 0, 0)),
            pl.BlockSpec((1, D), lambda b, j, pos: (0, 0)),
        ],
        out_specs=pl.BlockSpec((1, tm, D), lambda b, j, pos: (b, j, 0)),
        scratch_shapes=[pltpu.VMEM((2, 2 * tm, D), F32), pltpu.SemaphoreType.DMA((2,))],
    )
    return pl.pallas_call(
        _combine_kernel,
        out_shape=jax.ShapeDtypeStruct((B, S, D), F32),
        grid_spec=grid_spec,
        compiler_params=_cparams(("arbitrary", "arbitrary")),
        name="combine",
    )(pos, y_sorted, x1, route_w, mod3, g_post)


def _rope_table(positions, dim, period_fill):
    half = dim // 2
    inv_freq = jnp.exp(-math.log(ROPE_THETA) * jnp.arange(half, dtype=F32) * (2.0 / dim))
    ang = positions.astype(F32)[..., None] * inv_freq
    cos, sin = jnp.cos(ang), jnp.sin(ang)
    B, S = positions.shape
    rest = period_fill - dim
    one = jnp.ones((B, S, rest), F32)
    zero_r = jnp.zeros((B, S, rest), F32)
    zero_h = jnp.zeros((B, S, half), F32)
    c = jnp.concatenate([cos, cos, one], axis=-1)
    s1 = jnp.concatenate([-sin, zero_h, zero_r], axis=-1)
    s2 = jnp.concatenate([zero_h, sin, zero_r], axis=-1)
    reps = LANES // period_fill
    return jnp.concatenate([jnp.tile(t, (1, 1, reps)) for t in (c, s1, s2)], axis=-1)


def _routing_metadata(eid, tm):
    T = eid.shape[0]
    n_pairs = 2 * T
    n_tiles = n_pairs // tm + N_EXPERTS
    flat = eid.reshape(-1)
    order = jnp.argsort(flat, stable=True).astype(jnp.int32)
    counts = jnp.sum(flat[:, None] == jnp.arange(N_EXPERTS, dtype=jnp.int32)[None, :], axis=0).astype(jnp.int32)
    tiles_per = (counts + tm - 1) // tm
    tile_end = jnp.cumsum(tiles_per)
    tile_start = tile_end - tiles_per
    n_used = tile_end[-1]
    seg_start = jnp.cumsum(counts) - counts
    tile_ids = jnp.arange(n_tiles, dtype=jnp.int32)
    te = jnp.minimum(jnp.searchsorted(tile_end, tile_ids, side="right"), N_EXPERTS - 1).astype(jnp.int32)
    last_e = te[jnp.maximum(n_used - 1, 0)]
    tile_expert = jnp.where(tile_ids < n_used, te, last_e)
    slot_ids = jnp.arange(n_tiles * tm, dtype=jnp.int32)
    slot_e = jnp.repeat(tile_expert, tm)
    r = slot_ids - jnp.repeat(tile_start[tile_expert], tm) * tm
    valid = (jnp.repeat(tile_ids, tm) < n_used) & (r < counts[slot_e])
    src = jnp.clip(seg_start[slot_e] + r, 0, n_pairs - 1)
    tok_sorted = jnp.where(valid, order[src] // 2, 0).astype(jnp.int32)
    sorted_e = flat[order]
    padded_pos = tile_start[sorted_e] * tm + (jnp.arange(n_pairs, dtype=jnp.int32) - seg_start[sorted_e])
    pos_pair = jnp.zeros((n_pairs,), jnp.int32).at[order].set(padded_pos.astype(jnp.int32))
    return tile_expert, n_used.reshape(1).astype(jnp.int32), tok_sorted, pos_pair


def kernel(x, c, positions, w_ada, b_ada, g_pre_attn, g_post_attn, g_pre_ffn, g_post_ffn, w_in, g_cq, w_uq,
           g_ckv, w_ukv, g_out_moba, g_out_mla, w_o, w_group, b_group, w_expert, b_expert, w_e_gate, w_e_up,
           w_e_down):
    B, S, D = x.shape
    depth = w_ada.shape[0]
    rope_a = _rope_table(positions, MOBA_ROT_DIMS, MOBA_HEAD_DIM)
    rope_m = _rope_table(positions, MLA_ROPE_DIM, MLA_ROPE_DIM)
    row = lambda v: v.reshape(1, -1)

    for l in range(depth):
        n_main = 3 * MOBA_WIDTH + MLA_Q_RANK + MLA_KV_RANK
        w_in_p = jnp.concatenate([w_in[l][:, :n_main], jnp.tile(w_in[l][:, n_main:], (1, LANES // MLA_ROPE_DIM))],
                                 axis=1).astype(BF16)
        wq = w_uq[l].reshape(MLA_Q_RANK, MLA_HEADS, MLA_NOPE_DIM + MLA_ROPE_DIM)
        wq_n = wq[:, :, :MLA_NOPE_DIM]
        zero_n = jnp.zeros_like(wq_n)
        even = (jnp.arange(MLA_HEADS) % 2 == 0)[None, :, None]
        w_uq_n = jnp.concatenate([jnp.where(even, wq_n, zero_n), jnp.where(even, zero_n, wq_n)], axis=-1)
        w_uq_n = w_uq_n.reshape(MLA_Q_RANK, MLA_HEADS * LANES).astype(BF16)
        w_uq_p = wq[:, :, MLA_NOPE_DIM:].reshape(MLA_Q_RANK, MLA_HEADS * MLA_ROPE_DIM).astype(BF16)
        wkv = w_ukv[l].reshape(MLA_KV_RANK, MLA_HEADS, MLA_NOPE_DIM + MLA_V_DIM)
        w_ukv_p = jnp.concatenate([wkv[:, :, :MLA_NOPE_DIM].reshape(MLA_KV_RANK, -1),
                                   wkv[:, :, MLA_NOPE_DIM:].reshape(MLA_KV_RANK, -1)], axis=1).astype(BF16)
        w_r = jnp.concatenate([w_expert[l], w_group[l],
                               jnp.zeros((D, LANES - N_EXPERTS - N_GROUPS), F32)], axis=1)
        b_r = jnp.concatenate([b_expert[l], b_group[l], jnp.zeros((LANES - N_EXPERTS - N_GROUPS,), F32)]).reshape(1, LANES)
        w_gu = jnp.concatenate([w_e_gate[l], w_e_up[l]], axis=-1).astype(BF16)
        w_d = w_e_down[l].astype(BF16)

        mod3 = _adaln(c, w_ada[l], b_ada[l]).reshape(B, 6, D)

        qa, ka, va, kmean, qcat, kcat, vm = _inproj(
            x, mod3, row(g_pre_attn[l]), w_in_p, rope_a, rope_m, row(g_cq[l]), w_uq_n, w_uq_p,
            row(g_ckv[l]), w_ukv_p)
        o_moba = _moba(qa, ka, va, kmean)
        o_mla = _mla(qcat, kcat, vm)
        x1, h2, route_i, route_w = _postattn(
            x, o_moba, o_mla, mod3, row(g_out_moba[l]), row(g_out_mla[l]), w_o[l].astype(BF16),
            row(g_post_attn[l]), row(g_pre_ffn[l]), w_r, b_r)

        eid = route_i.reshape(B * S, LANES)[:, :2]
        tile_expert, n_used, tok_sorted, pos_pair = _routing_metadata(eid, TM_MOE)
        y_sorted = _moe(tile_expert, n_used, tok_sorted, h2.reshape(B * S, D), w_gu, w_d)
        tmc = min(TM_CMB, S)
        pos = pos_pair.reshape(B * S // tmc, tmc, 2).transpose(0, 2, 1).reshape(-1)
        x = _combine(pos, y_sorted, x1, route_w, mod3, row(g_post_ffn[l]))
    return x
```

```python
import functools
import math

import jax
import jax.numpy as jnp
import numpy as np
from jax import lax
from jax.experimental import pallas as pl
from jax.experimental.pallas import tpu as pltpu

F32 = jnp.float32
BF16 = jnp.bfloat16

MOBA_HEADS = 8
MOBA_HEAD_DIM = 64
MOBA_WIDTH = MOBA_HEADS * MOBA_HEAD_DIM
MOBA_BLOCK = 256
MOBA_TOPK = 3
MOBA_ROT_DIMS = MOBA_HEAD_DIM // 4
MLA_HEADS = 8
MLA_Q_RANK = 256
MLA_KV_RANK = 128
MLA_NOPE_DIM = 64
MLA_ROPE_DIM = 32
MLA_V_DIM = 64
MLA_WIDTH = MLA_HEADS * MLA_V_DIM
ROPE_THETA = 500000.0
N_GROUPS = 4
EXPERTS_PER_GROUP = 8
N_EXPERTS = N_GROUPS * EXPERTS_PER_GROUP
D_EXPERT = 256
EPS = 1e-6
NEG = -1e30

LANES = 128
HEADS_PER_STEP = LANES // MOBA_HEAD_DIM
N_PAIRS = MOBA_HEADS // HEADS_PER_STEP
VMEM_LIMIT = 56 * 1024 * 1024

TM_IN = 512
TM_MOE = 256
TM_CMB = 256


def _cparams(sem):
    return pltpu.CompilerParams(dimension_semantics=sem, vmem_limit_bytes=VMEM_LIMIT)


def _rms(x, g):
    return x * lax.rsqrt(jnp.mean(x * x, axis=-1, keepdims=True) + EPS) * g


def _adaln_kernel(c_ref, w_ref, b_ref, o_ref):
    c = c_ref[...]
    ca = c * jax.nn.sigmoid(c)
    o_ref[...] = jnp.dot(ca, w_ref[...], preferred_element_type=F32,
                         precision=lax.Precision.HIGHEST) + b_ref[...]


def _adaln(c, w, b):
    B, D = c.shape
    N = w.shape[1]
    tn = 1536 if N % 1536 == 0 else N
    return pl.pallas_call(
        _adaln_kernel,
        out_shape=jax.ShapeDtypeStruct((B, N), F32),
        grid=(N // tn,),
        in_specs=[pl.BlockSpec((B, D), lambda j: (0, 0)),
                  pl.BlockSpec((D, tn), lambda j: (0, j)),
                  pl.BlockSpec((1, tn), lambda j: (0, j))],
        out_specs=pl.BlockSpec((B, tn), lambda j: (0, j)),
        compiler_params=_cparams(("arbitrary",)),
        name="adaln",
    )(c, w, b.reshape(1, N))


def _rope128(xc, tab_ref, shift):
    c = tab_ref[0, :, 0:LANES]
    s1 = tab_ref[0, :, LANES:2 * LANES]
    s2 = tab_ref[0, :, 2 * LANES:3 * LANES]
    return xc * c + pltpu.roll(xc, LANES - shift, axis=1) * s1 + pltpu.roll(xc, shift, axis=1) * s2


def _store_vt(src, col0, vt_ref):
    for p in range(N_PAIRS):
        for r in range(src.shape[0] // MOBA_BLOCK):
            blk = src[r * MOBA_BLOCK:(r + 1) * MOBA_BLOCK, col0 + p * LANES:col0 + (p + 1) * LANES]
            vt_ref[0, p, r] = blk.T.astype(BF16)


def _inproj_kernel(x_ref, mod_ref, gpre_ref, win_ref, ra_ref, rm_ref, gcq_ref, wuqn_ref, wuqp_ref,
                   gckv_ref, wukv_ref,
                   qa_ref, ka_ref, vat_ref, kmean_ref, qcat_ref, kcat_ref, vmt_ref):
    tm = x_ref.shape[1]
    x = x_ref[0]
    sh = mod_ref[0, 0:1, :]
    sc = mod_ref[0, 1:2, :]
    h = _rms(x, gpre_ref[...]) * (1.0 + sc) + sh
    z = jnp.dot(h.astype(BF16), win_ref[...], preferred_element_type=F32)

    moba_scale = MOBA_HEAD_DIM ** -0.5
    half_a = MOBA_ROT_DIMS // 2
    n_chunks = MOBA_WIDTH // LANES
    for cidx in range(n_chunks):
        lo = cidx * LANES
        q = _rope128(z[:, lo:lo + LANES], ra_ref, half_a)
        qa_ref[0, :, lo:lo + LANES] = (q * moba_scale).astype(BF16)
        k = _rope128(z[:, MOBA_WIDTH + lo:MOBA_WIDTH + lo + LANES], ra_ref, half_a)
        ka_ref[0, :, lo:lo + LANES] = k.astype(BF16)
        for r in range(tm // MOBA_BLOCK):
            kmean_ref[0, r, :, lo:lo + LANES] = jnp.mean(
                k[r * MOBA_BLOCK:(r + 1) * MOBA_BLOCK], axis=0, keepdims=True)
    _store_vt(z, 2 * MOBA_WIDTH, vat_ref)

    o_cq = 3 * MOBA_WIDTH
    o_ckv = o_cq + MLA_Q_RANK
    o_kpe = o_ckv + MLA_KV_RANK
    mla_scale = (MLA_NOPE_DIM + MLA_ROPE_DIM) ** -0.5
    half_m = MLA_ROPE_DIM // 2

    cqn = _rms(z[:, o_cq:o_cq + MLA_Q_RANK], gcq_ref[...]).astype(BF16)
    qn = jnp.dot(cqn, wuqn_ref[...], preferred_element_type=F32) * mla_scale
    qp = jnp.dot(cqn, wuqp_ref[...], preferred_element_type=F32)
    lane = lax.broadcasted_iota(jnp.int32, (tm, LANES), 1)
    heads_per_chunk = LANES // MLA_ROPE_DIM
    qp_chunks = [_rope128(qp[:, g * LANES:(g + 1) * LANES], rm_ref, half_m) * mla_scale
                 for g in range(MLA_HEADS // heads_per_chunk)]
    for hd in range(MLA_HEADS):
        qcat_ref[0, hd, :, 0:LANES] = qn[:, hd * LANES:(hd + 1) * LANES].astype(BF16)
        sub = hd % heads_per_chunk
        keep = (lane >= sub * MLA_ROPE_DIM) & (lane < (sub + 1) * MLA_ROPE_DIM)
        qcat_ref[0, hd, :, LANES:2 * LANES] = jnp.where(keep, qp_chunks[hd // heads_per_chunk], 0.0).astype(BF16)

    ckvn = _rms(z[:, o_ckv:o_ckv + MLA_KV_RANK], gckv_ref[...]).astype(BF16)
    kv = jnp.dot(ckvn, wukv_ref[...], preferred_element_type=F32)
    kpe = _rope128(z[:, o_kpe:o_kpe + LANES], rm_ref, half_m).astype(BF16)
    for p in range(N_PAIRS):
        kcat_ref[0, p, :, 0:LANES] = kv[:, p * LANES:(p + 1) * LANES].astype(BF16)
        kcat_ref[0, p, :, LANES:2 * LANES] = kpe
    _store_vt(kv, MLA_WIDTH, vmt_ref)


def _inproj(x, mod3, g_pre, w_in_p, rope_a, rope_m, g_cq, w_uq_n, w_uq_p, g_ckv, w_ukv_p):
    B, S, D = x.shape
    tm = min(TM_IN, S)
    nb = S // MOBA_BLOCK
    rb = tm // MOBA_BLOCK
    const2 = lambda b, i: (0, 0)
    tok3 = lambda b, i: (b, i, 0)
    outs = pl.pallas_call(
        _inproj_kernel,
        out_shape=[
            jax.ShapeDtypeStruct((B, S, MOBA_WIDTH), BF16),
            jax.ShapeDtypeStruct((B, S, MOBA_WIDTH), BF16),
            jax.ShapeDtypeStruct((B, N_PAIRS, nb, LANES, MOBA_BLOCK), BF16),
            jax.ShapeDtypeStruct((B, nb, 1, MOBA_WIDTH), F32),
            jax.ShapeDtypeStruct((B, MLA_HEADS, S, 2 * LANES), BF16),
            jax.ShapeDtypeStruct((B, N_PAIRS, S, 2 * LANES), BF16),
            jax.ShapeDtypeStruct((B, N_PAIRS, nb, LANES, MOBA_BLOCK), BF16),
        ],
        grid=(B, S // tm),
        in_specs=[
            pl.BlockSpec((1, tm, D), tok3),
            pl.BlockSpec((1, 6, D), lambda b, i: (b, 0, 0)),
            pl.BlockSpec((1, D), const2),
            pl.BlockSpec(w_in_p.shape, const2),
            pl.BlockSpec((1, tm, 3 * LANES), tok3),
            pl.BlockSpec((1, tm, 3 * LANES), tok3),
            pl.BlockSpec((1, MLA_Q_RANK), const2),
            pl.BlockSpec(w_uq_n.shape, const2),
            pl.BlockSpec(w_uq_p.shape, const2),
            pl.BlockSpec((1, MLA_KV_RANK), const2),
            pl.BlockSpec(w_ukv_p.shape, const2),
        ],
        out_specs=[
            pl.BlockSpec((1, tm, MOBA_WIDTH), tok3),
            pl.BlockSpec((1, tm, MOBA_WIDTH), tok3),
            pl.BlockSpec((1, N_PAIRS, rb, LANES, MOBA_BLOCK), lambda b, i: (b, 0, i, 0, 0)),
            pl.BlockSpec((1, rb, 1, MOBA_WIDTH), lambda b, i: (b, i, 0, 0)),
            pl.BlockSpec((1, MLA_HEADS, tm, 2 * LANES), lambda b, i: (b, 0, i, 0)),
            pl.BlockSpec((1, N_PAIRS, tm, 2 * LANES), lambda b, i: (b, 0, i, 0)),
            pl.BlockSpec((1, N_PAIRS, rb, LANES, MOBA_BLOCK), lambda b, i: (b, 0, i, 0, 0)),
        ],
        compiler_params=_cparams(("arbitrary", "arbitrary")),
        name="inproj",
    )(x, mod3, g_pre, w_in_p, rope_a, rope_m, g_cq, w_uq_n, w_uq_p, g_ckv, w_ukv_p)
    return outs


def _nt_dot(a, b, **kw):
    return lax.dot_general(a, b, (((1,), (1,)), ((), ())), preferred_element_type=F32, **kw)


def _softmax_first(st, vt, m_ref, l_ref, acc_ref, idx):
    m = jnp.max(st, axis=0, keepdims=True)
    p = jnp.exp(st - m)
    m_ref[idx] = m
    l_ref[idx] = jnp.sum(p, axis=0, keepdims=True)
    acc_ref[idx] = jnp.dot(vt, p.astype(BF16), preferred_element_type=F32)


def _softmax_update(st, vt, bias, m_ref, l_ref, acc_ref, idx):
    m_old = m_ref[idx]
    cmax = jnp.max(st, axis=0, keepdims=True)
    if bias is not None:
        cmax = cmax + bias
    m_new = jnp.maximum(m_old, cmax)
    alpha = jnp.exp(m_old - m_new)
    shift = m_new if bias is None else m_new - bias
    p = jnp.exp(st - shift)
    m_ref[idx] = m_new
    l_ref[idx] = alpha * l_ref[idx] + jnp.sum(p, axis=0, keepdims=True)
    acc_ref[idx] = alpha * acc_ref[idx] + jnp.dot(vt, p.astype(BF16), preferred_element_type=F32)


def _moba_kernel(q_ref, k_ref, vt_ref, kmean_ref, o_ref, qh_ref, bias_ref, m_ref, l_ref, acc_ref):
    L = MOBA_BLOCK
    nb = kmean_ref.shape[1]
    i = pl.program_id(2)
    q = q_ref[0]
    lane = lax.broadcasted_iota(jnp.int32, (L, LANES), 1)
    kmean = kmean_ref[0, :, 0, :]
    blk = lax.broadcasted_iota(jnp.int32, (nb, L), 0)
    past = blk < i

    for hd in range(HEADS_PER_STEP):
        own = (lane >= hd * MOBA_HEAD_DIM) & (lane < (hd + 1) * MOBA_HEAD_DIM)
        qh = jnp.where(own, q, jnp.zeros_like(q))
        qh_ref[hd] = qh
        gate = _nt_dot(kmean, qh.astype(F32), precision=lax.Precision.HIGHEST)
        gate = jnp.where(past, gate, -jnp.inf)
        rank = jnp.zeros((nb, L), jnp.int32)
        for j in range(nb):
            gj = gate[j:j + 1, :]
            ahead = (gj > gate) | ((gj == gate) & (j < blk))
            rank = rank + ahead.astype(jnp.int32)
        bias_ref[hd] = jnp.where(past & (rank < MOBA_TOPK), 0.0, NEG)

    key = lax.broadcasted_iota(jnp.int32, (L, L), 0)
    qry = lax.broadcasted_iota(jnp.int32, (L, L), 1)
    k_own = k_ref[0, pl.ds(pl.multiple_of(i * L, L), L), :]
    for hd in range(HEADS_PER_STEP):
        st = jnp.where(key <= qry, _nt_dot(k_own, qh_ref[hd]), NEG)
        vt = vt_ref[0, 0, i, pl.ds(hd * MOBA_HEAD_DIM, MOBA_HEAD_DIM), :]
        _softmax_first(st, vt, m_ref, l_ref, acc_ref, hd)

    def body(j, carry):
        kj = k_ref[0, pl.ds(pl.multiple_of(j * L, L), L), :]
        for hd in range(HEADS_PER_STEP):
            st = _nt_dot(kj, qh_ref[hd])
            vt = vt_ref[0, 0, j, pl.ds(hd * MOBA_HEAD_DIM, MOBA_HEAD_DIM), :]
            _softmax_update(st, vt, bias_ref[hd, pl.ds(j, 1), :], m_ref, l_ref, acc_ref, hd)
        return carry

    lax.fori_loop(0, i, body, 0)

    ot = jnp.concatenate([acc_ref[hd] / l_ref[hd] for hd in range(HEADS_PER_STEP)], axis=0)
    o_ref[0] = ot.T


def _moba(qa, ka, vat, kmean):
    B, S, _ = qa.shape
    L = MOBA_BLOCK
    nb = S // L
    return pl.pallas_call(
        _moba_kernel,
        out_shape=jax.ShapeDtypeStruct((B, S, MOBA_WIDTH), F32),
        grid=(B, N_PAIRS, nb),
        in_specs=[
            pl.BlockSpec((1, L, LANES), lambda b, p, i: (b, i, p)),
            pl.BlockSpec((1, S, LANES), lambda b, p, i: (b, 0, p)),
            pl.BlockSpec((1, 1, nb, LANES, L), lambda b, p, i: (b, p, 0, 0, 0)),
            pl.BlockSpec((1, nb, 1, LANES), lambda b, p, i: (b, 0, 0, p)),
        ],
        out_specs=pl.BlockSpec((1, L, LANES), lambda b, p, i: (b, i, p)),
        scratch_shapes=[
            pltpu.VMEM((HEADS_PER_STEP, L, LANES), BF16),
            pltpu.VMEM((HEADS_PER_STEP, nb, L), F32),
            pltpu.VMEM((HEADS_PER_STEP, 1, L), F32),
            pltpu.VMEM((HEADS_PER_STEP, 1, L), F32),
            pltpu.VMEM((HEADS_PER_STEP, MOBA_HEAD_DIM, L), F32),
        ],
        compiler_params=_cparams(("arbitrary", "arbitrary", "arbitrary")),
        name="moba",
    )(qa, ka, vat, kmean)


def _mla_kernel(q_ref, k_ref, vt_ref, o_ref, m_ref, l_ref, acc_ref):
    L = MOBA_BLOCK
    i = pl.program_id(2)
    key = lax.broadcasted_iota(jnp.int32, (L, L), 0)
    qry = lax.broadcasted_iota(jnp.int32, (L, L), 1)
    k_own = k_ref[0, 0, pl.ds(pl.multiple_of(i * L, L), L), :]
    for hd in range(HEADS_PER_STEP):
        st = jnp.where(key <= qry, _nt_dot(k_own, q_ref[0, hd]), NEG)
        vt = vt_ref[0, 0, i, pl.ds(hd * MLA_V_DIM, MLA_V_DIM), :]
        _softmax_first(st, vt, m_ref, l_ref, acc_ref, hd)

    def body(j, carry):
        kj = k_ref[0, 0, pl.ds(pl.multiple_of(j * L, L), L), :]
        for hd in range(HEADS_PER_STEP):
            st = _nt_dot(kj, q_ref[0, hd])
            vt = vt_ref[0, 0, j, pl.ds(hd * MLA_V_DIM, MLA_V_DIM), :]
            _softmax_update(st, vt, None, m_ref, l_ref, acc_ref, hd)
        return carry

    lax.fori_loop(0, i, body, 0)

    ot = jnp.concatenate([acc_ref[hd] / l_ref[hd] for hd in range(HEADS_PER_STEP)], axis=0)
    o_ref[0] = ot.T


def _mla(qcat, kcat, vmt):
    B, _, S, W = qcat.shape
    L = MOBA_BLOCK
    nb = S // L
    return pl.pallas_call(
        _mla_kernel,
        out_shape=jax.ShapeDtypeStruct((B, S, MLA_WIDTH), F32),
        grid=(B, N_PAIRS, nb),
        in_specs=[
            pl.BlockSpec((1, HEADS_PER_STEP, L, W), lambda b, p, i: (b, p, i, 0)),
            pl.BlockSpec((1, 1, S, W), lambda b, p, i: (b, p, 0, 0)),
            pl.BlockSpec((1, 1, nb, LANES, L), lambda b, p, i: (b, p, 0, 0, 0)),
        ],
        out_specs=pl.BlockSpec((1, L, LANES), lambda b, p, i: (b, i, p)),
        scratch_shapes=[
            pltpu.VMEM((HEADS_PER_STEP, 1, L), F32),
            pltpu.VMEM((HEADS_PER_STEP, 1, L), F32),
            pltpu.VMEM((HEADS_PER_STEP, MLA_V_DIM, L), F32),
        ],
        compiler_params=_cparams(("arbitrary", "arbitrary", "arbitrary")),
        name="mla",
    )(qcat, kcat, vmt)


def _postattn_kernel(x_ref, om_ref, ol_ref, mod_ref, gom_ref, gol_ref, wo_ref, gpost_ref, gpre_ref,
                     wr_ref, br_ref, x1_ref, h2_ref, ri_ref, rw_ref):
    tm = x_ref.shape[1]
    gt_a = mod_ref[0, 2:3, :]
    sh_f = mod_ref[0, 3:4, :]
    sc_f = mod_ref[0, 4:5, :]
    a = jnp.concatenate([_rms(om_ref[0], gom_ref[...]), _rms(ol_ref[0], gol_ref[...])], axis=-1)
    o = jnp.dot(a.astype(BF16), wo_ref[...], preferred_element_type=F32)
    x1 = x_ref[0] + gt_a * _rms(o, gpost_ref[...])
    x1_ref[0] = x1
    h2 = _rms(x1, gpre_ref[...]) * (1.0 + sc_f) + sh_f
    h2_ref[0] = h2

    logit = jnp.dot(h2, wr_ref[...], preferred_element_type=F32, precision=lax.Precision.HIGHEST)
    biased = logit + br_ref[...]
    lane = lax.broadcasted_iota(jnp.int32, (tm, LANES), 1)
    big = jnp.int32(LANES)

    def first_argmax(v):
        mx = jnp.max(v, axis=-1, keepdims=True)
        return jnp.min(jnp.where(v == mx, lane, big), axis=-1, keepdims=True)

    def pick(v, idx):
        return jnp.sum(jnp.where(lane == idx, v, 0.0), axis=-1, keepdims=True)

    gmask = (lane >= N_EXPERTS) & (lane < N_EXPERTS + N_GROUPS)
    gl = jnp.where(gmask, logit, -jnp.inf)
    pg = jnp.exp(gl - jnp.max(gl, axis=-1, keepdims=True))
    g_lane = first_argmax(jnp.where(gmask, biased, -jnp.inf))
    g_prob = pick(pg, g_lane) / jnp.sum(pg, axis=-1, keepdims=True)
    e_lo = (g_lane - N_EXPERTS) * EXPERTS_PER_GROUP
    emask = (lane >= e_lo) & (lane < e_lo + EXPERTS_PER_GROUP)
    el = jnp.where(emask, logit, -jnp.inf)
    pe = jnp.exp(el - jnp.max(el, axis=-1, keepdims=True))
    eb = jnp.where(emask, biased, -jnp.inf)
    i1 = first_argmax(eb)
    i2 = first_argmax(jnp.where(lane == i1, -jnp.inf, eb))
    p1 = pick(pe, i1)
    p2 = pick(pe, i2)
    denom = p1 + p2
    w1 = p1 / denom * g_prob
    w2 = p2 / denom * g_prob
    ri_ref[0] = jnp.where(lane == 0, i1, jnp.where(lane == 1, i2, 0))
    rw_ref[0] = jnp.where(lane == 0, w1, jnp.where(lane == 1, w2, 0.0))


def _postattn(x, o_moba, o_mla, mod3, g_om, g_ol, w_o, g_post, g_pre, w_r, b_r):
    B, S, D = x.shape
    tm = min(TM_IN, S)
    const2 = lambda b, i: (0, 0)
    tok3 = lambda b, i: (b, i, 0)
    return pl.pallas_call(
        _postattn_kernel,
        out_shape=[
            jax.ShapeDtypeStruct((B, S, D), F32),
            jax.ShapeDtypeStruct((B, S, D), F32),
            jax.ShapeDtypeStruct((B, S, LANES), jnp.int32),
            jax.ShapeDtypeStruct((B, S, LANES), F32),
        ],
        grid=(B, S // tm),
        in_specs=[
            pl.BlockSpec((1, tm, D), tok3),
            pl.BlockSpec((1, tm, MOBA_WIDTH), tok3),
            pl.BlockSpec((1, tm, MLA_WIDTH), tok3),
            pl.BlockSpec((1, 6, D), lambda b, i: (b, 0, 0)),
            pl.BlockSpec((1, MOBA_WIDTH), const2),
            pl.BlockSpec((1, MLA_WIDTH), const2),
            pl.BlockSpec(w_o.shape, const2),
            pl.BlockSpec((1, D), const2),
            pl.BlockSpec((1, D), const2),
            pl.BlockSpec(w_r.shape, const2),
            pl.BlockSpec((1, LANES), const2),
        ],
        out_specs=[
            pl.BlockSpec((1, tm, D), tok3),
            pl.BlockSpec((1, tm, D), tok3),
            pl.BlockSpec((1, tm, LANES), tok3),
            pl.BlockSpec((1, tm, LANES), tok3),
        ],
        compiler_params=_cparams(("arbitrary", "arbitrary")),
        name="postattn",
    )(x, o_moba, o_mla, mod3, g_om, g_ol, w_o, g_post, g_pre, w_r, b_r)


GATHER_UNROLL = 8


def _start_row_gather(idx_ref, base, n_rows, src_hbm, dst_ref, sem):
    def body(r, carry):
        row = idx_ref[base + r]
        pltpu.make_async_copy(src_hbm.at[pl.ds(row, 1), :], dst_ref.at[pl.ds(r, 1), :], sem).start()
        return carry
    lax.fori_loop(0, n_rows, body, 0, unroll=GATHER_UNROLL)


def _wait_row_gather(n_rows, src_hbm, dst_ref, sem):
    pltpu.make_async_copy(src_hbm.at[pl.ds(0, n_rows), :], dst_ref, sem).wait()


def _moe_kernel(te_ref, nu_ref, tok_ref, h_hbm, wgu_ref, wd_ref, y_ref, xbuf, sem):
    tm = y_ref.shape[0]
    i = pl.program_id(0)
    n_used = nu_ref[0]
    slot = i % 2

    @pl.when((i == 0) & (n_used > 0))
    def _():
        _start_row_gather(tok_ref, 0, tm, h_hbm, xbuf.at[0], sem.at[0])

    @pl.when(i + 1 < n_used)
    def _():
        _start_row_gather(tok_ref, (i + 1) * tm, tm, h_hbm, xbuf.at[1 - slot], sem.at[1 - slot])

    @pl.when(i < n_used)
    def _():
        _wait_row_gather(tm, h_hbm, xbuf.at[slot], sem.at[slot])
        xb = xbuf[slot].astype(BF16)
        gu = jnp.dot(xb, wgu_ref[0], preferred_element_type=F32)
        g = gu[:, :D_EXPERT]
        u = gu[:, D_EXPERT:]
        hid = g * jax.nn.sigmoid(g) * u
        y_ref[...] = jnp.dot(hid.astype(BF16), wd_ref[0], preferred_element_type=F32)

    @pl.when(i >= n_used)
    def _():
        y_ref[...] = jnp.zeros_like(y_ref)


def _moe(tile_expert, n_used, tok_sorted, h2, w_gu, w_d):
    T, D = h2.shape
    n_tiles = tile_expert.shape[0]
    tm = TM_MOE
    grid_spec = pltpu.PrefetchScalarGridSpec(
        num_scalar_prefetch=3,
        grid=(n_tiles,),
        in_specs=[
            pl.BlockSpec(memory_space=pl.ANY),
            pl.BlockSpec((1, D, 2 * D_EXPERT), lambda i, te, nu, tok: (te[i], 0, 0)),
            pl.BlockSpec((1, D_EXPERT, D), lambda i, te, nu, tok: (te[i], 0, 0)),
        ],
        out_specs=pl.BlockSpec((tm, D), lambda i, te, nu, tok: (i, 0)),
        scratch_shapes=[pltpu.VMEM((2, tm, D), F32), pltpu.SemaphoreType.DMA((2,))],
    )
    return pl.pallas_call(
        _moe_kernel,
        out_shape=jax.ShapeDtypeStruct((n_tiles * tm, D), F32),
        grid_spec=grid_spec,
        compiler_params=_cparams(("arbitrary",)),
        name="moe",
    )(tile_expert, n_used, tok_sorted, h2, w_gu, w_d)


def _combine_kernel(pos_ref, y_hbm, x1_ref, rw_ref, mod_ref, g_ref, o_ref, ybuf, sem):
    tm = x1_ref.shape[1]
    n_j = pl.num_programs(1)
    step = pl.program_id(0) * n_j + pl.program_id(1)
    n_steps = pl.num_programs(0) * n_j
    slot = step % 2

    @pl.when(step == 0)
    def _():
        _start_row_gather(pos_ref, 0, 2 * tm, y_hbm, ybuf.at[0], sem.at[0])

    @pl.when(step + 1 < n_steps)
    def _():
        _start_row_gather(pos_ref, (step + 1) * 2 * tm, 2 * tm, y_hbm, ybuf.at[1 - slot], sem.at[1 - slot])

    _wait_row_gather(2 * tm, y_hbm, ybuf.at[slot], sem.at[slot])
    w1 = rw_ref[0, :, 0:1]
    w2 = rw_ref[0, :, 1:2]
    y = w1 * ybuf[slot, 0:tm, :] + w2 * ybuf[slot, tm:2 * tm, :]
    gt_f = mod_ref[0, 5:6, :]
    o_ref[0] = x1_ref[0] + gt_f * _rms(y, g_ref[...])


def _combine(pos, y_sorted, x1, route_w, mod3, g_post):
    B, S, D = x1.shape
    tm = min(TM_CMB, S)
    grid_spec = pltpu.PrefetchScalarGridSpec(
        num_scalar_prefetch=1,
        grid=(B, S // tm),
        in_specs=[
            pl.BlockSpec(memory_space=pl.ANY),
            pl.BlockSpec((1, tm, D), lambda b, j, pos: (b, j, 0)),
            pl.BlockSpec((1, tm, LANES), lambda b, j, pos: (b, j, 0)),
            pl.BlockSpec((1, 6, D), lambda b, j, pos: (b, 0, 0)),
            pl.BlockSpec((1, D), lambda b, j, pos: (0, 0)),
        ],
        out_specs=pl.BlockSpec((1, tm, D), lambda b, j, pos: (b, j, 0)),
        scratch_shapes=[pltpu.VMEM((2, 2 * tm, D), F32), pltpu.SemaphoreType.DMA((2,))],
    )
    return pl.pallas_call(
        _combine_kernel,
        out_shape=jax.ShapeDtypeStruct((B, S, D), F32),
        grid_spec=grid_spec,
        compiler_params=_cparams(("arbitrary", "arbitrary")),
        name="combine",
    )(pos, y_sorted, x1, route_w, mod3, g_post)


def _rope_table(positions, dim, period_fill):
    half = dim // 2
    inv_freq = jnp.exp(-math.log(ROPE_THETA) * jnp.arange(half, dtype=F32) * (2.0 / dim))
    ang = positions.astype(F32)[..., None] * inv_freq
    cos, sin = jnp.cos(ang), jnp.sin(ang)
    B, S = positions.shape
    rest = period_fill - dim
    one = jnp.ones((B, S, rest), F32)
    zero_r = jnp.zeros((B, S, rest), F32)
    zero_h = jnp.zeros((B, S, half), F32)
    c = jnp.concatenate([cos, cos, one], axis=-1)
    s1 = jnp.concatenate([-sin, zero_h, zero_r], axis=-1)
    s2 = jnp.concatenate([zero_h, sin, zero_r], axis=-1)
    reps = LANES // period_fill
    return jnp.concatenate([jnp.tile(t, (1, 1, reps)) for t in (c, s1, s2)], axis=-1)


def _routing_metadata(eid, tm):
    T = eid.shape[0]
    n_pairs = 2 * T
    n_tiles = n_pairs // tm + N_EXPERTS
    flat = eid.reshape(-1)
    onehot = (flat[:, None] == jnp.arange(N_EXPERTS, dtype=jnp.int32)[None, :]).astype(jnp.int32)
    csum = jnp.cumsum(onehot, axis=0)
    counts = csum[-1]
    rank = jnp.sum(onehot * csum, axis=1) - 1
    tiles_per = (counts + tm - 1) // tm
    tile_end = jnp.cumsum(tiles_per)
    tile_start = tile_end - tiles_per
    n_used = tile_end[-1]
    pos_pair = (jnp.sum(onehot * tile_start[None, :], axis=1) * tm + rank).astype(jnp.int32)
    tile_ids = jnp.arange(n_tiles, dtype=jnp.int32)
    te = jnp.minimum(jnp.sum((tile_ids[:, None] >= tile_end[None, :]).astype(jnp.int32), axis=1), N_EXPERTS - 1)
    last_e = jnp.sum(jnp.where(tile_ids == n_used - 1, te, 0))
    tile_expert = jnp.where(tile_ids < n_used, te, last_e).astype(jnp.int32)
    tok_sorted = jnp.zeros((n_tiles * tm,), jnp.int32).at[pos_pair].set(
        jnp.arange(n_pairs, dtype=jnp.int32) // 2, unique_indices=True)
    return tile_expert, n_used.reshape(1).astype(jnp.int32), tok_sorted, pos_pair


def kernel(x, c, positions, w_ada, b_ada, g_pre_attn, g_post_attn, g_pre_ffn, g_post_ffn, w_in, g_cq, w_uq,
           g_ckv, w_ukv, g_out_moba, g_out_mla, w_o, w_group, b_group, w_expert, b_expert, w_e_gate, w_e_up,
           w_e_down):
    B, S, D = x.shape
    depth = w_ada.shape[0]
    rope_a = _rope_table(positions, MOBA_ROT_DIMS, MOBA_HEAD_DIM)
    rope_m = _rope_table(positions, MLA_ROPE_DIM, MLA_ROPE_DIM)
    row = lambda v: v.reshape(1, -1)

    for l in range(depth):
        n_main = 3 * MOBA_WIDTH + MLA_Q_RANK + MLA_KV_RANK
        w_in_p = jnp.concatenate([w_in[l][:, :n_main], jnp.tile(w_in[l][:, n_main:], (1, LANES // MLA_ROPE_DIM))],
                                 axis=1).astype(BF16)
        wq = w_uq[l].reshape(MLA_Q_RANK, MLA_HEADS, MLA_NOPE_DIM + MLA_ROPE_DIM)
        wq_n = wq[:, :, :MLA_NOPE_DIM]
        zero_n = jnp.zeros_like(wq_n)
        even = (jnp.arange(MLA_HEADS) % 2 == 0)[None, :, None]
        w_uq_n = jnp.concatenate([jnp.where(even, wq_n, zero_n), jnp.where(even, zero_n, wq_n)], axis=-1)
        w_uq_n = w_uq_n.reshape(MLA_Q_RANK, MLA_HEADS * LANES).astype(BF16)
        w_uq_p = wq[:, :, MLA_NOPE_DIM:].reshape(MLA_Q_RANK, MLA_HEADS * MLA_ROPE_DIM).astype(BF16)
        wkv = w_ukv[l].reshape(MLA_KV_RANK, MLA_HEADS, MLA_NOPE_DIM + MLA_V_DIM)
        w_ukv_p = jnp.concatenate([wkv[:, :, :MLA_NOPE_DIM].reshape(MLA_KV_RANK, -1),
                                   wkv[:, :, MLA_NOPE_DIM:].reshape(MLA_KV_RANK, -1)], axis=1).astype(BF16)
        w_r = jnp.concatenate([w_expert[l], w_group[l],
                               jnp.zeros((D, LANES - N_EXPERTS - N_GROUPS), F32)], axis=1)
        b_r = jnp.concatenate([b_expert[l], b_group[l], jnp.zeros((LANES - N_EXPERTS - N_GROUPS,), F32)]).reshape(1, LANES)
        w_gu = jnp.concatenate([w_e_gate[l], w_e_up[l]], axis=-1).astype(BF16)
        w_d = w_e_down[l].astype(BF16)

        mod3 = _adaln(c, w_ada[l], b_ada[l]).reshape(B, 6, D)

        qa, ka, vat, kmean, qcat, kcat, vmt = _inproj(
            x, mod3, row(g_pre_attn[l]), w_in_p, rope_a, rope_m, row(g_cq[l]), w_uq_n, w_uq_p,
            row(g_ckv[l]), w_ukv_p)
        o_moba = _moba(qa, ka, vat, kmean)
        o_mla = _mla(qcat, kcat, vmt)
        x1, h2, route_i, route_w = _postattn(
            x, o_moba, o_mla, mod3, row(g_out_moba[l]), row(g_out_mla[l]), w_o[l].astype(BF16),
            row(g_post_attn[l]), row(g_pre_ffn[l]), w_r, b_r)

        eid = route_i.reshape(B * S, LANES)[:, :2]
        tile_expert, n_used, tok_sorted, pos_pair = _routing_metadata(eid, TM_MOE)
        y_sorted = _moe(tile_expert, n_used, tok_sorted, h2.reshape(B * S, D), w_gu, w_d)
        tmc = min(TM_CMB, S)
        pos = pos_pair.reshape(B * S // tmc, tmc, 2).transpose(0, 2, 1).reshape(-1)
        x = _combine(pos, y_sorted, x1, route_w, mod3, row(g_post_ffn[l]))
    return x
```

```python
import functools
import math

import jax
import jax.numpy as jnp
import numpy as np
from jax import lax
from jax.experimental import pallas as pl
from jax.experimental.pallas import tpu as pltpu

F32 = jnp.float32
BF16 = jnp.bfloat16

MOBA_HEADS = 8
MOBA_HEAD_DIM = 64
MOBA_WIDTH = MOBA_HEADS * MOBA_HEAD_DIM
MOBA_BLOCK = 256
MOBA_TOPK = 3
MOBA_ROT_DIMS = MOBA_HEAD_DIM // 4
MLA_HEADS = 8
MLA_Q_RANK = 256
MLA_KV_RANK = 128
MLA_NOPE_DIM = 64
MLA_ROPE_DIM = 32
MLA_V_DIM = 64
MLA_WIDTH = MLA_HEADS * MLA_V_DIM
ROPE_THETA = 500000.0
N_GROUPS = 4
EXPERTS_PER_GROUP = 8
N_EXPERTS = N_GROUPS * EXPERTS_PER_GROUP
D_EXPERT = 256
EPS = 1e-6
NEG = -1e30
LOG2E = math.log2(math.e)

LANES = 128
HEADS_PER_STEP = LANES // MOBA_HEAD_DIM
N_PAIRS = MOBA_HEADS // HEADS_PER_STEP
VMEM_LIMIT = 56 * 1024 * 1024

TM_IN = 512
KV_GROUP = 4
TM_MOE = 256
TM_CMB = 256


def _cparams(sem):
    return pltpu.CompilerParams(dimension_semantics=sem, vmem_limit_bytes=VMEM_LIMIT)


def _rms(x, g):
    return x * lax.rsqrt(jnp.mean(x * x, axis=-1, keepdims=True) + EPS) * g


def _adaln_kernel(c_ref, w_ref, b_ref, o_ref):
    c = c_ref[...]
    ca = c * jax.nn.sigmoid(c)
    o_ref[...] = jnp.dot(ca, w_ref[...], preferred_element_type=F32,
                         precision=lax.Precision.HIGHEST) + b_ref[...]


def _adaln(c, w, b):
    B, D = c.shape
    N = w.shape[1]
    tn = 1536 if N % 1536 == 0 else N
    return pl.pallas_call(
        _adaln_kernel,
        out_shape=jax.ShapeDtypeStruct((B, N), F32),
        grid=(N // tn,),
        in_specs=[pl.BlockSpec((B, D), lambda j: (0, 0)),
                  pl.BlockSpec((D, tn), lambda j: (0, j)),
                  pl.BlockSpec((1, tn), lambda j: (0, j))],
        out_specs=pl.BlockSpec((B, tn), lambda j: (0, j)),
        compiler_params=_cparams(("arbitrary",)),
        name="adaln",
    )(c, w, b.reshape(1, N))


def _rope128(xc, tab_ref, shift):
    c = tab_ref[0, :, 0:LANES]
    s1 = tab_ref[0, :, LANES:2 * LANES]
    s2 = tab_ref[0, :, 2 * LANES:3 * LANES]
    return xc * c + pltpu.roll(xc, LANES - shift, axis=1) * s1 + pltpu.roll(xc, shift, axis=1) * s2


def _store_vt(src, col0, vt_ref):
    for p in range(N_PAIRS):
        for r in range(src.shape[0] // MOBA_BLOCK):
            blk = src[r * MOBA_BLOCK:(r + 1) * MOBA_BLOCK, col0 + p * LANES:col0 + (p + 1) * LANES]
            vt_ref[0, p, r] = blk.T.astype(BF16)


def _inproj_kernel(x_ref, mod_ref, gpre_ref, win_ref, ra_ref, rm_ref, gcq_ref, wuqn_ref, wuqp_ref,
                   gckv_ref, wukv_ref,
                   qa_ref, ka_ref, vat_ref, kmean_ref, qcat_ref, kcat_ref, vmt_ref):
    tm = x_ref.shape[1]
    x = x_ref[0]
    sh = mod_ref[0, 0:1, :]
    sc = mod_ref[0, 1:2, :]
    h = _rms(x, gpre_ref[...]) * (1.0 + sc) + sh
    z = jnp.dot(h.astype(BF16), win_ref[...], preferred_element_type=F32)

    moba_scale = MOBA_HEAD_DIM ** -0.5 * LOG2E
    half_a = MOBA_ROT_DIMS // 2
    n_chunks = MOBA_WIDTH // LANES
    for cidx in range(n_chunks):
        lo = cidx * LANES
        q = _rope128(z[:, lo:lo + LANES], ra_ref, half_a)
        qa_ref[0, :, lo:lo + LANES] = (q * moba_scale).astype(BF16)
        k = _rope128(z[:, MOBA_WIDTH + lo:MOBA_WIDTH + lo + LANES], ra_ref, half_a)
        ka_ref[0, :, lo:lo + LANES] = k.astype(BF16)
        for r in range(tm // MOBA_BLOCK):
            kmean_ref[0, r, :, lo:lo + LANES] = jnp.mean(
                k[r * MOBA_BLOCK:(r + 1) * MOBA_BLOCK], axis=0, keepdims=True)
    _store_vt(z, 2 * MOBA_WIDTH, vat_ref)

    o_cq = 3 * MOBA_WIDTH
    o_ckv = o_cq + MLA_Q_RANK
    o_kpe = o_ckv + MLA_KV_RANK
    mla_scale = (MLA_NOPE_DIM + MLA_ROPE_DIM) ** -0.5 * LOG2E
    half_m = MLA_ROPE_DIM // 2

    cqn = _rms(z[:, o_cq:o_cq + MLA_Q_RANK], gcq_ref[...]).astype(BF16)
    qn = jnp.dot(cqn, wuqn_ref[...], preferred_element_type=F32) * mla_scale
    qp = jnp.dot(cqn, wuqp_ref[...], preferred_element_type=F32)
    lane = lax.broadcasted_iota(jnp.int32, (tm, LANES), 1)
    heads_per_chunk = LANES // MLA_ROPE_DIM
    qp_chunks = [_rope128(qp[:, g * LANES:(g + 1) * LANES], rm_ref, half_m) * mla_scale
                 for g in range(MLA_HEADS // heads_per_chunk)]
    for hd in range(MLA_HEADS):
        qcat_ref[0, hd, :, 0:LANES] = qn[:, hd * LANES:(hd + 1) * LANES].astype(BF16)
        sub = hd % heads_per_chunk
        keep = (lane >= sub * MLA_ROPE_DIM) & (lane < (sub + 1) * MLA_ROPE_DIM)
        qcat_ref[0, hd, :, LANES:2 * LANES] = jnp.where(keep, qp_chunks[hd // heads_per_chunk], 0.0).astype(BF16)

    ckvn = _rms(z[:, o_ckv:o_ckv + MLA_KV_RANK], gckv_ref[...]).astype(BF16)
    kv = jnp.dot(ckvn, wukv_ref[...], preferred_element_type=F32)
    kpe = _rope128(z[:, o_kpe:o_kpe + LANES], rm_ref, half_m).astype(BF16)
    for p in range(N_PAIRS):
        kcat_ref[0, p, :, 0:LANES] = kv[:, p * LANES:(p + 1) * LANES].astype(BF16)
        kcat_ref[0, p, :, LANES:2 * LANES] = kpe
    _store_vt(kv, MLA_WIDTH, vmt_ref)


def _inproj(x, mod3, g_pre, w_in_p, rope_a, rope_m, g_cq, w_uq_n, w_uq_p, g_ckv, w_ukv_p):
    B, S, D = x.shape
    tm = min(TM_IN, S)
    nb = S // MOBA_BLOCK
    rb = tm // MOBA_BLOCK
    const2 = lambda b, i: (0, 0)
    tok3 = lambda b, i: (b, i, 0)
    outs = pl.pallas_call(
        _inproj_kernel,
        out_shape=[
            jax.ShapeDtypeStruct((B, S, MOBA_WIDTH), BF16),
            jax.ShapeDtypeStruct((B, S, MOBA_WIDTH), BF16),
            jax.ShapeDtypeStruct((B, N_PAIRS, nb, LANES, MOBA_BLOCK), BF16),
            jax.ShapeDtypeStruct((B, nb, 1, MOBA_WIDTH), F32),
            jax.ShapeDtypeStruct((B, MLA_HEADS, S, 2 * LANES), BF16),
            jax.ShapeDtypeStruct((B, N_PAIRS, S, 2 * LANES), BF16),
            jax.ShapeDtypeStruct((B, N_PAIRS, nb, LANES, MOBA_BLOCK), BF16),
        ],
        grid=(B, S // tm),
        in_specs=[
            pl.BlockSpec((1, tm, D), tok3),
            pl.BlockSpec((1, 6, D), lambda b, i: (b, 0, 0)),
            pl.BlockSpec((1, D), const2),
            pl.BlockSpec(w_in_p.shape, const2),
            pl.BlockSpec((1, tm, 3 * LANES), tok3),
            pl.BlockSpec((1, tm, 3 * LANES), tok3),
            pl.BlockSpec((1, MLA_Q_RANK), const2),
            pl.BlockSpec(w_uq_n.shape, const2),
            pl.BlockSpec(w_uq_p.shape, const2),
            pl.BlockSpec((1, MLA_KV_RANK), const2),
            pl.BlockSpec(w_ukv_p.shape, const2),
        ],
        out_specs=[
            pl.BlockSpec((1, tm, MOBA_WIDTH), tok3),
            pl.BlockSpec((1, tm, MOBA_WIDTH), tok3),
            pl.BlockSpec((1, N_PAIRS, rb, LANES, MOBA_BLOCK), lambda b, i: (b, 0, i, 0, 0)),
            pl.BlockSpec((1, rb, 1, MOBA_WIDTH), lambda b, i: (b, i, 0, 0)),
            pl.BlockSpec((1, MLA_HEADS, tm, 2 * LANES), lambda b, i: (b, 0, i, 0)),
            pl.BlockSpec((1, N_PAIRS, tm, 2 * LANES), lambda b, i: (b, 0, i, 0)),
            pl.BlockSpec((1, N_PAIRS, rb, LANES, MOBA_BLOCK), lambda b, i: (b, 0, i, 0, 0)),
        ],
        compiler_params=_cparams(("arbitrary", "arbitrary")),
        name="inproj",
    )(x, mod3, g_pre, w_in_p, rope_a, rope_m, g_cq, w_uq_n, w_uq_p, g_ckv, w_ukv_p)
    return outs


def _nt_dot(a, b, **kw):
    return lax.dot_general(a, b, (((1,), (1,)), ((), ())), preferred_element_type=F32, **kw)


def _group_blocks(i, t):
    out = []
    for g in range(KV_GROUP):
        j = i - KV_GROUP * t - (KV_GROUP - 1 - g)
        out.append((jnp.maximum(j, 0), j >= 0))
    return out


def _softmax_update(sts, vts, biases, m_ref, l_ref, acc_ref, idx):
    m_old = m_ref[idx]
    m_new = m_old
    for st, b in zip(sts, biases):
        m_new = jnp.maximum(m_new, jnp.max(st, axis=0, keepdims=True) + b)
    alpha = jnp.exp2(m_old - m_new)
    l_new = alpha * l_ref[idx]
    pv = None
    for st, vt, b in zip(sts, vts, biases):
        p = jnp.exp2(st - (m_new - b))
        l_new = l_new + jnp.sum(p, axis=0, keepdims=True)
        d = jnp.dot(vt, p.astype(BF16), preferred_element_type=F32)
        pv = d if pv is None else pv + d
    m_ref[idx] = m_new
    l_ref[idx] = l_new
    acc_ref[idx] = alpha * acc_ref[idx] + pv


def _flash_pipeline(i, scores, consume, st_bufs, m_ref, l_ref, acc_ref):
    st_a, st_b = st_bufs
    m_ref[...] = jnp.full(m_ref.shape, -jnp.inf, F32)
    l_ref[...] = jnp.zeros(l_ref.shape, F32)
    acc_ref[...] = jnp.zeros(acc_ref.shape, F32)
    n_groups = (i + KV_GROUP) // KV_GROUP
    scores(0, st_a)

    def body(tt, carry):
        t = 2 * tt
        scores(t + 1, st_b)
        consume(t, st_a)

        @pl.when(t + 1 < n_groups)
        def _():
            scores(t + 2, st_a)
            consume(t + 1, st_b)

        return carry

    lax.fori_loop(0, (n_groups + 1) // 2, body, 0)


def _moba_kernel(q_ref, k_ref, vt_ref, kmean_ref, o_ref, qh_ref, bias_ref, sta_ref, stb_ref, m_ref, l_ref,
                 acc_ref):
    L = MOBA_BLOCK
    nb = kmean_ref.shape[1]
    i = pl.program_id(2)
    q = q_ref[0]
    lane = lax.broadcasted_iota(jnp.int32, (L, LANES), 1)
    kmean = kmean_ref[0, :, 0, :]
    blk = lax.broadcasted_iota(jnp.int32, (nb, L), 0)
    past = blk < i

    for hd in range(HEADS_PER_STEP):
        own = (lane >= hd * MOBA_HEAD_DIM) & (lane < (hd + 1) * MOBA_HEAD_DIM)
        qh = jnp.where(own, q, jnp.zeros_like(q))
        qh_ref[hd] = qh
        gate = _nt_dot(kmean, qh.astype(F32), precision=lax.Precision.HIGHEST)
        gate = jnp.where(past, gate, -jnp.inf)
        rank = jnp.zeros((nb, L), jnp.int32)
        for j in range(nb):
            gj = gate[j:j + 1, :]
            ahead = (gj > gate) | ((gj == gate) & (j < blk))
            rank = rank + ahead.astype(jnp.int32)
        bias_ref[hd] = jnp.where((blk == i) | (past & (rank < MOBA_TOPK)), 0.0, NEG)

    key_minus_qry = (lax.broadcasted_iota(jnp.int32, (L, L), 0) - lax.broadcasted_iota(jnp.int32, (L, L), 1))

    def scores(t, buf):
        for g, (j, _) in enumerate(_group_blocks(i, t)):
            kt = k_ref[0, pl.ds(pl.multiple_of(j * L, L), L), :]
            for hd in range(HEADS_PER_STEP):
                buf[g, hd] = _nt_dot(kt, qh_ref[hd])

    def consume(t, buf):
        blocks = _group_blocks(i, t)
        for hd in range(HEADS_PER_STEP):
            sts, vts, biases = [], [], []
            for g, (j, real) in enumerate(blocks):
                st = buf[g, hd]
                if g == KV_GROUP - 1:
                    st = jnp.where(key_minus_qry <= (i - j) * L, st, NEG)
                sts.append(st)
                vts.append(vt_ref[0, 0, j, pl.ds(hd * MOBA_HEAD_DIM, MOBA_HEAD_DIM), :])
                biases.append(jnp.where(real, bias_ref[hd, pl.ds(j, 1), :], NEG))
            _softmax_update(sts, vts, biases, m_ref, l_ref, acc_ref, hd)

    _flash_pipeline(i, scores, consume, (sta_ref, stb_ref), m_ref, l_ref, acc_ref)

    ot = jnp.concatenate([acc_ref[hd] / l_ref[hd] for hd in range(HEADS_PER_STEP)], axis=0)
    o_ref[0] = ot.T


def _moba(qa, ka, vat, kmean):
    B, S, _ = qa.shape
    L = MOBA_BLOCK
    nb = S // L
    return pl.pallas_call(
        _moba_kernel,
        out_shape=jax.ShapeDtypeStruct((B, S, MOBA_WIDTH), F32),
        grid=(B, N_PAIRS, nb),
        in_specs=[
            pl.BlockSpec((1, L, LANES), lambda b, p, i: (b, i, p)),
            pl.BlockSpec((1, S, LANES), lambda b, p, i: (b, 0, p)),
            pl.BlockSpec((1, 1, nb, LANES, L), lambda b, p, i: (b, p, 0, 0, 0)),
            pl.BlockSpec((1, nb, 1, LANES), lambda b, p, i: (b, 0, 0, p)),
        ],
        out_specs=pl.BlockSpec((1, L, LANES), lambda b, p, i: (b, i, p)),
        scratch_shapes=[
            pltpu.VMEM((HEADS_PER_STEP, L, LANES), BF16),
            pltpu.VMEM((HEADS_PER_STEP, nb, L), F32),
            pltpu.VMEM((KV_GROUP, HEADS_PER_STEP, L, L), F32),
            pltpu.VMEM((KV_GROUP, HEADS_PER_STEP, L, L), F32),
            pltpu.VMEM((HEADS_PER_STEP, 1, L), F32),
            pltpu.VMEM((HEADS_PER_STEP, 1, L), F32),
            pltpu.VMEM((HEADS_PER_STEP, MOBA_HEAD_DIM, L), F32),
        ],
        compiler_params=_cparams(("arbitrary", "arbitrary", "arbitrary")),
        name="moba",
    )(qa, ka, vat, kmean)


def _mla_kernel(q_ref, k_ref, vt_ref, o_ref, sta_ref, stb_ref, m_ref, l_ref, acc_ref):
    L = MOBA_BLOCK
    i = pl.program_id(2)
    key_minus_qry = (lax.broadcasted_iota(jnp.int32, (L, L), 0) - lax.broadcasted_iota(jnp.int32, (L, L), 1))

    def scores(t, buf):
        for g, (j, _) in enumerate(_group_blocks(i, t)):
            kt = k_ref[0, 0, pl.ds(pl.multiple_of(j * L, L), L), :]
            for hd in range(HEADS_PER_STEP):
                buf[g, hd] = _nt_dot(kt, q_ref[0, hd])

    def consume(t, buf):
        blocks = _group_blocks(i, t)
        for hd in range(HEADS_PER_STEP):
            sts, vts, biases = [], [], []
            for g, (j, real) in enumerate(blocks):
                st = buf[g, hd]
                if g == KV_GROUP - 1:
                    st = jnp.where(key_minus_qry <= (i - j) * L, st, NEG)
                sts.append(st)
                vts.append(vt_ref[0, 0, j, pl.ds(hd * MLA_V_DIM, MLA_V_DIM), :])
                biases.append(jnp.where(real, 0.0, NEG))
            _softmax_update(sts, vts, biases, m_ref, l_ref, acc_ref, hd)

    _flash_pipeline(i, scores, consume, (sta_ref, stb_ref), m_ref, l_ref, acc_ref)

    ot = jnp.concatenate([acc_ref[hd] / l_ref[hd] for hd in range(HEADS_PER_STEP)], axis=0)
    o_ref[0] = ot.T


def _mla(qcat, kcat, vmt):
    B, _, S, W = qcat.shape
    L = MOBA_BLOCK
    nb = S // L
    return pl.pallas_call(
        _mla_kernel,
        out_shape=jax.ShapeDtypeStruct((B, S, MLA_WIDTH), F32),
        grid=(B, N_PAIRS, nb),
        in_specs=[
            pl.BlockSpec((1, HEADS_PER_STEP, L, W), lambda b, p, i: (b, p, i, 0)),
            pl.BlockSpec((1, 1, S, W), lambda b, p, i: (b, p, 0, 0)),
            pl.BlockSpec((1, 1, nb, LANES, L), lambda b, p, i: (b, p, 0, 0, 0)),
        ],
        out_specs=pl.BlockSpec((1, L, LANES), lambda b, p, i: (b, i, p)),
        scratch_shapes=[
            pltpu.VMEM((KV_GROUP, HEADS_PER_STEP, L, L), F32),
            pltpu.VMEM((KV_GROUP, HEADS_PER_STEP, L, L), F32),
            pltpu.VMEM((HEADS_PER_STEP, 1, L), F32),
            pltpu.VMEM((HEADS_PER_STEP, 1, L), F32),
            pltpu.VMEM((HEADS_PER_STEP, MLA_V_DIM, L), F32),
        ],
        compiler_params=_cparams(("arbitrary", "arbitrary", "arbitrary")),
        name="mla",
    )(qcat, kcat, vmt)


def _postattn_kernel(x_ref, om_ref, ol_ref, mod_ref, gom_ref, gol_ref, wo_ref, gpost_ref, gpre_ref,
                     wr_ref, br_ref, x1_ref, h2_ref, ri_ref, rw_ref):
    tm = x_ref.shape[1]
    gt_a = mod_ref[0, 2:3, :]
    sh_f = mod_ref[0, 3:4, :]
    sc_f = mod_ref[0, 4:5, :]
    a = jnp.concatenate([_rms(om_ref[0], gom_ref[...]), _rms(ol_ref[0], gol_ref[...])], axis=-1)
    o = jnp.dot(a.astype(BF16), wo_ref[...], preferred_element_type=F32)
    x1 = x_ref[0] + gt_a * _rms(o, gpost_ref[...])
    x1_ref[0] = x1
    h2 = _rms(x1, gpre_ref[...]) * (1.0 + sc_f) + sh_f
    h2_ref[0] = h2

    logit = jnp.dot(h2, wr_ref[...], preferred_element_type=F32, precision=lax.Precision.HIGHEST)
    biased = logit + br_ref[...]
    lane = lax.broadcasted_iota(jnp.int32, (tm, LANES), 1)
    big = jnp.int32(LANES)

    def first_argmax(v):
        mx = jnp.max(v, axis=-1, keepdims=True)
        return jnp.min(jnp.where(v == mx, lane, big), axis=-1, keepdims=True)

    def pick(v, idx):
        return jnp.sum(jnp.where(lane == idx, v, 0.0), axis=-1, keepdims=True)

    gmask = (lane >= N_EXPERTS) & (lane < N_EXPERTS + N_GROUPS)
    gl = jnp.where(gmask, logit, -jnp.inf)
    pg = jnp.exp(gl - jnp.max(gl, axis=-1, keepdims=True))
    g_lane = first_argmax(jnp.where(gmask, biased, -jnp.inf))
    g_prob = pick(pg, g_lane) / jnp.sum(pg, axis=-1, keepdims=True)
    e_lo = (g_lane - N_EXPERTS) * EXPERTS_PER_GROUP
    emask = (lane >= e_lo) & (lane < e_lo + EXPERTS_PER_GROUP)
    el = jnp.where(emask, logit, -jnp.inf)
    pe = jnp.exp(el - jnp.max(el, axis=-1, keepdims=True))
    eb = jnp.where(emask, biased, -jnp.inf)
    i1 = first_argmax(eb)
    i2 = first_argmax(jnp.where(lane == i1, -jnp.inf, eb))
    p1 = pick(pe, i1)
    p2 = pick(pe, i2)
    denom = p1 + p2
    w1 = p1 / denom * g_prob
    w2 = p2 / denom * g_prob
    ri_ref[0] = jnp.where(lane == 0, i1, jnp.where(lane == 1, i2, 0))
    rw_ref[0] = jnp.where(lane == 0, w1, jnp.where(lane == 1, w2, 0.0))


def _postattn(x, o_moba, o_mla, mod3, g_om, g_ol, w_o, g_post, g_pre, w_r, b_r):
    B, S, D = x.shape
    tm = min(TM_IN, S)
    const2 = lambda b, i: (0, 0)
    tok3 = lambda b, i: (b, i, 0)
    return pl.pallas_call(
        _postattn_kernel,
        out_shape=[
            jax.ShapeDtypeStruct((B, S, D), F32),
            jax.ShapeDtypeStruct((B, S, D), F32),
            jax.ShapeDtypeStruct((B, S, LANES), jnp.int32),
            jax.ShapeDtypeStruct((B, S, LANES), F32),
        ],
        grid=(B, S // tm),
        in_specs=[
            pl.BlockSpec((1, tm, D), tok3),
            pl.BlockSpec((1, tm, MOBA_WIDTH), tok3),
            pl.BlockSpec((1, tm, MLA_WIDTH), tok3),
            pl.BlockSpec((1, 6, D), lambda b, i: (b, 0, 0)),
            pl.BlockSpec((1, MOBA_WIDTH), const2),
            pl.BlockSpec((1, MLA_WIDTH), const2),
            pl.BlockSpec(w_o.shape, const2),
            pl.BlockSpec((1, D), const2),
            pl.BlockSpec((1, D), const2),
            pl.BlockSpec(w_r.shape, const2),
            pl.BlockSpec((1, LANES), const2),
        ],
        out_specs=[
            pl.BlockSpec((1, tm, D), tok3),
            pl.BlockSpec((1, tm, D), tok3),
            pl.BlockSpec((1, tm, LANES), tok3),
            pl.BlockSpec((1, tm, LANES), tok3),
        ],
        compiler_params=_cparams(("arbitrary", "arbitrary")),
        name="postattn",
    )(x, o_moba, o_mla, mod3, g_om, g_ol, w_o, g_post, g_pre, w_r, b_r)


GATHER_UNROLL = 8


def _start_row_gather(idx_ref, base, n_rows, src_hbm, dst_ref, sem):
    def body(r, carry):
        row = idx_ref[base + r]
        pltpu.make_async_copy(src_hbm.at[pl.ds(row, 1), :], dst_ref.at[pl.ds(r, 1), :], sem).start()
        return carry
    lax.fori_loop(0, n_rows, body, 0, unroll=GATHER_UNROLL)


def _wait_row_gather(n_rows, src_hbm, dst_ref, sem):
    pltpu.make_async_copy(src_hbm.at[pl.ds(0, n_rows), :], dst_ref, sem).wait()


def _moe_kernel(te_ref, nu_ref, tok_ref, h_hbm, wgu_ref, wd_ref, y_ref, xbuf, sem):
    tm = y_ref.shape[0]
    i = pl.program_id(0)
    n_used = nu_ref[0]
    slot = i % 2

    @pl.when((i == 0) & (n_used > 0))
    def _():
        _start_row_gather(tok_ref, 0, tm, h_hbm, xbuf.at[0], sem.at[0])

    @pl.when(i + 1 < n_used)
    def _():
        _start_row_gather(tok_ref, (i + 1) * tm, tm, h_hbm, xbuf.at[1 - slot], sem.at[1 - slot])

    @pl.when(i < n_used)
    def _():
        _wait_row_gather(tm, h_hbm, xbuf.at[slot], sem.at[slot])
        xb = xbuf[slot].astype(BF16)
        gu = jnp.dot(xb, wgu_ref[0], preferred_element_type=F32)
        g = gu[:, :D_EXPERT]
        u = gu[:, D_EXPERT:]
        hid = g * jax.nn.sigmoid(g) * u
        y_ref[...] = jnp.dot(hid.astype(BF16), wd_ref[0], preferred_element_type=F32)

    @pl.when(i >= n_used)
    def _():
        y_ref[...] = jnp.zeros_like(y_ref)


def _moe(tile_expert, n_used, tok_sorted, h2, w_gu, w_d):
    T, D = h2.shape
    n_tiles = tile_expert.shape[0]
    tm = TM_MOE
    grid_spec = pltpu.PrefetchScalarGridSpec(
        num_scalar_prefetch=3,
        grid=(n_tiles,),
        in_specs=[
            pl.BlockSpec(memory_space=pl.ANY),
            pl.BlockSpec((1, D, 2 * D_EXPERT), lambda i, te, nu, tok: (te[i], 0, 0)),
            pl.BlockSpec((1, D_EXPERT, D), lambda i, te, nu, tok: (te[i], 0, 0)),
        ],
        out_specs=pl.BlockSpec((tm, D), lambda i, te, nu, tok: (i, 0)),
        scratch_shapes=[pltpu.VMEM((2, tm, D), F32), pltpu.SemaphoreType.DMA((2,))],
    )
    return pl.pallas_call(
        _moe_kernel,
        out_shape=jax.ShapeDtypeStruct((n_tiles * tm, D), F32),
        grid_spec=grid_spec,
        compiler_params=_cparams(("arbitrary",)),
        name="moe",
    )(tile_expert, n_used, tok_sorted, h2, w_gu, w_d)


def _combine_kernel(pos_ref, y_hbm, x1_ref, rw_ref, mod_ref, g_ref, o_ref, ybuf, sem):
    tm = x1_ref.shape[1]
    n_j = pl.num_programs(1)
    step = pl.program_id(0) * n_j + pl.program_id(1)
    n_steps = pl.num_programs(0) * n_j
    slot = step % 2

    @pl.when(step == 0)
    def _():
        _start_row_gather(pos_ref, 0, 2 * tm, y_hbm, ybuf.at[0], sem.at[0])

    @pl.when(step + 1 < n_steps)
    def _():
        _start_row_gather(pos_ref, (step + 1) * 2 * tm, 2 * tm, y_hbm, ybuf.at[1 - slot], sem.at[1 - slot])

    _wait_row_gather(2 * tm, y_hbm, ybuf.at[slot], sem.at[slot])
    w1 = rw_ref[0, :, 0:1]
    w2 = rw_ref[0, :, 1:2]
    y = w1 * ybuf[slot, 0:tm, :] + w2 * ybuf[slot, tm:2 * tm, :]
    gt_f = mod_ref[0, 5:6, :]
    o_ref[0] = x1_ref[0] + gt_f * _rms(y, g_ref[...])


def _combine(pos, y_sorted, x1, route_w, mod3, g_post):
    B, S, D = x1.shape
    tm = min(TM_CMB, S)
    grid_spec = pltpu.PrefetchScalarGridSpec(
        num_scalar_prefetch=1,
        grid=(B, S // tm),
        in_specs=[
            pl.BlockSpec(memory_space=pl.ANY),
            pl.BlockSpec((1, tm, D), lambda b, j, pos: (b, j, 0)),
            pl.BlockSpec((1, tm, LANES), lambda b, j, pos: (b, j, 0)),
            pl.BlockSpec((1, 6, D), lambda b, j, pos: (b, 0, 0)),
            pl.BlockSpec((1, D), lambda b, j, pos: (0, 0)),
        ],
        out_specs=pl.BlockSpec((1, tm, D), lambda b, j, pos: (b, j, 0)),
        scratch_shapes=[pltpu.VMEM((2, 2 * tm, D), F32), pltpu.SemaphoreType.DMA((2,))],
    )
    return pl.pallas_call(
        _combine_kernel,
        out_shape=jax.ShapeDtypeStruct((B, S, D), F32),
        grid_spec=grid_spec,
        compiler_params=_cparams(("arbitrary", "arbitrary")),
        name="combine",
    )(pos, y_sorted, x1, route_w, mod3, g_post)


def _rope_table(positions, dim, period_fill):
    half = dim // 2
    inv_freq = jnp.exp(-math.log(ROPE_THETA) * jnp.arange(half, dtype=F32) * (2.0 / dim))
    ang = positions.astype(F32)[..., None] * inv_freq
    cos, sin = jnp.cos(ang), jnp.sin(ang)
    B, S = positions.shape
    rest = period_fill - dim
    one = jnp.ones((B, S, rest), F32)
    zero_r = jnp.zeros((B, S, rest), F32)
    zero_h = jnp.zeros((B, S, half), F32)
    c = jnp.concatenate([cos, cos, one], axis=-1)
    s1 = jnp.concatenate([-sin, zero_h, zero_r], axis=-1)
    s2 = jnp.concatenate([zero_h, sin, zero_r], axis=-1)
    reps = LANES // period_fill
    return jnp.concatenate([jnp.tile(t, (1, 1, reps)) for t in (c, s1, s2)], axis=-1)


def _routing_metadata(eid, tm):
    T = eid.shape[0]
    n_pairs = 2 * T
    n_tiles = n_pairs // tm + N_EXPERTS
    flat = eid.reshape(-1)
    onehot = (flat[:, None] == jnp.arange(N_EXPERTS, dtype=jnp.int32)[None, :]).astype(jnp.int32)
    csum = jnp.cumsum(onehot, axis=0)
    counts = csum[-1]
    rank = jnp.sum(onehot * csum, axis=1) - 1
    tiles_per = (counts + tm - 1) // tm
    tile_end = jnp.cumsum(tiles_per)
    tile_start = tile_end - tiles_per
    n_used = tile_end[-1]
    pos_pair = (jnp.sum(onehot * tile_start[None, :], axis=1) * tm + rank).astype(jnp.int32)
    tile_ids = jnp.arange(n_tiles, dtype=jnp.int32)
    te = jnp.minimum(jnp.sum((tile_ids[:, None] >= tile_end[None, :]).astype(jnp.int32), axis=1), N_EXPERTS - 1)
    last_e = jnp.sum(jnp.where(tile_ids == n_used - 1, te, 0))
    tile_expert = jnp.where(tile_ids < n_used, te, last_e).astype(jnp.int32)
    tok_sorted = jnp.zeros((n_tiles * tm,), jnp.int32).at[pos_pair].set(
        jnp.arange(n_pairs, dtype=jnp.int32) // 2, unique_indices=True)
    return tile_expert, n_used.reshape(1).astype(jnp.int32), tok_sorted, pos_pair


def kernel(x, c, positions, w_ada, b_ada, g_pre_attn, g_post_attn, g_pre_ffn, g_post_ffn, w_in, g_cq, w_uq,
           g_ckv, w_ukv, g_out_moba, g_out_mla, w_o, w_group, b_group, w_expert, b_expert, w_e_gate, w_e_up,
           w_e_down):
    B, S, D = x.shape
    depth = w_ada.shape[0]
    rope_a = _rope_table(positions, MOBA_ROT_DIMS, MOBA_HEAD_DIM)
    rope_m = _rope_table(positions, MLA_ROPE_DIM, MLA_ROPE_DIM)
    row = lambda v: v.reshape(1, -1)

    for l in range(depth):
        n_main = 3 * MOBA_WIDTH + MLA_Q_RANK + MLA_KV_RANK
        w_in_p = jnp.concatenate([w_in[l][:, :n_main], jnp.tile(w_in[l][:, n_main:], (1, LANES // MLA_ROPE_DIM))],
                                 axis=1).astype(BF16)
        wq = w_uq[l].reshape(MLA_Q_RANK, MLA_HEADS, MLA_NOPE_DIM + MLA_ROPE_DIM)
        wq_n = wq[:, :, :MLA_NOPE_DIM]
        zero_n = jnp.zeros_like(wq_n)
        even = (jnp.arange(MLA_HEADS) % 2 == 0)[None, :, None]
        w_uq_n = jnp.concatenate([jnp.where(even, wq_n, zero_n), jnp.where(even, zero_n, wq_n)], axis=-1)
        w_uq_n = w_uq_n.reshape(MLA_Q_RANK, MLA_HEADS * LANES).astype(BF16)
        w_uq_p = wq[:, :, MLA_NOPE_DIM:].reshape(MLA_Q_RANK, MLA_HEADS * MLA_ROPE_DIM).astype(BF16)
        wkv = w_ukv[l].reshape(MLA_KV_RANK, MLA_HEADS, MLA_NOPE_DIM + MLA_V_DIM)
        w_ukv_p = jnp.concatenate([wkv[:, :, :MLA_NOPE_DIM].reshape(MLA_KV_RANK, -1),
                                   wkv[:, :, MLA_NOPE_DIM:].reshape(MLA_KV_RANK, -1)], axis=1).astype(BF16)
        w_r = jnp.concatenate([w_expert[l], w_group[l],
                               jnp.zeros((D, LANES - N_EXPERTS - N_GROUPS), F32)], axis=1)
        b_r = jnp.concatenate([b_expert[l], b_group[l], jnp.zeros((LANES - N_EXPERTS - N_GROUPS,), F32)]).reshape(1, LANES)
        w_gu = jnp.concatenate([w_e_gate[l], w_e_up[l]], axis=-1).astype(BF16)
        w_d = w_e_down[l].astype(BF16)

        mod3 = _adaln(c, w_ada[l], b_ada[l]).reshape(B, 6, D)

        qa, ka, vat, kmean, qcat, kcat, vmt = _inproj(
            x, mod3, row(g_pre_attn[l]), w_in_p, rope_a, rope_m, row(g_cq[l]), w_uq_n, w_uq_p,
            row(g_ckv[l]), w_ukv_p)
        o_moba = _moba(qa, ka, vat, kmean)
        o_mla = _mla(qcat, kcat, vmt)
        x1, h2, route_i, route_w = _postattn(
            x, o_moba, o_mla, mod3, row(g_out_moba[l]), row(g_out_mla[l]), w_o[l].astype(BF16),
            row(g_post_attn[l]), row(g_pre_ffn[l]), w_r, b_r)

        eid = route_i.reshape(B * S, LANES)[:, :2]
        tile_expert, n_used, tok_sorted, pos_pair = _routing_metadata(eid, TM_MOE)
        y_sorted = _moe(tile_expert, n_used, tok_sorted, h2.reshape(B * S, D), w_gu, w_d)
        tmc = min(TM_CMB, S)
        pos = pos_pair.reshape(B * S // tmc, tmc, 2).transpose(0, 2, 1).reshape(-1)
        x = _combine(pos, y_sorted, x1, route_w, mod3, row(g_post_ffn[l]))
    return x
```

```python
import functools
import math

import jax
import jax.numpy as jnp
import numpy as np
from jax import lax
from jax.experimental import pallas as pl
from jax.experimental.pallas import tpu as pltpu

F32 = jnp.float32
BF16 = jnp.bfloat16

MOBA_HEADS = 8
MOBA_HEAD_DIM = 64
MOBA_WIDTH = MOBA_HEADS * MOBA_HEAD_DIM
MOBA_BLOCK = 256
MOBA_TOPK = 3
MOBA_ROT_DIMS = MOBA_HEAD_DIM // 4
MLA_HEADS = 8
MLA_Q_RANK = 256
MLA_KV_RANK = 128
MLA_NOPE_DIM = 64
MLA_ROPE_DIM = 32
MLA_V_DIM = 64
MLA_WIDTH = MLA_HEADS * MLA_V_DIM
ROPE_THETA = 500000.0
N_GROUPS = 4
EXPERTS_PER_GROUP = 8
N_EXPERTS = N_GROUPS * EXPERTS_PER_GROUP
D_EXPERT = 256
EPS = 1e-6
NEG = -1e30
LOG2E = math.log2(math.e)

LANES = 128
HEADS_PER_STEP = LANES // MOBA_HEAD_DIM
N_PAIRS = MOBA_HEADS // HEADS_PER_STEP
VMEM_LIMIT = 56 * 1024 * 1024

TM_IN = 512
KV_GROUP = 4
TM_MOE = 256
TM_CMB = 256


def _cparams(sem):
    return pltpu.CompilerParams(dimension_semantics=sem, vmem_limit_bytes=VMEM_LIMIT)


def _rms(x, g):
    return x * lax.rsqrt(jnp.mean(x * x, axis=-1, keepdims=True) + EPS) * g


def _adaln_kernel(c_ref, w_ref, b_ref, o_ref):
    c = c_ref[...]
    ca = c * jax.nn.sigmoid(c)
    o_ref[...] = jnp.dot(ca, w_ref[...], preferred_element_type=F32,
                         precision=lax.Precision.HIGHEST) + b_ref[...]


def _adaln(c, w, b):
    B, D = c.shape
    N = w.shape[1]
    tn = 1536 if N % 1536 == 0 else N
    return pl.pallas_call(
        _adaln_kernel,
        out_shape=jax.ShapeDtypeStruct((B, N), F32),
        grid=(N // tn,),
        in_specs=[pl.BlockSpec((B, D), lambda j: (0, 0)),
                  pl.BlockSpec((D, tn), lambda j: (0, j)),
                  pl.BlockSpec((1, tn), lambda j: (0, j))],
        out_specs=pl.BlockSpec((B, tn), lambda j: (0, j)),
        compiler_params=_cparams(("arbitrary",)),
        name="adaln",
    )(c, w, b.reshape(1, N))


def _rope128(xc, tab_ref, shift):
    c = tab_ref[0, :, 0:LANES]
    s1 = tab_ref[0, :, LANES:2 * LANES]
    s2 = tab_ref[0, :, 2 * LANES:3 * LANES]
    return xc * c + pltpu.roll(xc, LANES - shift, axis=1) * s1 + pltpu.roll(xc, shift, axis=1) * s2


def _store_vt(src, col0, vt_ref):
    for p in range(N_PAIRS):
        for r in range(src.shape[0] // MOBA_BLOCK):
            blk = src[r * MOBA_BLOCK:(r + 1) * MOBA_BLOCK, col0 + p * LANES:col0 + (p + 1) * LANES]
            vt_ref[0, p, r] = blk.T.astype(BF16)


def _inproj_kernel(x_ref, mod_ref, gpre_ref, win_ref, ra_ref, rm_ref, gcq_ref, wuqn_ref, wuqp_ref,
                   gckv_ref, wukv_ref,
                   qa_ref, ka_ref, vat_ref, kmean_ref, qcat_ref, kcat_ref, vmt_ref):
    tm = x_ref.shape[1]
    x = x_ref[0]
    sh = mod_ref[0, 0:1, :]
    sc = mod_ref[0, 1:2, :]
    h = _rms(x, gpre_ref[...]) * (1.0 + sc) + sh
    z = jnp.dot(h.astype(BF16), win_ref[...], preferred_element_type=F32)

    moba_scale = MOBA_HEAD_DIM ** -0.5 * LOG2E
    half_a = MOBA_ROT_DIMS // 2
    n_chunks = MOBA_WIDTH // LANES
    for cidx in range(n_chunks):
        lo = cidx * LANES
        q = _rope128(z[:, lo:lo + LANES], ra_ref, half_a)
        qa_ref[0, :, lo:lo + LANES] = (q * moba_scale).astype(BF16)
        k = _rope128(z[:, MOBA_WIDTH + lo:MOBA_WIDTH + lo + LANES], ra_ref, half_a)
        ka_ref[0, :, lo:lo + LANES] = k.astype(BF16)
        for r in range(tm // MOBA_BLOCK):
            kmean_ref[0, r, :, lo:lo + LANES] = jnp.mean(
                k[r * MOBA_BLOCK:(r + 1) * MOBA_BLOCK], axis=0, keepdims=True)
    _store_vt(z, 2 * MOBA_WIDTH, vat_ref)

    o_cq = 3 * MOBA_WIDTH
    o_ckv = o_cq + MLA_Q_RANK
    o_kpe = o_ckv + MLA_KV_RANK
    mla_scale = (MLA_NOPE_DIM + MLA_ROPE_DIM) ** -0.5 * LOG2E
    half_m = MLA_ROPE_DIM // 2

    cqn = _rms(z[:, o_cq:o_cq + MLA_Q_RANK], gcq_ref[...]).astype(BF16)
    qn = jnp.dot(cqn, wuqn_ref[...], preferred_element_type=F32) * mla_scale
    qp = jnp.dot(cqn, wuqp_ref[...], preferred_element_type=F32)
    lane = lax.broadcasted_iota(jnp.int32, (tm, LANES), 1)
    heads_per_chunk = LANES // MLA_ROPE_DIM
    qp_chunks = [_rope128(qp[:, g * LANES:(g + 1) * LANES], rm_ref, half_m) * mla_scale
                 for g in range(MLA_HEADS // heads_per_chunk)]
    for hd in range(MLA_HEADS):
        qcat_ref[0, hd, :, 0:LANES] = qn[:, hd * LANES:(hd + 1) * LANES].astype(BF16)
        sub = hd % heads_per_chunk
        keep = (lane >= sub * MLA_ROPE_DIM) & (lane < (sub + 1) * MLA_ROPE_DIM)
        qcat_ref[0, hd, :, LANES:2 * LANES] = jnp.where(keep, qp_chunks[hd // heads_per_chunk], 0.0).astype(BF16)

    ckvn = _rms(z[:, o_ckv:o_ckv + MLA_KV_RANK], gckv_ref[...]).astype(BF16)
    kv = jnp.dot(ckvn, wukv_ref[...], preferred_element_type=F32)
    kpe = _rope128(z[:, o_kpe:o_kpe + LANES], rm_ref, half_m).astype(BF16)
    for p in range(N_PAIRS):
        kcat_ref[0, p, :, 0:LANES] = kv[:, p * LANES:(p + 1) * LANES].astype(BF16)
        kcat_ref[0, p, :, LANES:2 * LANES] = kpe
    _store_vt(kv, MLA_WIDTH, vmt_ref)


def _inproj(x, mod3, g_pre, w_in_p, rope_a, rope_m, g_cq, w_uq_n, w_uq_p, g_ckv, w_ukv_p):
    B, S, D = x.shape
    tm = min(TM_IN, S)
    nb = S // MOBA_BLOCK
    rb = tm // MOBA_BLOCK
    const2 = lambda b, i: (0, 0)
    tok3 = lambda b, i: (b, i, 0)
    outs = pl.pallas_call(
        _inproj_kernel,
        out_shape=[
            jax.ShapeDtypeStruct((B, S, MOBA_WIDTH), BF16),
            jax.ShapeDtypeStruct((B, S, MOBA_WIDTH), BF16),
            jax.ShapeDtypeStruct((B, N_PAIRS, nb, LANES, MOBA_BLOCK), BF16),
            jax.ShapeDtypeStruct((B, nb, 1, MOBA_WIDTH), F32),
            jax.ShapeDtypeStruct((B, MLA_HEADS, S, 2 * LANES), BF16),
            jax.ShapeDtypeStruct((B, N_PAIRS, S, 2 * LANES), BF16),
            jax.ShapeDtypeStruct((B, N_PAIRS, nb, LANES, MOBA_BLOCK), BF16),
        ],
        grid=(B, S // tm),
        in_specs=[
            pl.BlockSpec((1, tm, D), tok3),
            pl.BlockSpec((1, 6, D), lambda b, i: (b, 0, 0)),
            pl.BlockSpec((1, D), const2),
            pl.BlockSpec(w_in_p.shape, const2),
            pl.BlockSpec((1, tm, 3 * LANES), tok3),
            pl.BlockSpec((1, tm, 3 * LANES), tok3),
            pl.BlockSpec((1, MLA_Q_RANK), const2),
            pl.BlockSpec(w_uq_n.shape, const2),
            pl.BlockSpec(w_uq_p.shape, const2),
            pl.BlockSpec((1, MLA_KV_RANK), const2),
            pl.BlockSpec(w_ukv_p.shape, const2),
        ],
        out_specs=[
            pl.BlockSpec((1, tm, MOBA_WIDTH), tok3),
            pl.BlockSpec((1, tm, MOBA_WIDTH), tok3),
            pl.BlockSpec((1, N_PAIRS, rb, LANES, MOBA_BLOCK), lambda b, i: (b, 0, i, 0, 0)),
            pl.BlockSpec((1, rb, 1, MOBA_WIDTH), lambda b, i: (b, i, 0, 0)),
            pl.BlockSpec((1, MLA_HEADS, tm, 2 * LANES), lambda b, i: (b, 0, i, 0)),
            pl.BlockSpec((1, N_PAIRS, tm, 2 * LANES), lambda b, i: (b, 0, i, 0)),
            pl.BlockSpec((1, N_PAIRS, rb, LANES, MOBA_BLOCK), lambda b, i: (b, 0, i, 0, 0)),
        ],
        compiler_params=_cparams(("arbitrary", "arbitrary")),
        name="inproj",
    )(x, mod3, g_pre, w_in_p, rope_a, rope_m, g_cq, w_uq_n, w_uq_p, g_ckv, w_ukv_p)
    return outs


def _nt_dot(a, b, **kw):
    return lax.dot_general(a, b, (((1,), (1,)), ((), ())), preferred_element_type=F32, **kw)


def _group_blocks(i, t):
    out = []
    for g in range(KV_GROUP):
        j = i - KV_GROUP * t - (KV_GROUP - 1 - g)
        out.append((jnp.maximum(j, 0), j >= 0))
    return out


def _flash_attention(i, q_tile, k_tile, vt_tile, tile_bias, scratch):
    st_a, cm_a, st_b, cm_b, m_ref, l_ref, acc_ref = scratch
    L = MOBA_BLOCK
    key_minus_qry = (lax.broadcasted_iota(jnp.int32, (L, L), 0) - lax.broadcasted_iota(jnp.int32, (L, L), 1))

    def score_tile(blocks, g, hd, st_buf, cm_buf):
        j, _ = blocks[g]
        st = _nt_dot(k_tile(j), q_tile(hd))
        if g == KV_GROUP - 1:
            st = jnp.where(key_minus_qry <= (i - j) * L, st, NEG)
        st_buf[g, hd] = st
        cm_buf[g, hd] = jnp.max(st, axis=0, keepdims=True)

    def stage(t, cur, nxt):
        blocks = _group_blocks(i, t)
        nxt_blocks = _group_blocks(i, t + 1)
        for hd in range(HEADS_PER_STEP):
            biases = [tile_bias(hd, j, real) for j, real in blocks]
            m_old = m_ref[hd]
            m_new = m_old
            for g in range(KV_GROUP):
                m_new = jnp.maximum(m_new, cur[1][g, hd] + biases[g])
            alpha = jnp.exp2(m_old - m_new)
            l_new = alpha * l_ref[hd]
            pv = None
            for g, (j, _) in enumerate(blocks):
                score_tile(nxt_blocks, g, hd, *nxt)
                p = jnp.exp2(cur[0][g, hd] - (m_new - biases[g]))
                l_new = l_new + jnp.sum(p, axis=0, keepdims=True)
                d = jnp.dot(vt_tile(j, hd), p.astype(BF16), preferred_element_type=F32)
                pv = d if pv is None else pv + d
            m_ref[hd] = m_new
            l_ref[hd] = l_new
            acc_ref[hd] = alpha * acc_ref[hd] + pv

    m_ref[...] = jnp.full(m_ref.shape, -jnp.inf, F32)
    l_ref[...] = jnp.zeros(l_ref.shape, F32)
    acc_ref[...] = jnp.zeros(acc_ref.shape, F32)
    n_groups = (i + KV_GROUP) // KV_GROUP
    buf_a, buf_b = (st_a, cm_a), (st_b, cm_b)
    first = _group_blocks(i, 0)
    for g in range(KV_GROUP):
        for hd in range(HEADS_PER_STEP):
            score_tile(first, g, hd, *buf_a)

    def body(tt, carry):
        t = 2 * tt
        stage(t, buf_a, buf_b)

        @pl.when(t + 1 < n_groups)
        def _():
            stage(t + 1, buf_b, buf_a)

        return carry

    lax.fori_loop(0, (n_groups + 1) // 2, body, 0)


def _attn_scratch(dv):
    L = MOBA_BLOCK
    st = pltpu.VMEM((KV_GROUP, HEADS_PER_STEP, L, L), F32)
    cm = pltpu.VMEM((KV_GROUP, HEADS_PER_STEP, 1, L), F32)
    vec = pltpu.VMEM((HEADS_PER_STEP, 1, L), F32)
    return [st, cm, st, cm, vec, vec, pltpu.VMEM((HEADS_PER_STEP, dv, L), F32)]


def _moba_kernel(q_ref, k_ref, vt_ref, kmean_ref, o_ref, qh_ref, bias_ref, *scratch):
    L = MOBA_BLOCK
    nb = kmean_ref.shape[1]
    i = pl.program_id(2)
    q = q_ref[0]
    lane = lax.broadcasted_iota(jnp.int32, (L, LANES), 1)
    kmean = kmean_ref[0, :, 0, :]
    blk = lax.broadcasted_iota(jnp.int32, (nb, L), 0)
    past = blk < i

    for hd in range(HEADS_PER_STEP):
        own = (lane >= hd * MOBA_HEAD_DIM) & (lane < (hd + 1) * MOBA_HEAD_DIM)
        qh = jnp.where(own, q, jnp.zeros_like(q))
        qh_ref[hd] = qh
        gate = _nt_dot(kmean, qh.astype(F32), precision=lax.Precision.HIGHEST)
        gate = jnp.where(past, gate, -jnp.inf)
        rank = jnp.zeros((nb, L), jnp.int32)
        for j in range(nb):
            gj = gate[j:j + 1, :]
            ahead = (gj > gate) | ((gj == gate) & (j < blk))
            rank = rank + ahead.astype(jnp.int32)
        bias_ref[hd] = jnp.where((blk == i) | (past & (rank < MOBA_TOPK)), 0.0, NEG)

    _flash_attention(
        i,
        q_tile=lambda hd: qh_ref[hd],
        k_tile=lambda j: k_ref[0, pl.ds(pl.multiple_of(j * L, L), L), :],
        vt_tile=lambda j, hd: vt_ref[0, 0, j, pl.ds(hd * MOBA_HEAD_DIM, MOBA_HEAD_DIM), :],
        tile_bias=lambda hd, j, real: jnp.where(real, bias_ref[hd, pl.ds(j, 1), :], NEG),
        scratch=scratch)
    m_ref, l_ref, acc_ref = scratch[-3:]

    ot = jnp.concatenate([acc_ref[hd] / l_ref[hd] for hd in range(HEADS_PER_STEP)], axis=0)
    o_ref[0] = ot.T


def _moba(qa, ka, vat, kmean):
    B, S, _ = qa.shape
    L = MOBA_BLOCK
    nb = S // L
    return pl.pallas_call(
        _moba_kernel,
        out_shape=jax.ShapeDtypeStruct((B, S, MOBA_WIDTH), F32),
        grid=(B, N_PAIRS, nb),
        in_specs=[
            pl.BlockSpec((1, L, LANES), lambda b, p, i: (b, i, p)),
            pl.BlockSpec((1, S, LANES), lambda b, p, i: (b, 0, p)),
            pl.BlockSpec((1, 1, nb, LANES, L), lambda b, p, i: (b, p, 0, 0, 0)),
            pl.BlockSpec((1, nb, 1, LANES), lambda b, p, i: (b, 0, 0, p)),
        ],
        out_specs=pl.BlockSpec((1, L, LANES), lambda b, p, i: (b, i, p)),
        scratch_shapes=[
            pltpu.VMEM((HEADS_PER_STEP, L, LANES), BF16),
            pltpu.VMEM((HEADS_PER_STEP, nb, L), F32),
        ] + _attn_scratch(MOBA_HEAD_DIM),
        compiler_params=_cparams(("arbitrary", "arbitrary", "arbitrary")),
        name="moba",
    )(qa, ka, vat, kmean)


def _mla_kernel(q_ref, k_ref, vt_ref, o_ref, *scratch):
    L = MOBA_BLOCK
    i = pl.program_id(2)
    _flash_attention(
        i,
        q_tile=lambda hd: q_ref[0, hd],
        k_tile=lambda j: k_ref[0, 0, pl.ds(pl.multiple_of(j * L, L), L), :],
        vt_tile=lambda j, hd: vt_ref[0, 0, j, pl.ds(hd * MLA_V_DIM, MLA_V_DIM), :],
        tile_bias=lambda hd, j, real: jnp.where(real, 0.0, NEG),
        scratch=scratch)
    m_ref, l_ref, acc_ref = scratch[-3:]

    ot = jnp.concatenate([acc_ref[hd] / l_ref[hd] for hd in range(HEADS_PER_STEP)], axis=0)
    o_ref[0] = ot.T


def _mla(qcat, kcat, vmt):
    B, _, S, W = qcat.shape
    L = MOBA_BLOCK
    nb = S // L
    return pl.pallas_call(
        _mla_kernel,
        out_shape=jax.ShapeDtypeStruct((B, S, MLA_WIDTH), F32),
        grid=(B, N_PAIRS, nb),
        in_specs=[
            pl.BlockSpec((1, HEADS_PER_STEP, L, W), lambda b, p, i: (b, p, i, 0)),
            pl.BlockSpec((1, 1, S, W), lambda b, p, i: (b, p, 0, 0)),
            pl.BlockSpec((1, 1, nb, LANES, L), lambda b, p, i: (b, p, 0, 0, 0)),
        ],
        out_specs=pl.BlockSpec((1, L, LANES), lambda b, p, i: (b, i, p)),
        scratch_shapes=_attn_scratch(MLA_V_DIM),
        compiler_params=_cparams(("arbitrary", "arbitrary", "arbitrary")),
        name="mla",
    )(qcat, kcat, vmt)


def _postattn_kernel(x_ref, om_ref, ol_ref, mod_ref, gom_ref, gol_ref, wo_ref, gpost_ref, gpre_ref,
                     wr_ref, br_ref, x1_ref, h2_ref, ri_ref, rw_ref):
    tm = x_ref.shape[1]
    gt_a = mod_ref[0, 2:3, :]
    sh_f = mod_ref[0, 3:4, :]
    sc_f = mod_ref[0, 4:5, :]
    a = jnp.concatenate([_rms(om_ref[0], gom_ref[...]), _rms(ol_ref[0], gol_ref[...])], axis=-1)
    o = jnp.dot(a.astype(BF16), wo_ref[...], preferred_element_type=F32)
    x1 = x_ref[0] + gt_a * _rms(o, gpost_ref[...])
    x1_ref[0] = x1
    h2 = _rms(x1, gpre_ref[...]) * (1.0 + sc_f) + sh_f
    h2_ref[0] = h2

    logit = jnp.dot(h2, wr_ref[...], preferred_element_type=F32, precision=lax.Precision.HIGHEST)
    biased = logit + br_ref[...]
    lane = lax.broadcasted_iota(jnp.int32, (tm, LANES), 1)
    big = jnp.int32(LANES)

    def first_argmax(v):
        mx = jnp.max(v, axis=-1, keepdims=True)
        return jnp.min(jnp.where(v == mx, lane, big), axis=-1, keepdims=True)

    def pick(v, idx):
        return jnp.sum(jnp.where(lane == idx, v, 0.0), axis=-1, keepdims=True)

    gmask = (lane >= N_EXPERTS) & (lane < N_EXPERTS + N_GROUPS)
    gl = jnp.where(gmask, logit, -jnp.inf)
    pg = jnp.exp(gl - jnp.max(gl, axis=-1, keepdims=True))
    g_lane = first_argmax(jnp.where(gmask, biased, -jnp.inf))
    g_prob = pick(pg, g_lane) / jnp.sum(pg, axis=-1, keepdims=True)
    e_lo = (g_lane - N_EXPERTS) * EXPERTS_PER_GROUP
    emask = (lane >= e_lo) & (lane < e_lo + EXPERTS_PER_GROUP)
    el = jnp.where(emask, logit, -jnp.inf)
    pe = jnp.exp(el - jnp.max(el, axis=-1, keepdims=True))
    eb = jnp.where(emask, biased, -jnp.inf)
    i1 = first_argmax(eb)
    i2 = first_argmax(jnp.where(lane == i1, -jnp.inf, eb))
    p1 = pick(pe, i1)
    p2 = pick(pe, i2)
    denom = p1 + p2
    w1 = p1 / denom * g_prob
    w2 = p2 / denom * g_prob
    ri_ref[0] = jnp.where(lane == 0, i1, jnp.where(lane == 1, i2, 0))
    rw_ref[0] = jnp.where(lane == 0, w1, jnp.where(lane == 1, w2, 0.0))


def _postattn(x, o_moba, o_mla, mod3, g_om, g_ol, w_o, g_post, g_pre, w_r, b_r):
    B, S, D = x.shape
    tm = min(TM_IN, S)
    const2 = lambda b, i: (0, 0)
    tok3 = lambda b, i: (b, i, 0)
    return pl.pallas_call(
        _postattn_kernel,
        out_shape=[
            jax.ShapeDtypeStruct((B, S, D), F32),
            jax.ShapeDtypeStruct((B, S, D), F32),
            jax.ShapeDtypeStruct((B, S, LANES), jnp.int32),
            jax.ShapeDtypeStruct((B, S, LANES), F32),
        ],
        grid=(B, S // tm),
        in_specs=[
            pl.BlockSpec((1, tm, D), tok3),
            pl.BlockSpec((1, tm, MOBA_WIDTH), tok3),
            pl.BlockSpec((1, tm, MLA_WIDTH), tok3),
            pl.BlockSpec((1, 6, D), lambda b, i: (b, 0, 0)),
            pl.BlockSpec((1, MOBA_WIDTH), const2),
            pl.BlockSpec((1, MLA_WIDTH), const2),
            pl.BlockSpec(w_o.shape, const2),
            pl.BlockSpec((1, D), const2),
            pl.BlockSpec((1, D), const2),
            pl.BlockSpec(w_r.shape, const2),
            pl.BlockSpec((1, LANES), const2),
        ],
        out_specs=[
            pl.BlockSpec((1, tm, D), tok3),
            pl.BlockSpec((1, tm, D), tok3),
            pl.BlockSpec((1, tm, LANES), tok3),
            pl.BlockSpec((1, tm, LANES), tok3),
        ],
        compiler_params=_cparams(("arbitrary", "arbitrary")),
        name="postattn",
    )(x, o_moba, o_mla, mod3, g_om, g_ol, w_o, g_post, g_pre, w_r, b_r)


GATHER_UNROLL = 8


def _start_row_gather(idx_ref, base, n_rows, src_hbm, dst_ref, sem):
    def body(r, carry):
        row = idx_ref[base + r]
        pltpu.make_async_copy(src_hbm.at[pl.ds(row, 1), :], dst_ref.at[pl.ds(r, 1), :], sem).start()
        return carry
    lax.fori_loop(0, n_rows, body, 0, unroll=GATHER_UNROLL)


def _wait_row_gather(n_rows, src_hbm, dst_ref, sem):
    pltpu.make_async_copy(src_hbm.at[pl.ds(0, n_rows), :], dst_ref, sem).wait()


def _moe_kernel(te_ref, nu_ref, tok_ref, h_hbm, wgu_ref, wd_ref, y_ref, xbuf, sem):
    tm = y_ref.shape[0]
    i = pl.program_id(0)
    n_used = nu_ref[0]
    slot = i % 2

    @pl.when((i == 0) & (n_used > 0))
    def _():
        _start_row_gather(tok_ref, 0, tm, h_hbm, xbuf.at[0], sem.at[0])

    @pl.when(i + 1 < n_used)
    def _():
        _start_row_gather(tok_ref, (i + 1) * tm, tm, h_hbm, xbuf.at[1 - slot], sem.at[1 - slot])

    @pl.when(i < n_used)
    def _():
        _wait_row_gather(tm, h_hbm, xbuf.at[slot], sem.at[slot])
        xb = xbuf[slot].astype(BF16)
        gu = jnp.dot(xb, wgu_ref[0], preferred_element_type=F32)
        g = gu[:, :D_EXPERT]
        u = gu[:, D_EXPERT:]
        hid = g * jax.nn.sigmoid(g) * u
        y_ref[...] = jnp.dot(hid.astype(BF16), wd_ref[0], preferred_element_type=F32)

    @pl.when(i >= n_used)
    def _():
        y_ref[...] = jnp.zeros_like(y_ref)


def _moe(tile_expert, n_used, tok_sorted, h2, w_gu, w_d):
    T, D = h2.shape
    n_tiles = tile_expert.shape[0]
    tm = TM_MOE
    grid_spec = pltpu.PrefetchScalarGridSpec(
        num_scalar_prefetch=3,
        grid=(n_tiles,),
        in_specs=[
            pl.BlockSpec(memory_space=pl.ANY),
            pl.BlockSpec((1, D, 2 * D_EXPERT), lambda i, te, nu, tok: (te[i], 0, 0)),
            pl.BlockSpec((1, D_EXPERT, D), lambda i, te, nu, tok: (te[i], 0, 0)),
        ],
        out_specs=pl.BlockSpec((tm, D), lambda i, te, nu, tok: (i, 0)),
        scratch_shapes=[pltpu.VMEM((2, tm, D), F32), pltpu.SemaphoreType.DMA((2,))],
    )
    return pl.pallas_call(
        _moe_kernel,
        out_shape=jax.ShapeDtypeStruct((n_tiles * tm, D), F32),
        grid_spec=grid_spec,
        compiler_params=_cparams(("arbitrary",)),
        name="moe",
    )(tile_expert, n_used, tok_sorted, h2, w_gu, w_d)


def _combine_kernel(pos_ref, y_hbm, x1_ref, rw_ref, mod_ref, g_ref, o_ref, ybuf, sem):
    tm = x1_ref.shape[1]
    n_j = pl.num_programs(1)
    step = pl.program_id(0) * n_j + pl.program_id(1)
    n_steps = pl.num_programs(0) * n_j
    slot = step % 2

    @pl.when(step == 0)
    def _():
        _start_row_gather(pos_ref, 0, 2 * tm, y_hbm, ybuf.at[0], sem.at[0])

    @pl.when(step + 1 < n_steps)
    def _():
        _start_row_gather(pos_ref, (step + 1) * 2 * tm, 2 * tm, y_hbm, ybuf.at[1 - slot], sem.at[1 - slot])

    _wait_row_gather(2 * tm, y_hbm, ybuf.at[slot], sem.at[slot])
    w1 = rw_ref[0, :, 0:1]
    w2 = rw_ref[0, :, 1:2]
    y = w1 * ybuf[slot, 0:tm, :] + w2 * ybuf[slot, tm:2 * tm, :]
    gt_f = mod_ref[0, 5:6, :]
    o_ref[0] = x1_ref[0] + gt_f * _rms(y, g_ref[...])


def _combine(pos, y_sorted, x1, route_w, mod3, g_post):
    B, S, D = x1.shape
    tm = min(TM_CMB, S)
    grid_spec = pltpu.PrefetchScalarGridSpec(
        num_scalar_prefetch=1,
        grid=(B, S // tm),
        in_specs=[
            pl.BlockSpec(memory_space=pl.ANY),
            pl.BlockSpec((1, tm, D), lambda b, j, pos: (b, j, 0)),
            pl.BlockSpec((1, tm, LANES), lambda b, j, pos: (b, j, 0)),
            pl.BlockSpec((1, 6, D), lambda b, j, pos: (b, 0, 0)),
            pl.BlockSpec((1, D), lambda b, j, pos: (0, 0)),
        ],
        out_specs=pl.BlockSpec((1, tm, D), lambda b, j, pos: (b, j, 0)),
        scratch_shapes=[pltpu.VMEM((2, 2 * tm, D), F32), pltpu.SemaphoreType.DMA((2,))],
    )
    return pl.pallas_call(
        _combine_kernel,
        out_shape=jax.ShapeDtypeStruct((B, S, D), F32),
        grid_spec=grid_spec,
        compiler_params=_cparams(("arbitrary", "arbitrary")),
        name="combine",
    )(pos, y_sorted, x1, route_w, mod3, g_post)


def _rope_table(positions, dim, period_fill):
    half = dim // 2
    inv_freq = jnp.exp(-math.log(ROPE_THETA) * jnp.arange(half, dtype=F32) * (2.0 / dim))
    ang = positions.astype(F32)[..., None] * inv_freq
    cos, sin = jnp.cos(ang), jnp.sin(ang)
    B, S = positions.shape
    rest = period_fill - dim
    one = jnp.ones((B, S, rest), F32)
    zero_r = jnp.zeros((B, S, rest), F32)
    zero_h = jnp.zeros((B, S, half), F32)
    c = jnp.concatenate([cos, cos, one], axis=-1)
    s1 = jnp.concatenate([-sin, zero_h, zero_r], axis=-1)
    s2 = jnp.concatenate([zero_h, sin, zero_r], axis=-1)
    reps = LANES // period_fill
    return jnp.concatenate([jnp.tile(t, (1, 1, reps)) for t in (c, s1, s2)], axis=-1)


def _routing_metadata(eid, tm):
    T = eid.shape[0]
    n_pairs = 2 * T
    n_tiles = n_pairs // tm + N_EXPERTS
    flat = eid.reshape(-1)
    onehot = (flat[:, None] == jnp.arange(N_EXPERTS, dtype=jnp.int32)[None, :]).astype(jnp.int32)
    csum = jnp.cumsum(onehot, axis=0)
    counts = csum[-1]
    rank = jnp.sum(onehot * csum, axis=1) - 1
    tiles_per = (counts + tm - 1) // tm
    tile_end = jnp.cumsum(tiles_per)
    tile_start = tile_end - tiles_per
    n_used = tile_end[-1]
    pos_pair = (jnp.sum(onehot * tile_start[None, :], axis=1) * tm + rank).astype(jnp.int32)
    tile_ids = jnp.arange(n_tiles, dtype=jnp.int32)
    te = jnp.minimum(jnp.sum((tile_ids[:, None] >= tile_end[None, :]).astype(jnp.int32), axis=1), N_EXPERTS - 1)
    last_e = jnp.sum(jnp.where(tile_ids == n_used - 1, te, 0))
    tile_expert = jnp.where(tile_ids < n_used, te, last_e).astype(jnp.int32)
    tok_sorted = jnp.zeros((n_tiles * tm,), jnp.int32).at[pos_pair].set(
        jnp.arange(n_pairs, dtype=jnp.int32) // 2, unique_indices=True)
    return tile_expert, n_used.reshape(1).astype(jnp.int32), tok_sorted, pos_pair


def kernel(x, c, positions, w_ada, b_ada, g_pre_attn, g_post_attn, g_pre_ffn, g_post_ffn, w_in, g_cq, w_uq,
           g_ckv, w_ukv, g_out_moba, g_out_mla, w_o, w_group, b_group, w_expert, b_expert, w_e_gate, w_e_up,
           w_e_down):
    B, S, D = x.shape
    depth = w_ada.shape[0]
    rope_a = _rope_table(positions, MOBA_ROT_DIMS, MOBA_HEAD_DIM)
    rope_m = _rope_table(positions, MLA_ROPE_DIM, MLA_ROPE_DIM)
    row = lambda v: v.reshape(1, -1)

    for l in range(depth):
        n_main = 3 * MOBA_WIDTH + MLA_Q_RANK + MLA_KV_RANK
        w_in_p = jnp.concatenate([w_in[l][:, :n_main], jnp.tile(w_in[l][:, n_main:], (1, LANES // MLA_ROPE_DIM))],
                                 axis=1).astype(BF16)
        wq = w_uq[l].reshape(MLA_Q_RANK, MLA_HEADS, MLA_NOPE_DIM + MLA_ROPE_DIM)
        wq_n = wq[:, :, :MLA_NOPE_DIM]
        zero_n = jnp.zeros_like(wq_n)
        even = (jnp.arange(MLA_HEADS) % 2 == 0)[None, :, None]
        w_uq_n = jnp.concatenate([jnp.where(even, wq_n, zero_n), jnp.where(even, zero_n, wq_n)], axis=-1)
        w_uq_n = w_uq_n.reshape(MLA_Q_RANK, MLA_HEADS * LANES).astype(BF16)
        w_uq_p = wq[:, :, MLA_NOPE_DIM:].reshape(MLA_Q_RANK, MLA_HEADS * MLA_ROPE_DIM).astype(BF16)
        wkv = w_ukv[l].reshape(MLA_KV_RANK, MLA_HEADS, MLA_NOPE_DIM + MLA_V_DIM)
        w_ukv_p = jnp.concatenate([wkv[:, :, :MLA_NOPE_DIM].reshape(MLA_KV_RANK, -1),
                                   wkv[:, :, MLA_NOPE_DIM:].reshape(MLA_KV_RANK, -1)], axis=1).astype(BF16)
        w_r = jnp.concatenate([w_expert[l], w_group[l],
                               jnp.zeros((D, LANES - N_EXPERTS - N_GROUPS), F32)], axis=1)
        b_r = jnp.concatenate([b_expert[l], b_group[l], jnp.zeros((LANES - N_EXPERTS - N_GROUPS,), F32)]).reshape(1, LANES)
        w_gu = jnp.concatenate([w_e_gate[l], w_e_up[l]], axis=-1).astype(BF16)
        w_d = w_e_down[l].astype(BF16)

        mod3 = _adaln(c, w_ada[l], b_ada[l]).reshape(B, 6, D)

        qa, ka, vat, kmean, qcat, kcat, vmt = _inproj(
            x, mod3, row(g_pre_attn[l]), w_in_p, rope_a, rope_m, row(g_cq[l]), w_uq_n, w_uq_p,
            row(g_ckv[l]), w_ukv_p)
        o_moba = _moba(qa, ka, vat, kmean)
        o_mla = _mla(qcat, kcat, vmt)
        x1, h2, route_i, route_w = _postattn(
            x, o_moba, o_mla, mod3, row(g_out_moba[l]), row(g_out_mla[l]), w_o[l].astype(BF16),
            row(g_post_attn[l]), row(g_pre_ffn[l]), w_r, b_r)

        eid = route_i.reshape(B * S, LANES)[:, :2]
        tile_expert, n_used, tok_sorted, pos_pair = _routing_metadata(eid, TM_MOE)
        y_sorted = _moe(tile_expert, n_used, tok_sorted, h2.reshape(B * S, D), w_gu, w_d)
        tmc = min(TM_CMB, S)
        pos = pos_pair.reshape(B * S // tmc, tmc, 2).transpose(0, 2, 1).reshape(-1)
        x = _combine(pos, y_sorted, x1, route_w, mod3, row(g_post_ffn[l]))
    return x
```

```python
import functools
import math

import jax
import jax.numpy as jnp
import numpy as np
from jax import lax
from jax.experimental import pallas as pl
from jax.experimental.pallas import tpu as pltpu

F32 = jnp.float32
BF16 = jnp.bfloat16

D_MODEL = 1024
MOBA_HEADS = 8
MOBA_HEAD_DIM = 64
MOBA_WIDTH = MOBA_HEADS * MOBA_HEAD_DIM
MOBA_BLOCK = 256
MOBA_TOPK = 3
MOBA_ROT_DIMS = MOBA_HEAD_DIM // 4
MLA_HEADS = 8
MLA_Q_RANK = 256
MLA_KV_RANK = 128
MLA_NOPE_DIM = 64
MLA_ROPE_DIM = 32
MLA_V_DIM = 64
MLA_WIDTH = MLA_HEADS * MLA_V_DIM
ROPE_THETA = 500000.0
N_GROUPS = 4
EXPERTS_PER_GROUP = 8
N_EXPERTS = N_GROUPS * EXPERTS_PER_GROUP
D_EXPERT = 256
EPS = 1e-6
NEG = -1e30
LOG2E = math.log2(math.e)

LANES = 128
HEADS_PER_STEP = LANES // MOBA_HEAD_DIM
N_PAIRS = MOBA_HEADS // HEADS_PER_STEP
VMEM_LIMIT = 56 * 1024 * 1024

TM_IN = 512
KV_GROUP = 4
TM_MOE = 256
CHUNK = 8
LOCAL_ROWS = 2 * TM_IN + N_EXPERTS * CHUNK


def _cparams(sem):
    return pltpu.CompilerParams(dimension_semantics=sem, vmem_limit_bytes=VMEM_LIMIT)


def _rms(x, g):
    return x * lax.rsqrt(jnp.mean(x * x, axis=-1, keepdims=True) + EPS) * g


def _adaln_kernel(c_ref, w_ref, b_ref, o_ref):
    c = c_ref[...]
    ca = c * jax.nn.sigmoid(c)
    o_ref[...] = jnp.dot(ca, w_ref[...], preferred_element_type=F32,
                         precision=lax.Precision.HIGHEST) + b_ref[...]


def _adaln(c, w, b):
    B, D = c.shape
    N = w.shape[1]
    tn = 1536 if N % 1536 == 0 else N
    return pl.pallas_call(
        _adaln_kernel,
        out_shape=jax.ShapeDtypeStruct((B, N), F32),
        grid=(N // tn,),
        in_specs=[pl.BlockSpec((B, D), lambda j: (0, 0)),
                  pl.BlockSpec((D, tn), lambda j: (0, j)),
                  pl.BlockSpec((1, tn), lambda j: (0, j))],
        out_specs=pl.BlockSpec((B, tn), lambda j: (0, j)),
        compiler_params=_cparams(("arbitrary",)),
        name="adaln",
    )(c, w, b.reshape(1, N))


def _rope128(xc, tab_ref, shift):
    c = tab_ref[0, :, 0:LANES]
    s1 = tab_ref[0, :, LANES:2 * LANES]
    s2 = tab_ref[0, :, 2 * LANES:3 * LANES]
    return xc * c + pltpu.roll(xc, LANES - shift, axis=1) * s1 + pltpu.roll(xc, shift, axis=1) * s2


def _store_vt(src, col0, vt_ref):
    for p in range(N_PAIRS):
        for r in range(src.shape[0] // MOBA_BLOCK):
            blk = src[r * MOBA_BLOCK:(r + 1) * MOBA_BLOCK, col0 + p * LANES:col0 + (p + 1) * LANES]
            vt_ref[0, p, r] = blk.T.astype(BF16)


def _inproj_kernel(x_ref, mod_ref, gpre_ref, win_ref, ra_ref, rm_ref, gcq_ref, wuqn_ref, wuqp_ref,
                   gckv_ref, wukv_ref,
                   qa_ref, ka_ref, vat_ref, kmean_ref, qcat_ref, kcat_ref, vmt_ref):
    tm = x_ref.shape[1]
    x = x_ref[0]
    sh = mod_ref[0, 0:1, :]
    sc = mod_ref[0, 1:2, :]
    h = _rms(x, gpre_ref[...]) * (1.0 + sc) + sh
    z = jnp.dot(h.astype(BF16), win_ref[...], preferred_element_type=F32)

    moba_scale = MOBA_HEAD_DIM ** -0.5 * LOG2E
    half_a = MOBA_ROT_DIMS // 2
    n_chunks = MOBA_WIDTH // LANES
    for cidx in range(n_chunks):
        lo = cidx * LANES
        q = _rope128(z[:, lo:lo + LANES], ra_ref, half_a)
        qa_ref[0, :, lo:lo + LANES] = (q * moba_scale).astype(BF16)
        k = _rope128(z[:, MOBA_WIDTH + lo:MOBA_WIDTH + lo + LANES], ra_ref, half_a)
        ka_ref[0, :, lo:lo + LANES] = k.astype(BF16)
        for r in range(tm // MOBA_BLOCK):
            kmean_ref[0, r, :, lo:lo + LANES] = jnp.mean(
                k[r * MOBA_BLOCK:(r + 1) * MOBA_BLOCK], axis=0, keepdims=True)
    _store_vt(z, 2 * MOBA_WIDTH, vat_ref)

    o_cq = 3 * MOBA_WIDTH
    o_ckv = o_cq + MLA_Q_RANK
    o_kpe = o_ckv + MLA_KV_RANK
    mla_scale = (MLA_NOPE_DIM + MLA_ROPE_DIM) ** -0.5 * LOG2E
    half_m = MLA_ROPE_DIM // 2

    cqn = _rms(z[:, o_cq:o_cq + MLA_Q_RANK], gcq_ref[...]).astype(BF16)
    qn = jnp.dot(cqn, wuqn_ref[...], preferred_element_type=F32) * mla_scale
    qp = jnp.dot(cqn, wuqp_ref[...], preferred_element_type=F32)
    lane = lax.broadcasted_iota(jnp.int32, (tm, LANES), 1)
    heads_per_chunk = LANES // MLA_ROPE_DIM
    qp_chunks = [_rope128(qp[:, g * LANES:(g + 1) * LANES], rm_ref, half_m) * mla_scale
                 for g in range(MLA_HEADS // heads_per_chunk)]
    for hd in range(MLA_HEADS):
        qcat_ref[0, hd, :, 0:LANES] = qn[:, hd * LANES:(hd + 1) * LANES].astype(BF16)
        sub = hd % heads_per_chunk
        keep = (lane >= sub * MLA_ROPE_DIM) & (lane < (sub + 1) * MLA_ROPE_DIM)
        qcat_ref[0, hd, :, LANES:2 * LANES] = jnp.where(keep, qp_chunks[hd // heads_per_chunk], 0.0).astype(BF16)

    ckvn = _rms(z[:, o_ckv:o_ckv + MLA_KV_RANK], gckv_ref[...]).astype(BF16)
    kv = jnp.dot(ckvn, wukv_ref[...], preferred_element_type=F32)
    kpe = _rope128(z[:, o_kpe:o_kpe + LANES], rm_ref, half_m).astype(BF16)
    for p in range(N_PAIRS):
        kcat_ref[0, p, :, 0:LANES] = kv[:, p * LANES:(p + 1) * LANES].astype(BF16)
        kcat_ref[0, p, :, LANES:2 * LANES] = kpe
    _store_vt(kv, MLA_WIDTH, vmt_ref)


def _inproj(x, mod3, g_pre, w_in_p, rope_a, rope_m, g_cq, w_uq_n, w_uq_p, g_ckv, w_ukv_p):
    B, S, D = x.shape
    tm = min(TM_IN, S)
    nb = S // MOBA_BLOCK
    rb = tm // MOBA_BLOCK
    const2 = lambda b, i: (0, 0)
    tok3 = lambda b, i: (b, i, 0)
    outs = pl.pallas_call(
        _inproj_kernel,
        out_shape=[
            jax.ShapeDtypeStruct((B, S, MOBA_WIDTH), BF16),
            jax.ShapeDtypeStruct((B, S, MOBA_WIDTH), BF16),
            jax.ShapeDtypeStruct((B, N_PAIRS, nb, LANES, MOBA_BLOCK), BF16),
            jax.ShapeDtypeStruct((B, nb, 1, MOBA_WIDTH), F32),
            jax.ShapeDtypeStruct((B, MLA_HEADS, S, 2 * LANES), BF16),
            jax.ShapeDtypeStruct((B, N_PAIRS, S, 2 * LANES), BF16),
            jax.ShapeDtypeStruct((B, N_PAIRS, nb, LANES, MOBA_BLOCK), BF16),
        ],
        grid=(B, S // tm),
        in_specs=[
            pl.BlockSpec((1, tm, D), tok3),
            pl.BlockSpec((1, 6, D), lambda b, i: (b, 0, 0)),
            pl.BlockSpec((1, D), const2),
            pl.BlockSpec(w_in_p.shape, const2),
            pl.BlockSpec((1, tm, 3 * LANES), tok3),
            pl.BlockSpec((1, tm, 3 * LANES), tok3),
            pl.BlockSpec((1, MLA_Q_RANK), const2),
            pl.BlockSpec(w_uq_n.shape, const2),
            pl.BlockSpec(w_uq_p.shape, const2),
            pl.BlockSpec((1, MLA_KV_RANK), const2),
            pl.BlockSpec(w_ukv_p.shape, const2),
        ],
        out_specs=[
            pl.BlockSpec((1, tm, MOBA_WIDTH), tok3),
            pl.BlockSpec((1, tm, MOBA_WIDTH), tok3),
            pl.BlockSpec((1, N_PAIRS, rb, LANES, MOBA_BLOCK), lambda b, i: (b, 0, i, 0, 0)),
            pl.BlockSpec((1, rb, 1, MOBA_WIDTH), lambda b, i: (b, i, 0, 0)),
            pl.BlockSpec((1, MLA_HEADS, tm, 2 * LANES), lambda b, i: (b, 0, i, 0)),
            pl.BlockSpec((1, N_PAIRS, tm, 2 * LANES), lambda b, i: (b, 0, i, 0)),
            pl.BlockSpec((1, N_PAIRS, rb, LANES, MOBA_BLOCK), lambda b, i: (b, 0, i, 0, 0)),
        ],
        compiler_params=_cparams(("arbitrary", "arbitrary")),
        name="inproj",
    )(x, mod3, g_pre, w_in_p, rope_a, rope_m, g_cq, w_uq_n, w_uq_p, g_ckv, w_ukv_p)
    return outs


def _nt_dot(a, b, **kw):
    return lax.dot_general(a, b, (((1,), (1,)), ((), ())), preferred_element_type=F32, **kw)


def _group_blocks(i, t):
    out = []
    for g in range(KV_GROUP):
        j = i - KV_GROUP * t - (KV_GROUP - 1 - g)
        out.append((jnp.maximum(j, 0), j >= 0))
    return out


def _flash_attention(i, q_tile, k_tile, vt_tile, tile_bias, scratch):
    st_a, cm_a, st_b, cm_b, m_ref, l_ref, acc_ref = scratch
    L = MOBA_BLOCK
    key_minus_qry = (lax.broadcasted_iota(jnp.int32, (L, L), 0) - lax.broadcasted_iota(jnp.int32, (L, L), 1))

    def score_tile(blocks, g, hd, st_buf, cm_buf):
        j, _ = blocks[g]
        st = _nt_dot(k_tile(j), q_tile(hd))
        if g == KV_GROUP - 1:
            st = jnp.where(key_minus_qry <= (i - j) * L, st, NEG)
        st_buf[g, hd] = st
        cm_buf[g, hd] = jnp.max(st, axis=0, keepdims=True)

    def stage(t, cur, nxt):
        blocks = _group_blocks(i, t)
        nxt_blocks = _group_blocks(i, t + 1)
        for hd in range(HEADS_PER_STEP):
            biases = [tile_bias(hd, j, real) for j, real in blocks]
            m_old = m_ref[hd]
            m_new = m_old
            for g in range(KV_GROUP):
                m_new = jnp.maximum(m_new, cur[1][g, hd] + biases[g])
            alpha = jnp.exp2(m_old - m_new)
            l_new = alpha * l_ref[hd]
            pv = None
            for g, (j, _) in enumerate(blocks):
                score_tile(nxt_blocks, g, hd, *nxt)
                p = jnp.exp2(cur[0][g, hd] - (m_new - biases[g]))
                l_new = l_new + jnp.sum(p, axis=0, keepdims=True)
                d = jnp.dot(vt_tile(j, hd), p.astype(BF16), preferred_element_type=F32)
                pv = d if pv is None else pv + d
            m_ref[hd] = m_new
            l_ref[hd] = l_new
            acc_ref[hd] = alpha * acc_ref[hd] + pv

    m_ref[...] = jnp.full(m_ref.shape, -jnp.inf, F32)
    l_ref[...] = jnp.zeros(l_ref.shape, F32)
    acc_ref[...] = jnp.zeros(acc_ref.shape, F32)
    n_groups = (i + KV_GROUP) // KV_GROUP
    buf_a, buf_b = (st_a, cm_a), (st_b, cm_b)
    first = _group_blocks(i, 0)
    for g in range(KV_GROUP):
        for hd in range(HEADS_PER_STEP):
            score_tile(first, g, hd, *buf_a)

    def body(tt, carry):
        t = 2 * tt
        stage(t, buf_a, buf_b)

        @pl.when(t + 1 < n_groups)
        def _():
            stage(t + 1, buf_b, buf_a)

        return carry

    lax.fori_loop(0, (n_groups + 1) // 2, body, 0)


def _attn_scratch(dv):
    L = MOBA_BLOCK
    st = pltpu.VMEM((KV_GROUP, HEADS_PER_STEP, L, L), F32)
    cm = pltpu.VMEM((KV_GROUP, HEADS_PER_STEP, 1, L), F32)
    vec = pltpu.VMEM((HEADS_PER_STEP, 1, L), F32)
    return [st, cm, st, cm, vec, vec, pltpu.VMEM((HEADS_PER_STEP, dv, L), F32)]


def _moba_kernel(q_ref, k_ref, vt_ref, kmean_ref, o_ref, qh_ref, bias_ref, *scratch):
    L = MOBA_BLOCK
    nb = kmean_ref.shape[1]
    i = pl.program_id(2)
    q = q_ref[0]
    lane = lax.broadcasted_iota(jnp.int32, (L, LANES), 1)
    kmean = kmean_ref[0, :, 0, :]
    blk = lax.broadcasted_iota(jnp.int32, (nb, L), 0)
    past = blk < i

    for hd in range(HEADS_PER_STEP):
        own = (lane >= hd * MOBA_HEAD_DIM) & (lane < (hd + 1) * MOBA_HEAD_DIM)
        qh = jnp.where(own, q, jnp.zeros_like(q))
        qh_ref[hd] = qh
        gate = _nt_dot(kmean, qh.astype(F32), precision=lax.Precision.HIGHEST)
        gate = jnp.where(past, gate, -jnp.inf)
        rank = jnp.zeros((nb, L), jnp.int32)
        for j in range(nb):
            gj = gate[j:j + 1, :]
            ahead = (gj > gate) | ((gj == gate) & (j < blk))
            rank = rank + ahead.astype(jnp.int32)
        bias_ref[hd] = jnp.where((blk == i) | (past & (rank < MOBA_TOPK)), 0.0, NEG)

    _flash_attention(
        i,
        q_tile=lambda hd: qh_ref[hd],
        k_tile=lambda j: k_ref[0, pl.ds(pl.multiple_of(j * L, L), L), :],
        vt_tile=lambda j, hd: vt_ref[0, 0, j, pl.ds(hd * MOBA_HEAD_DIM, MOBA_HEAD_DIM), :],
        tile_bias=lambda hd, j, real: jnp.where(real, bias_ref[hd, pl.ds(j, 1), :], NEG),
        scratch=scratch)
    m_ref, l_ref, acc_ref = scratch[-3:]

    ot = jnp.concatenate([acc_ref[hd] / l_ref[hd] for hd in range(HEADS_PER_STEP)], axis=0)
    o_ref[0] = ot.T


def _moba(qa, ka, vat, kmean):
    B, S, _ = qa.shape
    L = MOBA_BLOCK
    nb = S // L
    return pl.pallas_call(
        _moba_kernel,
        out_shape=jax.ShapeDtypeStruct((B, S, MOBA_WIDTH), F32),
        grid=(B, N_PAIRS, nb),
        in_specs=[
            pl.BlockSpec((1, L, LANES), lambda b, p, i: (b, i, p)),
            pl.BlockSpec((1, S, LANES), lambda b, p, i: (b, 0, p)),
            pl.BlockSpec((1, 1, nb, LANES, L), lambda b, p, i: (b, p, 0, 0, 0)),
            pl.BlockSpec((1, nb, 1, LANES), lambda b, p, i: (b, 0, 0, p)),
        ],
        out_specs=pl.BlockSpec((1, L, LANES), lambda b, p, i: (b, i, p)),
        scratch_shapes=[
            pltpu.VMEM((HEADS_PER_STEP, L, LANES), BF16),
            pltpu.VMEM((HEADS_PER_STEP, nb, L), F32),
        ] + _attn_scratch(MOBA_HEAD_DIM),
        compiler_params=_cparams(("arbitrary", "arbitrary", "arbitrary")),
        name="moba",
    )(qa, ka, vat, kmean)


def _mla_kernel(q_ref, k_ref, vt_ref, o_ref, *scratch):
    L = MOBA_BLOCK
    i = pl.program_id(2)
    _flash_attention(
        i,
        q_tile=lambda hd: q_ref[0, hd],
        k_tile=lambda j: k_ref[0, 0, pl.ds(pl.multiple_of(j * L, L), L), :],
        vt_tile=lambda j, hd: vt_ref[0, 0, j, pl.ds(hd * MLA_V_DIM, MLA_V_DIM), :],
        tile_bias=lambda hd, j, real: jnp.where(real, 0.0, NEG),
        scratch=scratch)
    m_ref, l_ref, acc_ref = scratch[-3:]

    ot = jnp.concatenate([acc_ref[hd] / l_ref[hd] for hd in range(HEADS_PER_STEP)], axis=0)
    o_ref[0] = ot.T


def _mla(qcat, kcat, vmt):
    B, _, S, W = qcat.shape
    L = MOBA_BLOCK
    nb = S // L
    return pl.pallas_call(
        _mla_kernel,
        out_shape=jax.ShapeDtypeStruct((B, S, MLA_WIDTH), F32),
        grid=(B, N_PAIRS, nb),
        in_specs=[
            pl.BlockSpec((1, HEADS_PER_STEP, L, W), lambda b, p, i: (b, p, i, 0)),
            pl.BlockSpec((1, 1, S, W), lambda b, p, i: (b, p, 0, 0)),
            pl.BlockSpec((1, 1, nb, LANES, L), lambda b, p, i: (b, p, 0, 0, 0)),
        ],
        out_specs=pl.BlockSpec((1, L, LANES), lambda b, p, i: (b, i, p)),
        scratch_shapes=_attn_scratch(MLA_V_DIM),
        compiler_params=_cparams(("arbitrary", "arbitrary", "arbitrary")),
        name="mla",
    )(qcat, kcat, vmt)


def _postattn_kernel(x_ref, om_ref, ol_ref, mod_ref, gom_ref, gol_ref, wo_ref, gpost_ref, gpre_ref,
                     wr_ref, br_ref, x1_ref, xl_ref, cnt_ref, ri_ref):
    tm = x_ref.shape[1]
    gt_a = mod_ref[0, 2:3, :]
    sh_f = mod_ref[0, 3:4, :]
    sc_f = mod_ref[0, 4:5, :]
    a = jnp.concatenate([_rms(om_ref[0], gom_ref[...]), _rms(ol_ref[0], gol_ref[...])], axis=-1)
    o = jnp.dot(a.astype(BF16), wo_ref[...], preferred_element_type=F32)
    x1 = x_ref[0] + gt_a * _rms(o, gpost_ref[...])
    x1_ref[0] = x1
    h2 = _rms(x1, gpre_ref[...]) * (1.0 + sc_f) + sh_f

    logit = jnp.dot(h2, wr_ref[...], preferred_element_type=F32, precision=lax.Precision.HIGHEST)
    biased = logit + br_ref[...]
    lane = lax.broadcasted_iota(jnp.int32, (tm, LANES), 1)
    big = jnp.int32(LANES)

    def first_argmax(v):
        mx = jnp.max(v, axis=-1, keepdims=True)
        return jnp.min(jnp.where(v == mx, lane, big), axis=-1, keepdims=True)

    def pick(v, idx):
        return jnp.sum(jnp.where(lane == idx, v, 0.0), axis=-1, keepdims=True)

    gmask = (lane >= N_EXPERTS) & (lane < N_EXPERTS + N_GROUPS)
    gl = jnp.where(gmask, logit, -jnp.inf)
    pg = jnp.exp(gl - jnp.max(gl, axis=-1, keepdims=True))
    g_lane = first_argmax(jnp.where(gmask, biased, -jnp.inf))
    g_prob = pick(pg, g_lane) / jnp.sum(pg, axis=-1, keepdims=True)
    e_lo = (g_lane - N_EXPERTS) * EXPERTS_PER_GROUP
    emask = (lane >= e_lo) & (lane < e_lo + EXPERTS_PER_GROUP)
    el = jnp.where(emask, logit, -jnp.inf)
    pe = jnp.exp(el - jnp.max(el, axis=-1, keepdims=True))
    eb = jnp.where(emask, biased, -jnp.inf)
    i1 = first_argmax(eb)
    i2 = first_argmax(jnp.where(lane == i1, -jnp.inf, eb))
    p1 = pick(pe, i1)
    p2 = pick(pe, i2)
    denom = p1 + p2
    w1 = p1 / denom * g_prob
    w2 = p2 / denom * g_prob

    hit1 = lane == i1
    hit2 = lane == i2
    onehot = jnp.where(hit1 | hit2, 1.0, 0.0).astype(BF16)
    tok_r = lax.broadcasted_iota(jnp.int32, (tm, tm), 0)
    tok_c = lax.broadcasted_iota(jnp.int32, (tm, tm), 1)
    csum = jnp.dot(jnp.where(tok_c <= tok_r, 1.0, 0.0).astype(BF16), onehot,
                   preferred_element_type=F32)
    cnt = csum[tm - 1:tm, :]
    seg = jnp.floor((cnt + (CHUNK - 1.0)) * (1.0 / CHUNK)) * CHUNK
    ex_r = lax.broadcasted_iota(jnp.int32, (LANES, LANES), 0)
    ex_c = lax.broadcasted_iota(jnp.int32, (LANES, LANES), 1)
    seg_start = jnp.dot(jnp.broadcast_to(seg, (8, LANES)).astype(BF16),
                        jnp.where(ex_r < ex_c, 1.0, 0.0).astype(BF16),
                        preferred_element_type=F32)[0:1, :]
    place = seg_start + csum - 1.0
    slot1 = jnp.sum(jnp.where(hit1, place, 0.0), axis=-1, keepdims=True)
    slot2 = jnp.sum(jnp.where(hit2, place, 0.0), axis=-1, keepdims=True)
    ri_ref[0] = jnp.where(lane == 0, i1, jnp.where(lane == 1, i2, jnp.where(
        lane == 2, slot1.astype(jnp.int32), jnp.where(lane == 3, slot2.astype(jnp.int32), 0))))
    cnt_ref[0, 0] = cnt

    packed = jnp.where(lane == 0, slot1, jnp.where(lane == 1, slot2, jnp.where(
        lane == 2, w1, jnp.where(lane == 3, w2, 0.0))))
    packed_t = packed.T
    srow = lax.broadcasted_iota(jnp.int32, (LOCAL_ROWS, tm), 0).astype(F32)
    at1 = srow == packed_t[0:1, :]
    at2 = srow == packed_t[1:2, :]
    perm = jnp.where(at1 | at2, 1.0, 0.0).astype(BF16)
    xl_ref[0, :, 0:D_MODEL] = jnp.dot(perm, h2.astype(BF16), preferred_element_type=F32)
    gate = jnp.sum(jnp.where(at1, packed_t[2:3, :], 0.0) + jnp.where(at2, packed_t[3:4, :], 0.0),
                   axis=-1, keepdims=True)
    xl_ref[0, :, D_MODEL:D_MODEL + LANES] = jnp.broadcast_to(gate, (LOCAL_ROWS, LANES))


def _postattn(x, o_moba, o_mla, mod3, g_om, g_ol, w_o, g_post, g_pre, w_r, b_r):
    B, S, D = x.shape
    tm = TM_IN
    assert S % tm == 0 and D == D_MODEL
    const2 = lambda b, i: (0, 0)
    tok3 = lambda b, i: (b, i, 0)
    n_j = S // tm
    return pl.pallas_call(
        _postattn_kernel,
        out_shape=[
            jax.ShapeDtypeStruct((B, S, D), F32),
            jax.ShapeDtypeStruct((B * n_j, LOCAL_ROWS, D + LANES), F32),
            jax.ShapeDtypeStruct((B * n_j, 1, 1, LANES), F32),
            jax.ShapeDtypeStruct((B, S, LANES), jnp.int32),
        ],
        grid=(B, S // tm),
        in_specs=[
            pl.BlockSpec((1, tm, D), tok3),
            pl.BlockSpec((1, tm, MOBA_WIDTH), tok3),
            pl.BlockSpec((1, tm, MLA_WIDTH), tok3),
            pl.BlockSpec((1, 6, D), lambda b, i: (b, 0, 0)),
            pl.BlockSpec((1, MOBA_WIDTH), const2),
            pl.BlockSpec((1, MLA_WIDTH), const2),
            pl.BlockSpec(w_o.shape, const2),
            pl.BlockSpec((1, D), const2),
            pl.BlockSpec((1, D), const2),
            pl.BlockSpec(w_r.shape, const2),
            pl.BlockSpec((1, LANES), const2),
        ],
        out_specs=[
            pl.BlockSpec((1, tm, D), tok3),
            pl.BlockSpec((1, LOCAL_ROWS, D + LANES), lambda b, i: (b * n_j + i, 0, 0)),
            pl.BlockSpec((1, 1, 1, LANES), lambda b, i: (b * n_j + i, 0, 0, 0)),
            pl.BlockSpec((1, tm, LANES), tok3),
        ],
        compiler_params=_cparams(("arbitrary", "arbitrary")),
        name="postattn",
    )(x, o_moba, o_mla, mod3, g_om, g_ol, w_o, g_post, g_pre, w_r, b_r)


def _start_chunk_gather(idx_ref, base, n_chunks, src_hbm, dst_ref, sem):
    for k in range(n_chunks):
        row = pl.multiple_of(idx_ref[base + k], CHUNK)
        pltpu.make_async_copy(src_hbm.at[pl.ds(row, CHUNK), :], dst_ref.at[pl.ds(k * CHUNK, CHUNK), :], sem).start()


def _wait_chunk_gather(n_chunks, src_hbm, dst_ref, sem):
    pltpu.make_async_copy(src_hbm.at[pl.ds(0, n_chunks * CHUNK), :], dst_ref, sem).wait()


def _moe_kernel(te_ref, nu_ref, src_ref, xl_hbm, wgu_ref, wd_ref, y_ref, xbuf, sem):
    tm = y_ref.shape[0]
    n_chunks = tm // CHUNK
    i = pl.program_id(0)
    n_used = nu_ref[0]
    slot = i % 2

    @pl.when((i == 0) & (n_used > 0))
    def _():
        _start_chunk_gather(src_ref, 0, n_chunks, xl_hbm, xbuf.at[0], sem.at[0])

    @pl.when(i + 1 < n_used)
    def _():
        _start_chunk_gather(src_ref, (i + 1) * n_chunks, n_chunks, xl_hbm, xbuf.at[1 - slot], sem.at[1 - slot])

    @pl.when(i < n_used)
    def _():
        _wait_chunk_gather(n_chunks, xl_hbm, xbuf.at[slot], sem.at[slot])
        xb = xbuf[slot, :, 0:D_MODEL].astype(BF16)
        gate = xbuf[slot, :, D_MODEL:D_MODEL + LANES]
        gu = jnp.dot(xb, wgu_ref[0], preferred_element_type=F32)
        g = gu[:, :D_EXPERT]
        u = gu[:, D_EXPERT:]
        hid = g * jax.nn.sigmoid(g) * u
        y = jnp.dot(hid.astype(BF16), wd_ref[0], preferred_element_type=F32)
        y_ref[...] = jnp.concatenate([gate] * (D_MODEL // LANES), axis=1) * y

    @pl.when(i >= n_used)
    def _():
        y_ref[...] = jnp.zeros_like(y_ref)


def _moe(tile_expert, n_used, chunk_src, x_local, w_gu, w_d):
    D = D_MODEL
    n_tiles = tile_expert.shape[0]
    tm = TM_MOE
    grid_spec = pltpu.PrefetchScalarGridSpec(
        num_scalar_prefetch=3,
        grid=(n_tiles,),
        in_specs=[
            pl.BlockSpec(memory_space=pl.ANY),
            pl.BlockSpec((1, D, 2 * D_EXPERT), lambda i, te, nu, src: (te[i], 0, 0)),
            pl.BlockSpec((1, D_EXPERT, D), lambda i, te, nu, src: (te[i], 0, 0)),
        ],
        out_specs=pl.BlockSpec((tm, D), lambda i, te, nu, src: (i, 0)),
        scratch_shapes=[pltpu.VMEM((2, tm, D + LANES), F32), pltpu.SemaphoreType.DMA((2,))],
    )
    return pl.pallas_call(
        _moe_kernel,
        out_shape=jax.ShapeDtypeStruct((n_tiles * tm, D), F32),
        grid_spec=grid_spec,
        compiler_params=_cparams(("arbitrary",)),
        name="moe",
    )(tile_expert, n_used, chunk_src, x_local, w_gu, w_d)


def _combine_kernel(src_ref, y_hbm, x1_ref, ri_ref, mod_ref, g_ref, o_ref, ybuf, sem):
    tm = x1_ref.shape[1]
    n_chunks = LOCAL_ROWS // CHUNK
    n_j = pl.num_programs(1)
    step = pl.program_id(0) * n_j + pl.program_id(1)
    n_steps = pl.num_programs(0) * n_j
    slot = step % 2

    @pl.when(step == 0)
    def _():
        _start_chunk_gather(src_ref, 0, n_chunks, y_hbm, ybuf.at[0], sem.at[0])

    @pl.when(step + 1 < n_steps)
    def _():
        _start_chunk_gather(src_ref, (step + 1) * n_chunks, n_chunks, y_hbm, ybuf.at[1 - slot], sem.at[1 - slot])

    _wait_chunk_gather(n_chunks, y_hbm, ybuf.at[slot], sem.at[slot])
    yl = ybuf[slot]
    y_hi = yl.astype(BF16)
    y_lo = (yl - y_hi.astype(F32)).astype(BF16)
    col = lax.broadcasted_iota(jnp.int32, (tm, LOCAL_ROWS), 1)
    pick = jnp.where((col == ri_ref[0, :, 2:3]) | (col == ri_ref[0, :, 3:4]), 1.0, 0.0).astype(BF16)
    y = jnp.dot(pick, y_hi, preferred_element_type=F32) + jnp.dot(pick, y_lo, preferred_element_type=F32)
    gt_f = mod_ref[0, 5:6, :]
    o_ref[0] = x1_ref[0] + gt_f * _rms(y, g_ref[...])


def _combine(chunk_src, y_sorted, x1, route_i, mod3, g_post):
    B, S, D = x1.shape
    tm = TM_IN
    grid_spec = pltpu.PrefetchScalarGridSpec(
        num_scalar_prefetch=1,
        grid=(B, S // tm),
        in_specs=[
            pl.BlockSpec(memory_space=pl.ANY),
            pl.BlockSpec((1, tm, D), lambda b, j, src: (b, j, 0)),
            pl.BlockSpec((1, tm, LANES), lambda b, j, src: (b, j, 0)),
            pl.BlockSpec((1, 6, D), lambda b, j, src: (b, 0, 0)),
            pl.BlockSpec((1, D), lambda b, j, src: (0, 0)),
        ],
        out_specs=pl.BlockSpec((1, tm, D), lambda b, j, src: (b, j, 0)),
        scratch_shapes=[pltpu.VMEM((2, LOCAL_ROWS, D), F32), pltpu.SemaphoreType.DMA((2,))],
    )
    return pl.pallas_call(
        _combine_kernel,
        out_shape=jax.ShapeDtypeStruct((B, S, D), F32),
        grid_spec=grid_spec,
        compiler_params=_cparams(("arbitrary", "arbitrary")),
        name="combine",
    )(chunk_src, y_sorted, x1, route_i, mod3, g_post)


def _rope_table(positions, dim, period_fill):
    half = dim // 2
    inv_freq = jnp.exp(-math.log(ROPE_THETA) * jnp.arange(half, dtype=F32) * (2.0 / dim))
    ang = positions.astype(F32)[..., None] * inv_freq
    cos, sin = jnp.cos(ang), jnp.sin(ang)
    B, S = positions.shape
    rest = period_fill - dim
    one = jnp.ones((B, S, rest), F32)
    zero_r = jnp.zeros((B, S, rest), F32)
    zero_h = jnp.zeros((B, S, half), F32)
    c = jnp.concatenate([cos, cos, one], axis=-1)
    s1 = jnp.concatenate([-sin, zero_h, zero_r], axis=-1)
    s2 = jnp.concatenate([zero_h, sin, zero_r], axis=-1)
    reps = LANES // period_fill
    return jnp.concatenate([jnp.tile(t, (1, 1, reps)) for t in (c, s1, s2)], axis=-1)


def _routing_metadata(counts, tm):
    i32 = jnp.int32
    n_tt = counts.shape[0]
    n_tiles = (2 * n_tt * TM_IN + n_tt * N_EXPERTS * (CHUNK - 1)) // tm + N_EXPERTS + 1
    seg = (counts + CHUNK - 1) // CHUNK * CHUNK
    loc_start = jnp.cumsum(seg, axis=1) - seg
    loc_total = jnp.sum(seg, axis=1)
    run = jnp.cumsum(seg, axis=0)
    rows_e = run[-1]
    tiles_per = (rows_e + tm - 1) // tm
    tile_end = jnp.cumsum(tiles_per)
    n_used = tile_end[-1]
    first_row = (tile_end - tiles_per) * tm
    tile_ids = jnp.arange(n_tiles, dtype=i32)
    te = jnp.minimum(jnp.sum((tile_ids[:, None] >= tile_end[None, :]).astype(i32), axis=1), N_EXPERTS - 1)
    last_e = jnp.sum(jnp.where(tile_ids == n_used - 1, te, 0))
    tile_expert = jnp.where(tile_ids < n_used, te, last_e).astype(i32)

    cpt = tm // CHUNK
    chunk_ids = jnp.arange(n_tiles * cpt, dtype=i32)
    e_u = jnp.repeat(tile_expert, cpt)
    off_u = chunk_ids * CHUNK - first_row[e_u]
    run_u = run.T[e_u]
    tau_u = jnp.minimum(jnp.sum((off_u[:, None] >= run_u).astype(i32), axis=1), n_tt - 1)
    pick_tau = lambda a: jnp.take_along_axis(a.T[e_u], tau_u[:, None], axis=1)[:, 0]
    src_u = tau_u * LOCAL_ROWS + pick_tau(loc_start) + off_u - (pick_tau(run) - pick_tau(seg))
    real_u = (jnp.repeat(tile_ids, cpt) < n_used) & (off_u < rows_e[e_u])
    zero_local = LOCAL_ROWS - CHUNK
    moe_src = jnp.where(real_u, src_u, zero_local).astype(i32)

    lrow = jnp.arange(LOCAL_ROWS // CHUNK, dtype=i32) * CHUNK
    loc_end = loc_start + seg
    e_v = jnp.minimum(jnp.sum((lrow[None, :, None] >= loc_end[:, None, :]).astype(i32), axis=2), N_EXPERTS - 1)
    seg_first = first_row[None, :] + run - seg
    src_v = (jnp.take_along_axis(seg_first, e_v, axis=1) + lrow[None, :]
             - jnp.take_along_axis(loc_start, e_v, axis=1))
    zero_sorted = (n_tiles - 1) * tm
    cmb_src = jnp.where(lrow[None, :] < loc_total[:, None], src_v, zero_sorted).astype(i32).reshape(-1)
    return tile_expert, n_used.reshape(1).astype(i32), moe_src, cmb_src


def kernel(x, c, positions, w_ada, b_ada, g_pre_attn, g_post_attn, g_pre_ffn, g_post_ffn, w_in, g_cq, w_uq,
           g_ckv, w_ukv, g_out_moba, g_out_mla, w_o, w_group, b_group, w_expert, b_expert, w_e_gate, w_e_up,
           w_e_down):
    B, S, D = x.shape
    depth = w_ada.shape[0]
    rope_a = _rope_table(positions, MOBA_ROT_DIMS, MOBA_HEAD_DIM)
    rope_m = _rope_table(positions, MLA_ROPE_DIM, MLA_ROPE_DIM)
    row = lambda v: v.reshape(1, -1)

    for l in range(depth):
        n_main = 3 * MOBA_WIDTH + MLA_Q_RANK + MLA_KV_RANK
        w_in_p = jnp.concatenate([w_in[l][:, :n_main], jnp.tile(w_in[l][:, n_main:], (1, LANES // MLA_ROPE_DIM))],
                                 axis=1).astype(BF16)
        wq = w_uq[l].reshape(MLA_Q_RANK, MLA_HEADS, MLA_NOPE_DIM + MLA_ROPE_DIM)
        wq_n = wq[:, :, :MLA_NOPE_DIM]
        zero_n = jnp.zeros_like(wq_n)
        even = (jnp.arange(MLA_HEADS) % 2 == 0)[None, :, None]
        w_uq_n = jnp.concatenate([jnp.where(even, wq_n, zero_n), jnp.where(even, zero_n, wq_n)], axis=-1)
        w_uq_n = w_uq_n.reshape(MLA_Q_RANK, MLA_HEADS * LANES).astype(BF16)
        w_uq_p = wq[:, :, MLA_NOPE_DIM:].reshape(MLA_Q_RANK, MLA_HEADS * MLA_ROPE_DIM).astype(BF16)
        wkv = w_ukv[l].reshape(MLA_KV_RANK, MLA_HEADS, MLA_NOPE_DIM + MLA_V_DIM)
        w_ukv_p = jnp.concatenate([wkv[:, :, :MLA_NOPE_DIM].reshape(MLA_KV_RANK, -1),
                                   wkv[:, :, MLA_NOPE_DIM:].reshape(MLA_KV_RANK, -1)], axis=1).astype(BF16)
        w_r = jnp.concatenate([w_expert[l], w_group[l],
                               jnp.zeros((D, LANES - N_EXPERTS - N_GROUPS), F32)], axis=1)
        b_r = jnp.concatenate([b_expert[l], b_group[l], jnp.zeros((LANES - N_EXPERTS - N_GROUPS,), F32)]).reshape(1, LANES)
        w_gu = jnp.concatenate([w_e_gate[l], w_e_up[l]], axis=-1).astype(BF16)
        w_d = w_e_down[l].astype(BF16)

        mod3 = _adaln(c, w_ada[l], b_ada[l]).reshape(B, 6, D)

        qa, ka, vat, kmean, qcat, kcat, vmt = _inproj(
            x, mod3, row(g_pre_attn[l]), w_in_p, rope_a, rope_m, row(g_cq[l]), w_uq_n, w_uq_p,
            row(g_ckv[l]), w_ukv_p)
        o_moba = _moba(qa, ka, vat, kmean)
        o_mla = _mla(qcat, kcat, vmt)
        x1, x_local, counts, route_i = _postattn(
            x, o_moba, o_mla, mod3, row(g_out_moba[l]), row(g_out_mla[l]), w_o[l].astype(BF16),
            row(g_post_attn[l]), row(g_pre_ffn[l]), w_r, b_r)

        counts = counts.reshape(-1, LANES)[:, :N_EXPERTS].astype(jnp.int32)
        tile_expert, n_used, moe_src, cmb_src = _routing_metadata(counts, TM_MOE)
        y_sorted = _moe(tile_expert, n_used, moe_src, x_local.reshape(-1, D + LANES), w_gu, w_d)
        x = _combine(cmb_src, y_sorted, x1, route_i, mod3, row(g_post_ffn[l]))
    return x
```

```python
import functools
import math

import jax
import jax.numpy as jnp
import numpy as np
from jax import lax
from jax.experimental import pallas as pl
from jax.experimental.pallas import tpu as pltpu

F32 = jnp.float32
BF16 = jnp.bfloat16

D_MODEL = 1024
MOBA_HEADS = 8
MOBA_HEAD_DIM = 64
MOBA_WIDTH = MOBA_HEADS * MOBA_HEAD_DIM
MOBA_BLOCK = 256
MOBA_TOPK = 3
MOBA_ROT_DIMS = MOBA_HEAD_DIM // 4
MLA_HEADS = 8
MLA_Q_RANK = 256
MLA_KV_RANK = 128
MLA_NOPE_DIM = 64
MLA_ROPE_DIM = 32
MLA_V_DIM = 64
MLA_WIDTH = MLA_HEADS * MLA_V_DIM
ROPE_THETA = 500000.0
N_GROUPS = 4
EXPERTS_PER_GROUP = 8
N_EXPERTS = N_GROUPS * EXPERTS_PER_GROUP
D_EXPERT = 256
EPS = 1e-6
NEG = -1e30
LOG2E = math.log2(math.e)

LANES = 128
HEADS_PER_STEP = LANES // MOBA_HEAD_DIM
N_PAIRS = MOBA_HEADS // HEADS_PER_STEP
VMEM_LIMIT = 56 * 1024 * 1024

TM_IN = 512
KV_GROUP = 4
TM_MOE = 256
CHUNK = 8
LOCAL_ROWS = 2 * TM_IN + N_EXPERTS * CHUNK


def _cparams(sem):
    return pltpu.CompilerParams(dimension_semantics=sem, vmem_limit_bytes=VMEM_LIMIT)


def _rms(x, g):
    return x * lax.rsqrt(jnp.mean(x * x, axis=-1, keepdims=True) + EPS) * g


def _adaln_kernel(c_ref, w_ref, b_ref, o_ref):
    c = c_ref[...]
    ca = c * jax.nn.sigmoid(c)
    o_ref[...] = jnp.dot(ca, w_ref[...], preferred_element_type=F32,
                         precision=lax.Precision.HIGHEST) + b_ref[...]


def _adaln(c, w, b):
    B, D = c.shape
    N = w.shape[1]
    tn = 1536 if N % 1536 == 0 else N
    return pl.pallas_call(
        _adaln_kernel,
        out_shape=jax.ShapeDtypeStruct((B, N), F32),
        grid=(N // tn,),
        in_specs=[pl.BlockSpec((B, D), lambda j: (0, 0)),
                  pl.BlockSpec((D, tn), lambda j: (0, j)),
                  pl.BlockSpec((1, tn), lambda j: (0, j))],
        out_specs=pl.BlockSpec((B, tn), lambda j: (0, j)),
        compiler_params=_cparams(("arbitrary",)),
        name="adaln",
    )(c, w, b.reshape(1, N))


def _rope128(xc, tab, shift):
    c = tab[:, 0:LANES]
    s1 = tab[:, LANES:2 * LANES]
    s2 = tab[:, 2 * LANES:3 * LANES]
    return xc * c + pltpu.roll(xc, LANES - shift, axis=1) * s1 + pltpu.roll(xc, shift, axis=1) * s2


def _rope_expansion():
    e = np.zeros((LANES, 6 * LANES), np.float32)
    ha, hm = MOBA_ROT_DIMS // 2, MLA_ROPE_DIM // 2
    cos_a, sin_a, cos_m, sin_m, one = 0, ha, 2 * ha, 2 * ha + hm, 2 * ha + 2 * hm
    for lane in range(LANES):
        d = lane % MOBA_HEAD_DIM
        if d < ha:
            e[cos_a + d, lane] = 1.0
            e[sin_a + d, LANES + lane] = -1.0
        elif d < 2 * ha:
            e[cos_a + d - ha, lane] = 1.0
            e[sin_a + d - ha, 2 * LANES + lane] = 1.0
        else:
            e[one, lane] = 1.0
        d = lane % MLA_ROPE_DIM
        e[cos_m + d % hm, 3 * LANES + lane] = 1.0
        if d < hm:
            e[sin_m + d, 4 * LANES + lane] = -1.0
        else:
            e[sin_m + d - hm, 5 * LANES + lane] = 1.0
    return e


def _expand_exact(v, e_ref):
    hi = v.astype(BF16)
    r1 = v - hi.astype(F32)
    mid = r1.astype(BF16)
    lo = (r1 - mid.astype(F32)).astype(BF16)
    e = e_ref[...]
    return (jnp.dot(hi, e, preferred_element_type=F32) + jnp.dot(mid, e, preferred_element_type=F32)
            + jnp.dot(lo, e, preferred_element_type=F32))


def _store_vt(src, col0, vt_ref):
    for p in range(N_PAIRS):
        for r in range(src.shape[0] // MOBA_BLOCK):
            blk = src[r * MOBA_BLOCK:(r + 1) * MOBA_BLOCK, col0 + p * LANES:col0 + (p + 1) * LANES]
            vt_ref[0, p, r] = blk.T.astype(BF16)


def _inproj_kernel(x_ref, mod_ref, gpre_ref, win_ref, cs_ref, rexp_ref, gcq_ref, wuqn_ref, wuqp_ref,
                   gckv_ref, wukv_ref,
                   qa_ref, ka_ref, vat_ref, kmean_ref, qcat_ref, kcat_ref, vmt_ref):
    tm = x_ref.shape[1]
    rope_tab = _expand_exact(cs_ref[0], rexp_ref)
    tab_a = rope_tab[:, 0:3 * LANES]
    tab_m = rope_tab[:, 3 * LANES:6 * LANES]
    x = x_ref[0]
    sh = mod_ref[0, 0:1, :]
    sc = mod_ref[0, 1:2, :]
    h = _rms(x, gpre_ref[...]) * (1.0 + sc) + sh
    z = jnp.dot(h.astype(BF16), win_ref[...], preferred_element_type=F32)

    moba_scale = MOBA_HEAD_DIM ** -0.5 * LOG2E
    half_a = MOBA_ROT_DIMS // 2
    n_chunks = MOBA_WIDTH // LANES
    for cidx in range(n_chunks):
        lo = cidx * LANES
        q = _rope128(z[:, lo:lo + LANES], tab_a, half_a)
        qa_ref[0, :, lo:lo + LANES] = (q * moba_scale).astype(BF16)
        k = _rope128(z[:, MOBA_WIDTH + lo:MOBA_WIDTH + lo + LANES], tab_a, half_a)
        ka_ref[0, :, lo:lo + LANES] = k.astype(BF16)
        for r in range(tm // MOBA_BLOCK):
            kmean_ref[0, r, :, lo:lo + LANES] = jnp.mean(
                k[r * MOBA_BLOCK:(r + 1) * MOBA_BLOCK], axis=0, keepdims=True)
    _store_vt(z, 2 * MOBA_WIDTH, vat_ref)

    o_cq = 3 * MOBA_WIDTH
    o_ckv = o_cq + MLA_Q_RANK
    o_kpe = o_ckv + MLA_KV_RANK
    mla_scale = (MLA_NOPE_DIM + MLA_ROPE_DIM) ** -0.5 * LOG2E
    half_m = MLA_ROPE_DIM // 2

    cqn = _rms(z[:, o_cq:o_cq + MLA_Q_RANK], gcq_ref[...]).astype(BF16)
    qn = jnp.dot(cqn, wuqn_ref[...], preferred_element_type=F32) * mla_scale
    qp = jnp.dot(cqn, wuqp_ref[...], preferred_element_type=F32)
    lane = lax.broadcasted_iota(jnp.int32, (tm, LANES), 1)
    heads_per_chunk = LANES // MLA_ROPE_DIM
    qp_chunks = [_rope128(qp[:, g * LANES:(g + 1) * LANES], tab_m, half_m) * mla_scale
                 for g in range(MLA_HEADS // heads_per_chunk)]
    for hd in range(MLA_HEADS):
        qcat_ref[0, hd, :, 0:LANES] = qn[:, hd * LANES:(hd + 1) * LANES].astype(BF16)
        sub = hd % heads_per_chunk
        keep = (lane >= sub * MLA_ROPE_DIM) & (lane < (sub + 1) * MLA_ROPE_DIM)
        qcat_ref[0, hd, :, LANES:2 * LANES] = jnp.where(keep, qp_chunks[hd // heads_per_chunk], 0.0).astype(BF16)

    ckvn = _rms(z[:, o_ckv:o_ckv + MLA_KV_RANK], gckv_ref[...]).astype(BF16)
    kv = jnp.dot(ckvn, wukv_ref[...], preferred_element_type=F32)
    kpe = _rope128(z[:, o_kpe:o_kpe + LANES], tab_m, half_m).astype(BF16)
    for p in range(N_PAIRS):
        kcat_ref[0, p, :, 0:LANES] = kv[:, p * LANES:(p + 1) * LANES].astype(BF16)
        kcat_ref[0, p, :, LANES:2 * LANES] = kpe
    _store_vt(kv, MLA_WIDTH, vmt_ref)


def _inproj(x, mod3, g_pre, w_in_p, rope_cs, rope_exp, g_cq, w_uq_n, w_uq_p, g_ckv, w_ukv_p):
    B, S, D = x.shape
    tm = min(TM_IN, S)
    nb = S // MOBA_BLOCK
    rb = tm // MOBA_BLOCK
    const2 = lambda b, i: (0, 0)
    tok3 = lambda b, i: (b, i, 0)
    outs = pl.pallas_call(
        _inproj_kernel,
        out_shape=[
            jax.ShapeDtypeStruct((B, S, MOBA_WIDTH), BF16),
            jax.ShapeDtypeStruct((B, S, MOBA_WIDTH), BF16),
            jax.ShapeDtypeStruct((B, N_PAIRS, nb, LANES, MOBA_BLOCK), BF16),
            jax.ShapeDtypeStruct((B, nb, 1, MOBA_WIDTH), F32),
            jax.ShapeDtypeStruct((B, MLA_HEADS, S, 2 * LANES), BF16),
            jax.ShapeDtypeStruct((B, N_PAIRS, S, 2 * LANES), BF16),
            jax.ShapeDtypeStruct((B, N_PAIRS, nb, LANES, MOBA_BLOCK), BF16),
        ],
        grid=(B, S // tm),
        in_specs=[
            pl.BlockSpec((1, tm, D), tok3),
            pl.BlockSpec((1, 6, D), lambda b, i: (b, 0, 0)),
            pl.BlockSpec((1, D), const2),
            pl.BlockSpec(w_in_p.shape, const2),
            pl.BlockSpec((1, tm, LANES), tok3),
            pl.BlockSpec(rope_exp.shape, const2),
            pl.BlockSpec((1, MLA_Q_RANK), const2),
            pl.BlockSpec(w_uq_n.shape, const2),
            pl.BlockSpec(w_uq_p.shape, const2),
            pl.BlockSpec((1, MLA_KV_RANK), const2),
            pl.BlockSpec(w_ukv_p.shape, const2),
        ],
        out_specs=[
            pl.BlockSpec((1, tm, MOBA_WIDTH), tok3),
            pl.BlockSpec((1, tm, MOBA_WIDTH), tok3),
            pl.BlockSpec((1, N_PAIRS, rb, LANES, MOBA_BLOCK), lambda b, i: (b, 0, i, 0, 0)),
            pl.BlockSpec((1, rb, 1, MOBA_WIDTH), lambda b, i: (b, i, 0, 0)),
            pl.BlockSpec((1, MLA_HEADS, tm, 2 * LANES), lambda b, i: (b, 0, i, 0)),
            pl.BlockSpec((1, N_PAIRS, tm, 2 * LANES), lambda b, i: (b, 0, i, 0)),
            pl.BlockSpec((1, N_PAIRS, rb, LANES, MOBA_BLOCK), lambda b, i: (b, 0, i, 0, 0)),
        ],
        compiler_params=_cparams(("arbitrary", "arbitrary")),
        name="inproj",
    )(x, mod3, g_pre, w_in_p, rope_cs, rope_exp, g_cq, w_uq_n, w_uq_p, g_ckv, w_ukv_p)
    return outs


def _nt_dot(a, b, **kw):
    return lax.dot_general(a, b, (((1,), (1,)), ((), ())), preferred_element_type=F32, **kw)


def _group_blocks(i, t):
    out = []
    for g in range(KV_GROUP):
        j = i - KV_GROUP * t - (KV_GROUP - 1 - g)
        out.append((jnp.maximum(j, 0), j >= 0))
    return out


def _flash_attention(i, q_tile, k_tile, vt_tile, tile_bias, scratch):
    st_a, cm_a, st_b, cm_b, m_ref, l_ref, acc_ref = scratch
    L = MOBA_BLOCK
    key_minus_qry = (lax.broadcasted_iota(jnp.int32, (L, L), 0) - lax.broadcasted_iota(jnp.int32, (L, L), 1))

    def score_tile(blocks, g, hd, st_buf, cm_buf):
        j, _ = blocks[g]
        st = _nt_dot(k_tile(j), q_tile(hd))
        if g == KV_GROUP - 1:
            st = jnp.where(key_minus_qry <= (i - j) * L, st, NEG)
        st_buf[g, hd] = st
        cm_buf[g, hd] = jnp.max(st, axis=0, keepdims=True)

    def stage(t, cur, nxt):
        blocks = _group_blocks(i, t)
        nxt_blocks = _group_blocks(i, t + 1)
        for hd in range(HEADS_PER_STEP):
            biases = [tile_bias(hd, j, real) for j, real in blocks]
            m_old = m_ref[hd]
            m_new = m_old
            for g in range(KV_GROUP):
                m_new = jnp.maximum(m_new, cur[1][g, hd] + biases[g])
            alpha = jnp.exp2(m_old - m_new)
            l_new = alpha * l_ref[hd]
            pv = None
            for g, (j, _) in enumerate(blocks):
                score_tile(nxt_blocks, g, hd, *nxt)
                p = jnp.exp2(cur[0][g, hd] - (m_new - biases[g]))
                l_new = l_new + jnp.sum(p, axis=0, keepdims=True)
                d = jnp.dot(vt_tile(j, hd), p.astype(BF16), preferred_element_type=F32)
                pv = d if pv is None else pv + d
            m_ref[hd] = m_new
            l_ref[hd] = l_new
            acc_ref[hd] = alpha * acc_ref[hd] + pv

    m_ref[...] = jnp.full(m_ref.shape, -jnp.inf, F32)
    l_ref[...] = jnp.zeros(l_ref.shape, F32)
    acc_ref[...] = jnp.zeros(acc_ref.shape, F32)
    n_groups = (i + KV_GROUP) // KV_GROUP
    buf_a, buf_b = (st_a, cm_a), (st_b, cm_b)
    first = _group_blocks(i, 0)
    for g in range(KV_GROUP):
        for hd in range(HEADS_PER_STEP):
            score_tile(first, g, hd, *buf_a)

    def body(tt, carry):
        t = 2 * tt
        stage(t, buf_a, buf_b)

        @pl.when(t + 1 < n_groups)
        def _():
            stage(t + 1, buf_b, buf_a)

        return carry

    lax.fori_loop(0, (n_groups + 1) // 2, body, 0)


def _attn_scratch(dv):
    L = MOBA_BLOCK
    st = pltpu.VMEM((KV_GROUP, HEADS_PER_STEP, L, L), F32)
    cm = pltpu.VMEM((KV_GROUP, HEADS_PER_STEP, 1, L), F32)
    vec = pltpu.VMEM((HEADS_PER_STEP, 1, L), F32)
    return [st, cm, st, cm, vec, vec, pltpu.VMEM((HEADS_PER_STEP, dv, L), F32)]


def _moba_kernel(q_ref, k_ref, vt_ref, kmean_ref, o_ref, qh_ref, bias_ref, *scratch):
    L = MOBA_BLOCK
    nb = kmean_ref.shape[1]
    i = pl.program_id(2)
    q = q_ref[0]
    lane = lax.broadcasted_iota(jnp.int32, (L, LANES), 1)
    kmean = kmean_ref[0, :, 0, :]
    blk = lax.broadcasted_iota(jnp.int32, (nb, L), 0)
    past = blk < i

    for hd in range(HEADS_PER_STEP):
        own = (lane >= hd * MOBA_HEAD_DIM) & (lane < (hd + 1) * MOBA_HEAD_DIM)
        qh = jnp.where(own, q, jnp.zeros_like(q))
        qh_ref[hd] = qh
        gate = _nt_dot(kmean, qh.astype(F32), precision=lax.Precision.HIGHEST)
        gate = jnp.where(past, gate, -jnp.inf)
        rank = jnp.zeros((nb, L), jnp.int32)
        for j in range(nb):
            gj = gate[j:j + 1, :]
            ahead = (gj > gate) | ((gj == gate) & (j < blk))
            rank = rank + ahead.astype(jnp.int32)
        bias_ref[hd] = jnp.where((blk == i) | (past & (rank < MOBA_TOPK)), 0.0, NEG)

    _flash_attention(
        i,
        q_tile=lambda hd: qh_ref[hd],
        k_tile=lambda j: k_ref[0, pl.ds(pl.multiple_of(j * L, L), L), :],
        vt_tile=lambda j, hd: vt_ref[0, 0, j, pl.ds(hd * MOBA_HEAD_DIM, MOBA_HEAD_DIM), :],
        tile_bias=lambda hd, j, real: jnp.where(real, bias_ref[hd, pl.ds(j, 1), :], NEG),
        scratch=scratch)
    m_ref, l_ref, acc_ref = scratch[-3:]

    ot = jnp.concatenate([acc_ref[hd] / l_ref[hd] for hd in range(HEADS_PER_STEP)], axis=0)
    o_ref[0] = ot.T


def _moba(qa, ka, vat, kmean):
    B, S, _ = qa.shape
    L = MOBA_BLOCK
    nb = S // L
    return pl.pallas_call(
        _moba_kernel,
        out_shape=jax.ShapeDtypeStruct((B, S, MOBA_WIDTH), F32),
        grid=(B, N_PAIRS, nb),
        in_specs=[
            pl.BlockSpec((1, L, LANES), lambda b, p, i: (b, i, p)),
            pl.BlockSpec((1, S, LANES), lambda b, p, i: (b, 0, p)),
            pl.BlockSpec((1, 1, nb, LANES, L), lambda b, p, i: (b, p, 0, 0, 0)),
            pl.BlockSpec((1, nb, 1, LANES), lambda b, p, i: (b, 0, 0, p)),
        ],
        out_specs=pl.BlockSpec((1, L, LANES), lambda b, p, i: (b, i, p)),
        scratch_shapes=[
            pltpu.VMEM((HEADS_PER_STEP, L, LANES), BF16),
            pltpu.VMEM((HEADS_PER_STEP, nb, L), F32),
        ] + _attn_scratch(MOBA_HEAD_DIM),
        compiler_params=_cparams(("arbitrary", "arbitrary", "arbitrary")),
        name="moba",
    )(qa, ka, vat, kmean)


def _mla_kernel(q_ref, k_ref, vt_ref, o_ref, *scratch):
    L = MOBA_BLOCK
    i = pl.program_id(2)
    _flash_attention(
        i,
        q_tile=lambda hd: q_ref[0, hd],
        k_tile=lambda j: k_ref[0, 0, pl.ds(pl.multiple_of(j * L, L), L), :],
        vt_tile=lambda j, hd: vt_ref[0, 0, j, pl.ds(hd * MLA_V_DIM, MLA_V_DIM), :],
        tile_bias=lambda hd, j, real: jnp.where(real, 0.0, NEG),
        scratch=scratch)
    m_ref, l_ref, acc_ref = scratch[-3:]

    ot = jnp.concatenate([acc_ref[hd] / l_ref[hd] for hd in range(HEADS_PER_STEP)], axis=0)
    o_ref[0] = ot.T


def _mla(qcat, kcat, vmt):
    B, _, S, W = qcat.shape
    L = MOBA_BLOCK
    nb = S // L
    return pl.pallas_call(
        _mla_kernel,
        out_shape=jax.ShapeDtypeStruct((B, S, MLA_WIDTH), F32),
        grid=(B, N_PAIRS, nb),
        in_specs=[
            pl.BlockSpec((1, HEADS_PER_STEP, L, W), lambda b, p, i: (b, p, i, 0)),
            pl.BlockSpec((1, 1, S, W), lambda b, p, i: (b, p, 0, 0)),
            pl.BlockSpec((1, 1, nb, LANES, L), lambda b, p, i: (b, p, 0, 0, 0)),
        ],
        out_specs=pl.BlockSpec((1, L, LANES), lambda b, p, i: (b, i, p)),
        scratch_shapes=_attn_scratch(MLA_V_DIM),
        compiler_params=_cparams(("arbitrary", "arbitrary", "arbitrary")),
        name="mla",
    )(qcat, kcat, vmt)


def _postattn_kernel(x_ref, om_ref, ol_ref, mod_ref, gom_ref, gol_ref, wo_ref, gpost_ref, gpre_ref,
                     wr_ref, br_ref, x1_ref, xl_ref, cnt_ref, ri_ref):
    tm = x_ref.shape[1]
    gt_a = mod_ref[0, 2:3, :]
    sh_f = mod_ref[0, 3:4, :]
    sc_f = mod_ref[0, 4:5, :]
    a = jnp.concatenate([_rms(om_ref[0], gom_ref[...]), _rms(ol_ref[0], gol_ref[...])], axis=-1)
    o = jnp.dot(a.astype(BF16), wo_ref[...], preferred_element_type=F32)
    x1 = x_ref[0] + gt_a * _rms(o, gpost_ref[...])
    x1_ref[0] = x1
    h2 = _rms(x1, gpre_ref[...]) * (1.0 + sc_f) + sh_f

    h2_hi = h2.astype(BF16)
    h2_lo = (h2 - h2_hi.astype(F32)).astype(BF16)
    logit = (jnp.dot(h2_hi, wr_ref[0], preferred_element_type=F32)
             + jnp.dot(h2_lo, wr_ref[0], preferred_element_type=F32)
             + jnp.dot(h2_hi, wr_ref[1], preferred_element_type=F32))
    biased = logit + br_ref[...]
    lane = lax.broadcasted_iota(jnp.int32, (tm, LANES), 1)
    big = jnp.int32(LANES)

    def first_argmax(v):
        mx = jnp.max(v, axis=-1, keepdims=True)
        return jnp.min(jnp.where(v == mx, lane, big), axis=-1, keepdims=True)

    def pick(v, idx):
        return jnp.sum(jnp.where(lane == idx, v, 0.0), axis=-1, keepdims=True)

    gmask = (lane >= N_EXPERTS) & (lane < N_EXPERTS + N_GROUPS)
    gl = jnp.where(gmask, logit, -jnp.inf)
    pg = jnp.exp(gl - jnp.max(gl, axis=-1, keepdims=True))
    g_lane = first_argmax(jnp.where(gmask, biased, -jnp.inf))
    g_prob = pick(pg, g_lane) / jnp.sum(pg, axis=-1, keepdims=True)
    e_lo = (g_lane - N_EXPERTS) * EXPERTS_PER_GROUP
    emask = (lane >= e_lo) & (lane < e_lo + EXPERTS_PER_GROUP)
    el = jnp.where(emask, logit, -jnp.inf)
    pe = jnp.exp(el - jnp.max(el, axis=-1, keepdims=True))
    eb = jnp.where(emask, biased, -jnp.inf)
    i1 = first_argmax(eb)
    i2 = first_argmax(jnp.where(lane == i1, -jnp.inf, eb))
    p1 = pick(pe, i1)
    p2 = pick(pe, i2)
    denom = p1 + p2
    w1 = p1 / denom * g_prob
    w2 = p2 / denom * g_prob

    hit1 = lane == i1
    hit2 = lane == i2
    onehot = jnp.where(hit1 | hit2, 1.0, 0.0).astype(BF16)
    tok_r = lax.broadcasted_iota(jnp.int32, (tm, tm), 0)
    tok_c = lax.broadcasted_iota(jnp.int32, (tm, tm), 1)
    csum = jnp.dot(jnp.where(tok_c <= tok_r, 1.0, 0.0).astype(BF16), onehot,
                   preferred_element_type=F32)
    cnt = csum[tm - 1:tm, :]
    seg = jnp.floor((cnt + (CHUNK - 1.0)) * (1.0 / CHUNK)) * CHUNK
    ex_r = lax.broadcasted_iota(jnp.int32, (LANES, LANES), 0)
    ex_c = lax.broadcasted_iota(jnp.int32, (LANES, LANES), 1)
    seg_start = jnp.dot(jnp.broadcast_to(seg, (8, LANES)).astype(BF16),
                        jnp.where(ex_r < ex_c, 1.0, 0.0).astype(BF16),
                        preferred_element_type=F32)[0:1, :]
    place = seg_start + csum - 1.0
    slot1 = jnp.sum(jnp.where(hit1, place, 0.0), axis=-1, keepdims=True)
    slot2 = jnp.sum(jnp.where(hit2, place, 0.0), axis=-1, keepdims=True)
    ri_ref[0] = jnp.where(lane == 0, i1, jnp.where(lane == 1, i2, jnp.where(
        lane == 2, slot1.astype(jnp.int32), jnp.where(lane == 3, slot2.astype(jnp.int32), 0))))
    cnt_ref[0, 0] = cnt

    packed = jnp.where(lane == 0, slot1, jnp.where(lane == 1, slot2, jnp.where(
        lane == 2, w1, jnp.where(lane == 3, w2, 0.0))))
    packed_t = packed.T
    srow = lax.broadcasted_iota(jnp.int32, (LOCAL_ROWS, tm), 0).astype(F32)
    at1 = srow == packed_t[0:1, :]
    at2 = srow == packed_t[1:2, :]
    perm = jnp.where(at1 | at2, 1.0, 0.0).astype(BF16)
    xl_ref[0, :, 0:D_MODEL] = jnp.dot(perm, h2_hi, preferred_element_type=F32)
    gate = jnp.sum(jnp.where(at1, packed_t[2:3, :], 0.0) + jnp.where(at2, packed_t[3:4, :], 0.0),
                   axis=-1, keepdims=True)
    xl_ref[0, :, D_MODEL:D_MODEL + LANES] = jnp.broadcast_to(gate, (LOCAL_ROWS, LANES))


def _postattn(x, o_moba, o_mla, mod3, g_om, g_ol, w_o, g_post, g_pre, w_r, b_r):
    B, S, D = x.shape
    tm = TM_IN
    assert S % tm == 0 and D == D_MODEL
    const2 = lambda b, i: (0, 0)
    tok3 = lambda b, i: (b, i, 0)
    n_j = S // tm
    return pl.pallas_call(
        _postattn_kernel,
        out_shape=[
            jax.ShapeDtypeStruct((B, S, D), F32),
            jax.ShapeDtypeStruct((B * n_j, LOCAL_ROWS, D + LANES), F32),
            jax.ShapeDtypeStruct((B * n_j, 1, 1, LANES), F32),
            jax.ShapeDtypeStruct((B, S, LANES), jnp.int32),
        ],
        grid=(B, S // tm),
        in_specs=[
            pl.BlockSpec((1, tm, D), tok3),
            pl.BlockSpec((1, tm, MOBA_WIDTH), tok3),
            pl.BlockSpec((1, tm, MLA_WIDTH), tok3),
            pl.BlockSpec((1, 6, D), lambda b, i: (b, 0, 0)),
            pl.BlockSpec((1, MOBA_WIDTH), const2),
            pl.BlockSpec((1, MLA_WIDTH), const2),
            pl.BlockSpec(w_o.shape, const2),
            pl.BlockSpec((1, D), const2),
            pl.BlockSpec((1, D), const2),
            pl.BlockSpec(w_r.shape, lambda b, i: (0, 0, 0)),
            pl.BlockSpec((1, LANES), const2),
        ],
        out_specs=[
            pl.BlockSpec((1, tm, D), tok3),
            pl.BlockSpec((1, LOCAL_ROWS, D + LANES), lambda b, i: (b * n_j + i, 0, 0)),
            pl.BlockSpec((1, 1, 1, LANES), lambda b, i: (b * n_j + i, 0, 0, 0)),
            pl.BlockSpec((1, tm, LANES), tok3),
        ],
        compiler_params=_cparams(("arbitrary", "arbitrary")),
        name="postattn",
    )(x, o_moba, o_mla, mod3, g_om, g_ol, w_o, g_post, g_pre, w_r, b_r)


def _start_chunk_gather(idx_ref, base, n_chunks, src_hbm, dst_ref, sem):
    for k in range(n_chunks):
        row = pl.multiple_of(idx_ref[base + k], CHUNK)
        pltpu.make_async_copy(src_hbm.at[pl.ds(row, CHUNK), :], dst_ref.at[pl.ds(k * CHUNK, CHUNK), :], sem).start()


def _wait_chunk_gather(n_chunks, src_hbm, dst_ref, sem):
    pltpu.make_async_copy(src_hbm.at[pl.ds(0, n_chunks * CHUNK), :], dst_ref, sem).wait()


def _moe_kernel(te_ref, nu_ref, src_ref, xl_hbm, wgu_ref, wd_ref, y_ref, xbuf, sem):
    tm = y_ref.shape[0]
    n_chunks = tm // CHUNK
    i = pl.program_id(0)
    n_used = nu_ref[0]
    slot = i % 2

    @pl.when((i == 0) & (n_used > 0))
    def _():
        _start_chunk_gather(src_ref, 0, n_chunks, xl_hbm, xbuf.at[0], sem.at[0])

    @pl.when(i + 1 < n_used)
    def _():
        _start_chunk_gather(src_ref, (i + 1) * n_chunks, n_chunks, xl_hbm, xbuf.at[1 - slot], sem.at[1 - slot])

    @pl.when(i < n_used)
    def _():
        _wait_chunk_gather(n_chunks, xl_hbm, xbuf.at[slot], sem.at[slot])
        xb = xbuf[slot, :, 0:D_MODEL].astype(BF16)
        gate = xbuf[slot, :, D_MODEL:D_MODEL + LANES]
        gu = jnp.dot(xb, wgu_ref[0], preferred_element_type=F32)
        g = gu[:, :D_EXPERT]
        u = gu[:, D_EXPERT:]
        hid = g * jax.nn.sigmoid(g) * u
        y = jnp.dot(hid.astype(BF16), wd_ref[0], preferred_element_type=F32)
        y_ref[...] = jnp.concatenate([gate] * (D_MODEL // LANES), axis=1) * y

    @pl.when(i >= n_used)
    def _():
        y_ref[...] = jnp.zeros_like(y_ref)


def _moe(tile_expert, n_used, chunk_src, x_local, w_gu, w_d):
    D = D_MODEL
    n_tiles = tile_expert.shape[0]
    tm = TM_MOE
    grid_spec = pltpu.PrefetchScalarGridSpec(
        num_scalar_prefetch=3,
        grid=(n_tiles,),
        in_specs=[
            pl.BlockSpec(memory_space=pl.ANY),
            pl.BlockSpec((1, D, 2 * D_EXPERT), lambda i, te, nu, src: (te[i], 0, 0)),
            pl.BlockSpec((1, D_EXPERT, D), lambda i, te, nu, src: (te[i], 0, 0)),
        ],
        out_specs=pl.BlockSpec((tm, D), lambda i, te, nu, src: (i, 0)),
        scratch_shapes=[pltpu.VMEM((2, tm, D + LANES), F32), pltpu.SemaphoreType.DMA((2,))],
    )
    return pl.pallas_call(
        _moe_kernel,
        out_shape=jax.ShapeDtypeStruct((n_tiles * tm, D), F32),
        grid_spec=grid_spec,
        compiler_params=_cparams(("arbitrary",)),
        name="moe",
    )(tile_expert, n_used, chunk_src, x_local, w_gu, w_d)


def _combine_kernel(src_ref, y_hbm, x1_ref, ri_ref, mod_ref, g_ref, o_ref, ybuf, sem):
    tm = x1_ref.shape[1]
    n_chunks = LOCAL_ROWS // CHUNK
    n_j = pl.num_programs(1)
    step = pl.program_id(0) * n_j + pl.program_id(1)
    n_steps = pl.num_programs(0) * n_j
    slot = step % 2

    @pl.when(step == 0)
    def _():
        _start_chunk_gather(src_ref, 0, n_chunks, y_hbm, ybuf.at[0], sem.at[0])

    @pl.when(step + 1 < n_steps)
    def _():
        _start_chunk_gather(src_ref, (step + 1) * n_chunks, n_chunks, y_hbm, ybuf.at[1 - slot], sem.at[1 - slot])

    _wait_chunk_gather(n_chunks, y_hbm, ybuf.at[slot], sem.at[slot])
    yl = ybuf[slot]
    y_hi = yl.astype(BF16)
    y_lo = (yl - y_hi.astype(F32)).astype(BF16)
    col = lax.broadcasted_iota(jnp.int32, (tm, LOCAL_ROWS), 1)
    pick = jnp.where((col == ri_ref[0, :, 2:3]) | (col == ri_ref[0, :, 3:4]), 1.0, 0.0).astype(BF16)
    y = jnp.dot(pick, y_hi, preferred_element_type=F32) + jnp.dot(pick, y_lo, preferred_element_type=F32)
    gt_f = mod_ref[0, 5:6, :]
    o_ref[0] = x1_ref[0] + gt_f * _rms(y, g_ref[...])


def _combine(chunk_src, y_sorted, x1, route_i, mod3, g_post):
    B, S, D = x1.shape
    tm = TM_IN
    grid_spec = pltpu.PrefetchScalarGridSpec(
        num_scalar_prefetch=1,
        grid=(B, S // tm),
        in_specs=[
            pl.BlockSpec(memory_space=pl.ANY),
            pl.BlockSpec((1, tm, D), lambda b, j, src: (b, j, 0)),
            pl.BlockSpec((1, tm, LANES), lambda b, j, src: (b, j, 0)),
            pl.BlockSpec((1, 6, D), lambda b, j, src: (b, 0, 0)),
            pl.BlockSpec((1, D), lambda b, j, src: (0, 0)),
        ],
        out_specs=pl.BlockSpec((1, tm, D), lambda b, j, src: (b, j, 0)),
        scratch_shapes=[pltpu.VMEM((2, LOCAL_ROWS, D), F32), pltpu.SemaphoreType.DMA((2,))],
    )
    return pl.pallas_call(
        _combine_kernel,
        out_shape=jax.ShapeDtypeStruct((B, S, D), F32),
        grid_spec=grid_spec,
        compiler_params=_cparams(("arbitrary", "arbitrary")),
        name="combine",
    )(chunk_src, y_sorted, x1, route_i, mod3, g_post)


def _rope_angles(positions):
    def cos_sin(dim):
        half = dim // 2
        inv_freq = jnp.exp(-math.log(ROPE_THETA) * jnp.arange(half, dtype=F32) * (2.0 / dim))
        ang = positions.astype(F32)[..., None] * inv_freq
        return [jnp.cos(ang), jnp.sin(ang)]
    B, S = positions.shape
    parts = cos_sin(MOBA_ROT_DIMS) + cos_sin(MLA_ROPE_DIM) + [jnp.ones((B, S, 1), F32)]
    used = MOBA_ROT_DIMS + MLA_ROPE_DIM + 1
    return jnp.concatenate(parts + [jnp.zeros((B, S, LANES - used), F32)], axis=-1)


def _routing_metadata(counts, tm):
    i32 = jnp.int32
    n_tt = counts.shape[0]
    n_tiles = (2 * n_tt * TM_IN + n_tt * N_EXPERTS * (CHUNK - 1)) // tm + N_EXPERTS + 1
    seg = (counts + CHUNK - 1) // CHUNK * CHUNK
    loc_start = jnp.cumsum(seg, axis=1) - seg
    loc_total = jnp.sum(seg, axis=1)
    run = jnp.cumsum(seg, axis=0)
    rows_e = run[-1]
    tiles_per = (rows_e + tm - 1) // tm
    tile_end = jnp.cumsum(tiles_per)
    n_used = tile_end[-1]
    first_row = (tile_end - tiles_per) * tm
    tile_ids = jnp.arange(n_tiles, dtype=i32)
    te = jnp.minimum(jnp.sum((tile_ids[:, None] >= tile_end[None, :]).astype(i32), axis=1), N_EXPERTS - 1)
    last_e = jnp.sum(jnp.where(tile_ids == n_used - 1, te, 0))
    tile_expert = jnp.where(tile_ids < n_used, te, last_e).astype(i32)

    cpt = tm // CHUNK
    is_e = (tile_expert[:, None] == jnp.arange(N_EXPERTS, dtype=i32)[None, :]).astype(i32)
    of_tile = lambda a: jnp.sum(is_e[:, None, :] * a[None, :, :], axis=2)
    run_t, seg_t, loc_t = of_tile(run), of_tile(seg), of_tile(loc_start)
    first_t = jnp.sum(is_e * first_row[None, :], axis=1)
    rows_t = jnp.sum(is_e * rows_e[None, :], axis=1)
    off = (tile_ids * tm - first_t)[:, None] + jnp.arange(cpt, dtype=i32)[None, :] * CHUNK
    tau = jnp.minimum(jnp.sum((off[:, :, None] >= run_t[:, None, :]).astype(i32), axis=2), n_tt - 1)
    is_tau = (tau[:, :, None] == jnp.arange(n_tt, dtype=i32)[None, None, :]).astype(i32)
    at_tau = lambda a_t: jnp.sum(is_tau * a_t[:, None, :], axis=2)
    src = tau * LOCAL_ROWS + at_tau(loc_t) + off - (at_tau(run_t) - at_tau(seg_t))
    real = (tile_ids < n_used)[:, None] & (off < rows_t[:, None])
    zero_local = LOCAL_ROWS - CHUNK
    moe_src = jnp.where(real, src, zero_local).astype(i32).reshape(-1)

    lrow = jnp.arange(LOCAL_ROWS // CHUNK, dtype=i32) * CHUNK
    loc_end = loc_start + seg
    e_v = jnp.minimum(jnp.sum((lrow[None, :, None] >= loc_end[:, None, :]).astype(i32), axis=2), N_EXPERTS - 1)
    is_ev = (e_v[:, :, None] == jnp.arange(N_EXPERTS, dtype=i32)[None, None, :]).astype(i32)
    seg_first = first_row[None, :] + run - seg
    src_v = jnp.sum(is_ev * (seg_first - loc_start)[:, None, :], axis=2) + lrow[None, :]
    zero_sorted = (n_tiles - 1) * tm
    cmb_src = jnp.where(lrow[None, :] < loc_total[:, None], src_v, zero_sorted).astype(i32).reshape(-1)
    return tile_expert, n_used.reshape(1).astype(i32), moe_src, cmb_src


def kernel(x, c, positions, w_ada, b_ada, g_pre_attn, g_post_attn, g_pre_ffn, g_post_ffn, w_in, g_cq, w_uq,
           g_ckv, w_ukv, g_out_moba, g_out_mla, w_o, w_group, b_group, w_expert, b_expert, w_e_gate, w_e_up,
           w_e_down):
    B, S, D = x.shape
    depth = w_ada.shape[0]
    rope_cs = _rope_angles(positions)
    rope_exp = jnp.asarray(_rope_expansion(), BF16)
    row = lambda v: v.reshape(1, -1)

    for l in range(depth):
        n_main = 3 * MOBA_WIDTH + MLA_Q_RANK + MLA_KV_RANK
        w_in_p = jnp.concatenate([w_in[l][:, :n_main], jnp.tile(w_in[l][:, n_main:], (1, LANES // MLA_ROPE_DIM))],
                                 axis=1).astype(BF16)
        wq = w_uq[l].reshape(MLA_Q_RANK, MLA_HEADS, MLA_NOPE_DIM + MLA_ROPE_DIM)
        wq_n = wq[:, :, :MLA_NOPE_DIM]
        zero_n = jnp.zeros_like(wq_n)
        even = (jnp.arange(MLA_HEADS) % 2 == 0)[None, :, None]
        w_uq_n = jnp.concatenate([jnp.where(even, wq_n, zero_n), jnp.where(even, zero_n, wq_n)], axis=-1)
        w_uq_n = w_uq_n.reshape(MLA_Q_RANK, MLA_HEADS * LANES).astype(BF16)
        w_uq_p = wq[:, :, MLA_NOPE_DIM:].reshape(MLA_Q_RANK, MLA_HEADS * MLA_ROPE_DIM).astype(BF16)
        wkv = w_ukv[l].reshape(MLA_KV_RANK, MLA_HEADS, MLA_NOPE_DIM + MLA_V_DIM)
        w_ukv_p = jnp.concatenate([wkv[:, :, :MLA_NOPE_DIM].reshape(MLA_KV_RANK, -1),
                                   wkv[:, :, MLA_NOPE_DIM:].reshape(MLA_KV_RANK, -1)], axis=1).astype(BF16)
        w_r = jnp.concatenate([w_expert[l], w_group[l],
                               jnp.zeros((D, LANES - N_EXPERTS - N_GROUPS), F32)], axis=1)
        w_r_hi = w_r.astype(BF16)
        w_r = jnp.stack([w_r_hi, (w_r - w_r_hi.astype(F32)).astype(BF16)])
        b_r = jnp.concatenate([b_expert[l], b_group[l], jnp.zeros((LANES - N_EXPERTS - N_GROUPS,), F32)]).reshape(1, LANES)
        w_gu = jnp.concatenate([w_e_gate[l], w_e_up[l]], axis=-1).astype(BF16)
        w_d = w_e_down[l].astype(BF16)

        mod3 = _adaln(c, w_ada[l], b_ada[l]).reshape(B, 6, D)

        qa, ka, vat, kmean, qcat, kcat, vmt = _inproj(
            x, mod3, row(g_pre_attn[l]), w_in_p, rope_cs, rope_exp, row(g_cq[l]), w_uq_n, w_uq_p,
            row(g_ckv[l]), w_ukv_p)
        o_moba = _moba(qa, ka, vat, kmean)
        o_mla = _mla(qcat, kcat, vmt)
        x1, x_local, counts, route_i = _postattn(
            x, o_moba, o_mla, mod3, row(g_out_moba[l]), row(g_out_mla[l]), w_o[l].astype(BF16),
            row(g_post_attn[l]), row(g_pre_ffn[l]), w_r, b_r)

        counts = counts.reshape(-1, LANES)[:, :N_EXPERTS].astype(jnp.int32)
        tile_expert, n_used, moe_src, cmb_src = _routing_metadata(counts, TM_MOE)
        y_sorted = _moe(tile_expert, n_used, moe_src, x_local.reshape(-1, D + LANES), w_gu, w_d)
        x = _combine(cmb_src, y_sorted, x1, route_i, mod3, row(g_post_ffn[l]))
    return x
```

```python
import functools
import math

import jax
import jax.numpy as jnp
import numpy as np
from jax import lax
from jax.experimental import pallas as pl
from jax.experimental.pallas import tpu as pltpu

F32 = jnp.float32
BF16 = jnp.bfloat16

D_MODEL = 1024
MOBA_HEADS = 8
MOBA_HEAD_DIM = 64
MOBA_WIDTH = MOBA_HEADS * MOBA_HEAD_DIM
MOBA_BLOCK = 256
MOBA_TOPK = 3
MOBA_ROT_DIMS = MOBA_HEAD_DIM // 4
MLA_HEADS = 8
MLA_Q_RANK = 256
MLA_KV_RANK = 128
MLA_NOPE_DIM = 64
MLA_ROPE_DIM = 32
MLA_V_DIM = 64
MLA_WIDTH = MLA_HEADS * MLA_V_DIM
ROPE_THETA = 500000.0
N_GROUPS = 4
EXPERTS_PER_GROUP = 8
N_EXPERTS = N_GROUPS * EXPERTS_PER_GROUP
D_EXPERT = 256
EPS = 1e-6
NEG = -1e30
LOG2E = math.log2(math.e)

LANES = 128
HEADS_PER_STEP = LANES // MOBA_HEAD_DIM
N_PAIRS = MOBA_HEADS // HEADS_PER_STEP
VMEM_LIMIT = 56 * 1024 * 1024

TM_IN = 512
KV_GROUP = 4
TM_MOE = 256
CHUNK = 8
LOCAL_ROWS = 2 * TM_IN + N_EXPERTS * CHUNK


def _cparams(sem):
    return pltpu.CompilerParams(dimension_semantics=sem, vmem_limit_bytes=VMEM_LIMIT)


def _rms(x, g):
    return x * lax.rsqrt(jnp.mean(x * x, axis=-1, keepdims=True) + EPS) * g


def _adaln_kernel(c_ref, w_ref, b_ref, o_ref):
    c = c_ref[...]
    ca = c * jax.nn.sigmoid(c)
    o_ref[...] = jnp.dot(ca, w_ref[...], preferred_element_type=F32,
                         precision=lax.Precision.HIGHEST) + b_ref[...]


def _adaln(c, w, b):
    B, D = c.shape
    N = w.shape[1]
    tn = 1536 if N % 1536 == 0 else N
    return pl.pallas_call(
        _adaln_kernel,
        out_shape=jax.ShapeDtypeStruct((B, N), F32),
        grid=(N // tn,),
        in_specs=[pl.BlockSpec((B, D), lambda j: (0, 0)),
                  pl.BlockSpec((D, tn), lambda j: (0, j)),
                  pl.BlockSpec((1, tn), lambda j: (0, j))],
        out_specs=pl.BlockSpec((B, tn), lambda j: (0, j)),
        compiler_params=_cparams(("arbitrary",)),
        name="adaln",
    )(c, w, b.reshape(1, N))


def _rope128(xc, tab, shift):
    c = tab[:, 0:LANES]
    s1 = tab[:, LANES:2 * LANES]
    s2 = tab[:, 2 * LANES:3 * LANES]
    return xc * c + pltpu.roll(xc, LANES - shift, axis=1) * s1 + pltpu.roll(xc, shift, axis=1) * s2


def _rope_expansion():
    e = np.zeros((LANES, 6 * LANES), np.float32)
    ha, hm = MOBA_ROT_DIMS // 2, MLA_ROPE_DIM // 2
    cos_a, sin_a, cos_m, sin_m, one = 0, ha, 2 * ha, 2 * ha + hm, 2 * ha + 2 * hm
    for lane in range(LANES):
        d = lane % MOBA_HEAD_DIM
        if d < ha:
            e[cos_a + d, lane] = 1.0
            e[sin_a + d, LANES + lane] = -1.0
        elif d < 2 * ha:
            e[cos_a + d - ha, lane] = 1.0
            e[sin_a + d - ha, 2 * LANES + lane] = 1.0
        else:
            e[one, lane] = 1.0
        d = lane % MLA_ROPE_DIM
        e[cos_m + d % hm, 3 * LANES + lane] = 1.0
        if d < hm:
            e[sin_m + d, 4 * LANES + lane] = -1.0
        else:
            e[sin_m + d - hm, 5 * LANES + lane] = 1.0
    return e


def _expand_exact(v, e_ref):
    hi = v.astype(BF16)
    r1 = v - hi.astype(F32)
    mid = r1.astype(BF16)
    lo = (r1 - mid.astype(F32)).astype(BF16)
    e = e_ref[...]
    return (jnp.dot(hi, e, preferred_element_type=F32) + jnp.dot(mid, e, preferred_element_type=F32)
            + jnp.dot(lo, e, preferred_element_type=F32))


def _store_vt(src, col0, vt_ref):
    for p in range(N_PAIRS):
        for r in range(src.shape[0] // MOBA_BLOCK):
            blk = src[r * MOBA_BLOCK:(r + 1) * MOBA_BLOCK, col0 + p * LANES:col0 + (p + 1) * LANES]
            vt_ref[0, p, r] = blk.T.astype(BF16)


def _inproj_kernel(x_ref, mod_ref, gpre_ref, win_ref, cs_ref, rexp_ref, gcq_ref, wuqn_ref, wuqp_ref,
                   gckv_ref, wukv_ref,
                   qa_ref, ka_ref, vat_ref, kmean_ref, qcat_ref, kcat_ref, vmt_ref):
    tm = x_ref.shape[1]
    rope_tab = _expand_exact(cs_ref[0], rexp_ref)
    tab_a = rope_tab[:, 0:3 * LANES]
    tab_m = rope_tab[:, 3 * LANES:6 * LANES]
    x = x_ref[0]
    sh = mod_ref[0, 0:1, :]
    sc = mod_ref[0, 1:2, :]
    h = _rms(x, gpre_ref[...]) * (1.0 + sc) + sh
    z = jnp.dot(h.astype(BF16), win_ref[...], preferred_element_type=F32)

    moba_scale = MOBA_HEAD_DIM ** -0.5 * LOG2E
    half_a = MOBA_ROT_DIMS // 2
    n_chunks = MOBA_WIDTH // LANES
    for cidx in range(n_chunks):
        lo = cidx * LANES
        q = _rope128(z[:, lo:lo + LANES], tab_a, half_a)
        qa_ref[0, :, lo:lo + LANES] = (q * moba_scale).astype(BF16)
        k = _rope128(z[:, MOBA_WIDTH + lo:MOBA_WIDTH + lo + LANES], tab_a, half_a)
        ka_ref[0, :, lo:lo + LANES] = k.astype(BF16)
        for r in range(tm // MOBA_BLOCK):
            kmean_ref[0, r, :, lo:lo + LANES] = jnp.mean(
                k[r * MOBA_BLOCK:(r + 1) * MOBA_BLOCK], axis=0, keepdims=True)
    _store_vt(z, 2 * MOBA_WIDTH, vat_ref)

    o_cq = 3 * MOBA_WIDTH
    o_ckv = o_cq + MLA_Q_RANK
    o_kpe = o_ckv + MLA_KV_RANK
    mla_scale = (MLA_NOPE_DIM + MLA_ROPE_DIM) ** -0.5 * LOG2E
    half_m = MLA_ROPE_DIM // 2

    cqn = _rms(z[:, o_cq:o_cq + MLA_Q_RANK], gcq_ref[...]).astype(BF16)
    qn = jnp.dot(cqn, wuqn_ref[...], preferred_element_type=F32) * mla_scale
    qp = jnp.dot(cqn, wuqp_ref[...], preferred_element_type=F32)
    lane = lax.broadcasted_iota(jnp.int32, (tm, LANES), 1)
    heads_per_chunk = LANES // MLA_ROPE_DIM
    qp_chunks = [_rope128(qp[:, g * LANES:(g + 1) * LANES], tab_m, half_m) * mla_scale
                 for g in range(MLA_HEADS // heads_per_chunk)]
    for hd in range(MLA_HEADS):
        qcat_ref[0, hd, :, 0:LANES] = qn[:, hd * LANES:(hd + 1) * LANES].astype(BF16)
        sub = hd % heads_per_chunk
        keep = (lane >= sub * MLA_ROPE_DIM) & (lane < (sub + 1) * MLA_ROPE_DIM)
        qcat_ref[0, hd, :, LANES:2 * LANES] = jnp.where(keep, qp_chunks[hd // heads_per_chunk], 0.0).astype(BF16)

    ckvn = _rms(z[:, o_ckv:o_ckv + MLA_KV_RANK], gckv_ref[...]).astype(BF16)
    kv = jnp.dot(ckvn, wukv_ref[...], preferred_element_type=F32)
    kpe = _rope128(z[:, o_kpe:o_kpe + LANES], tab_m, half_m).astype(BF16)
    for p in range(N_PAIRS):
        kcat_ref[0, p, :, 0:LANES] = kv[:, p * LANES:(p + 1) * LANES].astype(BF16)
        kcat_ref[0, p, :, LANES:2 * LANES] = kpe
    _store_vt(kv, MLA_WIDTH, vmt_ref)


def _inproj(x, mod3, g_pre, w_in_p, rope_cs, rope_exp, g_cq, w_uq_n, w_uq_p, g_ckv, w_ukv_p):
    B, S, D = x.shape
    tm = min(TM_IN, S)
    nb = S // MOBA_BLOCK
    rb = tm // MOBA_BLOCK
    const2 = lambda b, i: (0, 0)
    tok3 = lambda b, i: (b, i, 0)
    outs = pl.pallas_call(
        _inproj_kernel,
        out_shape=[
            jax.ShapeDtypeStruct((B, S, MOBA_WIDTH), BF16),
            jax.ShapeDtypeStruct((B, S, MOBA_WIDTH), BF16),
            jax.ShapeDtypeStruct((B, N_PAIRS, nb, LANES, MOBA_BLOCK), BF16),
            jax.ShapeDtypeStruct((B, nb, 1, MOBA_WIDTH), F32),
            jax.ShapeDtypeStruct((B, MLA_HEADS, S, 2 * LANES), BF16),
            jax.ShapeDtypeStruct((B, N_PAIRS, S, 2 * LANES), BF16),
            jax.ShapeDtypeStruct((B, N_PAIRS, nb, LANES, MOBA_BLOCK), BF16),
        ],
        grid=(B, S // tm),
        in_specs=[
            pl.BlockSpec((1, tm, D), tok3),
            pl.BlockSpec((1, 6, D), lambda b, i: (b, 0, 0)),
            pl.BlockSpec((1, D), const2),
            pl.BlockSpec(w_in_p.shape, const2),
            pl.BlockSpec((1, tm, LANES), tok3),
            pl.BlockSpec(rope_exp.shape, const2),
            pl.BlockSpec((1, MLA_Q_RANK), const2),
            pl.BlockSpec(w_uq_n.shape, const2),
            pl.BlockSpec(w_uq_p.shape, const2),
            pl.BlockSpec((1, MLA_KV_RANK), const2),
            pl.BlockSpec(w_ukv_p.shape, const2),
        ],
        out_specs=[
            pl.BlockSpec((1, tm, MOBA_WIDTH), tok3),
            pl.BlockSpec((1, tm, MOBA_WIDTH), tok3),
            pl.BlockSpec((1, N_PAIRS, rb, LANES, MOBA_BLOCK), lambda b, i: (b, 0, i, 0, 0)),
            pl.BlockSpec((1, rb, 1, MOBA_WIDTH), lambda b, i: (b, i, 0, 0)),
            pl.BlockSpec((1, MLA_HEADS, tm, 2 * LANES), lambda b, i: (b, 0, i, 0)),
            pl.BlockSpec((1, N_PAIRS, tm, 2 * LANES), lambda b, i: (b, 0, i, 0)),
            pl.BlockSpec((1, N_PAIRS, rb, LANES, MOBA_BLOCK), lambda b, i: (b, 0, i, 0, 0)),
        ],
        compiler_params=_cparams(("arbitrary", "arbitrary")),
        name="inproj",
    )(x, mod3, g_pre, w_in_p, rope_cs, rope_exp, g_cq, w_uq_n, w_uq_p, g_ckv, w_ukv_p)
    return outs


def _nt_dot(a, b, **kw):
    return lax.dot_general(a, b, (((1,), (1,)), ((), ())), preferred_element_type=F32, **kw)


def _group_blocks(i, t):
    out = []
    for g in range(KV_GROUP):
        j = i - KV_GROUP * t - (KV_GROUP - 1 - g)
        out.append((jnp.maximum(j, 0), j >= 0))
    return out


def _stage_tables(nb):
    qi, grp = [], []
    for i in range(nb):
        for t in range((i + KV_GROUP) // KV_GROUP):
            qi.append(i)
            grp.append(t)
    n = len(qi)
    last = [1 if (s + 1 == n or qi[s + 1] != qi[s]) else 0 for s in range(n)]
    qi.append(qi[-1])
    grp.append(grp[-1])
    return np.asarray(qi, np.int32), np.asarray(grp, np.int32), np.asarray(last, np.int32), n


def _flash_sweep(tables, n_stages, q_tile, k_tile, vt_tile, tile_bias, write_out, scratch):
    qi_ref, grp_ref, last_ref = tables
    st_a, cm_a, st_b, cm_b, m_ref, l_ref, acc_ref = scratch
    L = MOBA_BLOCK
    key_minus_qry = (lax.broadcasted_iota(jnp.int32, (L, L), 0) - lax.broadcasted_iota(jnp.int32, (L, L), 1))

    def score_tile(i, blocks, g, hd, st_buf, cm_buf):
        j, _ = blocks[g]
        st = _nt_dot(k_tile(j), q_tile(i, hd))
        if g == KV_GROUP - 1:
            st = jnp.where(key_minus_qry <= (i - j) * L, st, NEG)
        st_buf[g, hd] = st
        cm_buf[g, hd] = jnp.max(st, axis=0, keepdims=True)

    def stage(s, cur, nxt):
        i, t = qi_ref[s], grp_ref[s]
        i_nxt = qi_ref[s + 1]
        blocks = _group_blocks(i, t)
        nxt_blocks = _group_blocks(i_nxt, grp_ref[s + 1])

        @pl.when(t == 0)
        def _():
            m_ref[...] = jnp.full(m_ref.shape, -jnp.inf, F32)
            l_ref[...] = jnp.zeros(l_ref.shape, F32)
            acc_ref[...] = jnp.zeros(acc_ref.shape, F32)

        for hd in range(HEADS_PER_STEP):
            biases = [tile_bias(i, hd, j, real) for j, real in blocks]
            m_old = m_ref[hd]
            m_new = m_old
            for g in range(KV_GROUP):
                m_new = jnp.maximum(m_new, cur[1][g, hd] + biases[g])
            alpha = jnp.exp2(m_old - m_new)
            l_new = alpha * l_ref[hd]
            pv = None
            for g, (j, _) in enumerate(blocks):
                score_tile(i_nxt, nxt_blocks, g, hd, *nxt)
                p = jnp.exp2(cur[0][g, hd] - (m_new - biases[g]))
                l_new = l_new + jnp.sum(p, axis=0, keepdims=True)
                d = jnp.dot(vt_tile(j, hd), p.astype(BF16), preferred_element_type=F32)
                pv = d if pv is None else pv + d
            m_ref[hd] = m_new
            l_ref[hd] = l_new
            acc_ref[hd] = alpha * acc_ref[hd] + pv

        @pl.when(last_ref[s] == 1)
        def _():
            ot = jnp.concatenate([acc_ref[hd] / l_ref[hd] for hd in range(HEADS_PER_STEP)], axis=0)
            write_out(i, ot.T)

    buf_a, buf_b = (st_a, cm_a), (st_b, cm_b)
    first = _group_blocks(qi_ref[0], grp_ref[0])
    for g in range(KV_GROUP):
        for hd in range(HEADS_PER_STEP):
            score_tile(qi_ref[0], first, g, hd, *buf_a)

    def body(ss, carry):
        stage(2 * ss, buf_a, buf_b)
        stage(2 * ss + 1, buf_b, buf_a)
        return carry

    lax.fori_loop(0, n_stages // 2, body, 0)
    if n_stages % 2:
        stage(n_stages - 1, buf_a, buf_b)


def _attn_scratch(dv):
    L = MOBA_BLOCK
    st = pltpu.VMEM((KV_GROUP, HEADS_PER_STEP, L, L), F32)
    cm = pltpu.VMEM((KV_GROUP, HEADS_PER_STEP, 1, L), F32)
    vec = pltpu.VMEM((HEADS_PER_STEP, 1, L), F32)
    return [st, cm, st, cm, vec, vec, pltpu.VMEM((HEADS_PER_STEP, dv, L), F32)]


def _moba_kernel(qi_ref, grp_ref, last_ref, q_ref, k_ref, vt_ref, kmean_ref, o_ref, qh_ref, bias_ref, *scratch,
                 n_stages):
    L = MOBA_BLOCK
    nb = kmean_ref.shape[1]
    lane = lax.broadcasted_iota(jnp.int32, (L, LANES), 1)
    kmean = kmean_ref[0, :, 0, :]
    blk = lax.broadcasted_iota(jnp.int32, (nb, L), 0)

    def select_blocks(i, carry):
        rows = pl.ds(pl.multiple_of(i * L, L), L)
        q = q_ref[0, rows, :]
        past = blk < i
        for hd in range(HEADS_PER_STEP):
            own = (lane >= hd * MOBA_HEAD_DIM) & (lane < (hd + 1) * MOBA_HEAD_DIM)
            qh = jnp.where(own, q, jnp.zeros_like(q))
            qh_ref[hd, rows, :] = qh
            gate = _nt_dot(kmean, qh.astype(F32), precision=lax.Precision.HIGHEST)
            gate = jnp.where(past, gate, -jnp.inf)
            rank = jnp.zeros((nb, L), jnp.int32)
            for j in range(nb):
                gj = gate[j:j + 1, :]
                ahead = (gj > gate) | ((gj == gate) & (j < blk))
                rank = rank + ahead.astype(jnp.int32)
            bias_ref[hd, i] = jnp.where((blk == i) | (past & (rank < MOBA_TOPK)), 0.0, NEG)
        return carry

    lax.fori_loop(0, nb, select_blocks, 0)

    def write_out(i, o):
        o_ref[0, pl.ds(pl.multiple_of(i * L, L), L), :] = o

    _flash_sweep(
        (qi_ref, grp_ref, last_ref), n_stages,
        q_tile=lambda i, hd: qh_ref[hd, pl.ds(pl.multiple_of(i * L, L), L), :],
        k_tile=lambda j: k_ref[0, pl.ds(pl.multiple_of(j * L, L), L), :],
        vt_tile=lambda j, hd: vt_ref[0, 0, j, pl.ds(hd * MOBA_HEAD_DIM, MOBA_HEAD_DIM), :],
        tile_bias=lambda i, hd, j, real: jnp.where(real, bias_ref[hd, i, pl.ds(j, 1), :], NEG),
        write_out=write_out, scratch=scratch)


def _moba(qa, ka, vat, kmean):
    B, S, _ = qa.shape
    L = MOBA_BLOCK
    nb = S // L
    qi, grp, last, n_stages = _stage_tables(nb)
    pair3 = lambda b, p, *_: (b, 0, p)
    grid_spec = pltpu.PrefetchScalarGridSpec(
        num_scalar_prefetch=3,
        grid=(B, N_PAIRS),
        in_specs=[
            pl.BlockSpec((1, S, LANES), pair3),
            pl.BlockSpec((1, S, LANES), pair3),
            pl.BlockSpec((1, 1, nb, LANES, L), lambda b, p, *_: (b, p, 0, 0, 0)),
            pl.BlockSpec((1, nb, 1, LANES), lambda b, p, *_: (b, 0, 0, p)),
        ],
        out_specs=pl.BlockSpec((1, S, LANES), pair3),
        scratch_shapes=[
            pltpu.VMEM((HEADS_PER_STEP, S, LANES), BF16),
            pltpu.VMEM((HEADS_PER_STEP, nb, nb, L), F32),
        ] + _attn_scratch(MOBA_HEAD_DIM),
    )
    return pl.pallas_call(
        functools.partial(_moba_kernel, n_stages=n_stages),
        out_shape=jax.ShapeDtypeStruct((B, S, MOBA_WIDTH), F32),
        grid_spec=grid_spec,
        compiler_params=_cparams(("arbitrary", "arbitrary")),
        name="moba",
    )(jnp.asarray(qi), jnp.asarray(grp), jnp.asarray(last), qa, ka, vat, kmean)


def _mla_kernel(qi_ref, grp_ref, last_ref, q_ref, k_ref, vt_ref, o_ref, *scratch, n_stages):
    L = MOBA_BLOCK

    def write_out(i, o):
        o_ref[0, pl.ds(pl.multiple_of(i * L, L), L), :] = o

    _flash_sweep(
        (qi_ref, grp_ref, last_ref), n_stages,
        q_tile=lambda i, hd: q_ref[0, hd, pl.ds(pl.multiple_of(i * L, L), L), :],
        k_tile=lambda j: k_ref[0, 0, pl.ds(pl.multiple_of(j * L, L), L), :],
        vt_tile=lambda j, hd: vt_ref[0, 0, j, pl.ds(hd * MLA_V_DIM, MLA_V_DIM), :],
        tile_bias=lambda i, hd, j, real: jnp.where(real, 0.0, NEG),
        write_out=write_out, scratch=scratch)


def _mla(qcat, kcat, vmt):
    B, _, S, W = qcat.shape
    L = MOBA_BLOCK
    nb = S // L
    qi, grp, last, n_stages = _stage_tables(nb)
    grid_spec = pltpu.PrefetchScalarGridSpec(
        num_scalar_prefetch=3,
        grid=(B, N_PAIRS),
        in_specs=[
            pl.BlockSpec((1, HEADS_PER_STEP, S, W), lambda b, p, *_: (b, p, 0, 0)),
            pl.BlockSpec((1, 1, S, W), lambda b, p, *_: (b, p, 0, 0)),
            pl.BlockSpec((1, 1, nb, LANES, L), lambda b, p, *_: (b, p, 0, 0, 0)),
        ],
        out_specs=pl.BlockSpec((1, S, LANES), lambda b, p, *_: (b, 0, p)),
        scratch_shapes=_attn_scratch(MLA_V_DIM),
    )
    return pl.pallas_call(
        functools.partial(_mla_kernel, n_stages=n_stages),
        out_shape=jax.ShapeDtypeStruct((B, S, MLA_WIDTH), F32),
        grid_spec=grid_spec,
        compiler_params=_cparams(("arbitrary", "arbitrary")),
        name="mla",
    )(jnp.asarray(qi), jnp.asarray(grp), jnp.asarray(last), qcat, kcat, vmt)


def _postattn_kernel(x_ref, om_ref, ol_ref, mod_ref, gom_ref, gol_ref, wo_ref, gpost_ref, gpre_ref,
                     wr_ref, br_ref, x1_ref, xl_ref, cnt_ref, ri_ref):
    tm = x_ref.shape[1]
    gt_a = mod_ref[0, 2:3, :]
    sh_f = mod_ref[0, 3:4, :]
    sc_f = mod_ref[0, 4:5, :]
    a = jnp.concatenate([_rms(om_ref[0], gom_ref[...]), _rms(ol_ref[0], gol_ref[...])], axis=-1)
    o = jnp.dot(a.astype(BF16), wo_ref[...], preferred_element_type=F32)
    x1 = x_ref[0] + gt_a * _rms(o, gpost_ref[...])
    x1_ref[0] = x1
    h2 = _rms(x1, gpre_ref[...]) * (1.0 + sc_f) + sh_f

    h2_hi = h2.astype(BF16)
    h2_lo = (h2 - h2_hi.astype(F32)).astype(BF16)
    logit = (jnp.dot(h2_hi, wr_ref[0], preferred_element_type=F32)
             + jnp.dot(h2_lo, wr_ref[0], preferred_element_type=F32)
             + jnp.dot(h2_hi, wr_ref[1], preferred_element_type=F32))
    biased = logit + br_ref[...]
    lane = lax.broadcasted_iota(jnp.int32, (tm, LANES), 1)
    big = jnp.int32(LANES)

    def first_argmax(v):
        mx = jnp.max(v, axis=-1, keepdims=True)
        return jnp.min(jnp.where(v == mx, lane, big), axis=-1, keepdims=True)

    def pick(v, idx):
        return jnp.sum(jnp.where(lane == idx, v, 0.0), axis=-1, keepdims=True)

    gmask = (lane >= N_EXPERTS) & (lane < N_EXPERTS + N_GROUPS)
    gl = jnp.where(gmask, logit, -jnp.inf)
    pg = jnp.exp(gl - jnp.max(gl, axis=-1, keepdims=True))
    g_lane = first_argmax(jnp.where(gmask, biased, -jnp.inf))
    g_prob = pick(pg, g_lane) / jnp.sum(pg, axis=-1, keepdims=True)
    e_lo = (g_lane - N_EXPERTS) * EXPERTS_PER_GROUP
    emask = (lane >= e_lo) & (lane < e_lo + EXPERTS_PER_GROUP)
    el = jnp.where(emask, logit, -jnp.inf)
    pe = jnp.exp(el - jnp.max(el, axis=-1, keepdims=True))
    eb = jnp.where(emask, biased, -jnp.inf)
    i1 = first_argmax(eb)
    i2 = first_argmax(jnp.where(lane == i1, -jnp.inf, eb))
    p1 = pick(pe, i1)
    p2 = pick(pe, i2)
    denom = p1 + p2
    w1 = p1 / denom * g_prob
    w2 = p2 / denom * g_prob

    hit1 = lane == i1
    hit2 = lane == i2
    onehot = jnp.where(hit1 | hit2, 1.0, 0.0).astype(BF16)
    tok_r = lax.broadcasted_iota(jnp.int32, (tm, tm), 0)
    tok_c = lax.broadcasted_iota(jnp.int32, (tm, tm), 1)
    csum = jnp.dot(jnp.where(tok_c <= tok_r, 1.0, 0.0).astype(BF16), onehot,
                   preferred_element_type=F32)
    cnt = csum[tm - 1:tm, :]
    seg = jnp.floor((cnt + (CHUNK - 1.0)) * (1.0 / CHUNK)) * CHUNK
    ex_r = lax.broadcasted_iota(jnp.int32, (LANES, LANES), 0)
    ex_c = lax.broadcasted_iota(jnp.int32, (LANES, LANES), 1)
    seg_start = jnp.dot(jnp.broadcast_to(seg, (8, LANES)).astype(BF16),
                        jnp.where(ex_r < ex_c, 1.0, 0.0).astype(BF16),
                        preferred_element_type=F32)[0:1, :]
    place = seg_start + csum - 1.0
    slot1 = jnp.sum(jnp.where(hit1, place, 0.0), axis=-1, keepdims=True)
    slot2 = jnp.sum(jnp.where(hit2, place, 0.0), axis=-1, keepdims=True)
    ri_ref[0] = jnp.where(lane == 0, i1, jnp.where(lane == 1, i2, jnp.where(
        lane == 2, slot1.astype(jnp.int32), jnp.where(lane == 3, slot2.astype(jnp.int32), 0))))
    cnt_ref[0, 0] = cnt

    packed = jnp.where(lane == 0, slot1, jnp.where(lane == 1, slot2, jnp.where(
        lane == 2, w1, jnp.where(lane == 3, w2, 0.0))))
    packed_t = packed.T
    srow = lax.broadcasted_iota(jnp.int32, (LOCAL_ROWS, tm), 0).astype(F32)
    at1 = srow == packed_t[0:1, :]
    at2 = srow == packed_t[1:2, :]
    perm = jnp.where(at1 | at2, 1.0, 0.0).astype(BF16)
    xl_ref[0, :, 0:D_MODEL] = jnp.dot(perm, h2_hi, preferred_element_type=F32)
    gate = jnp.sum(jnp.where(at1, packed_t[2:3, :], 0.0) + jnp.where(at2, packed_t[3:4, :], 0.0),
                   axis=-1, keepdims=True)
    xl_ref[0, :, D_MODEL:D_MODEL + LANES] = jnp.broadcast_to(gate, (LOCAL_ROWS, LANES))


def _postattn(x, o_moba, o_mla, mod3, g_om, g_ol, w_o, g_post, g_pre, w_r, b_r):
    B, S, D = x.shape
    tm = TM_IN
    assert S % tm == 0 and D == D_MODEL
    const2 = lambda b, i: (0, 0)
    tok3 = lambda b, i: (b, i, 0)
    n_j = S // tm
    return pl.pallas_call(
        _postattn_kernel,
        out_shape=[
            jax.ShapeDtypeStruct((B, S, D), F32),
            jax.ShapeDtypeStruct((B * n_j, LOCAL_ROWS, D + LANES), F32),
            jax.ShapeDtypeStruct((B * n_j, 1, 1, LANES), F32),
            jax.ShapeDtypeStruct((B, S, LANES), jnp.int32),
        ],
        grid=(B, S // tm),
        in_specs=[
            pl.BlockSpec((1, tm, D), tok3),
            pl.BlockSpec((1, tm, MOBA_WIDTH), tok3),
            pl.BlockSpec((1, tm, MLA_WIDTH), tok3),
            pl.BlockSpec((1, 6, D), lambda b, i: (b, 0, 0)),
            pl.BlockSpec((1, MOBA_WIDTH), const2),
            pl.BlockSpec((1, MLA_WIDTH), const2),
            pl.BlockSpec(w_o.shape, const2),
            pl.BlockSpec((1, D), const2),
            pl.BlockSpec((1, D), const2),
            pl.BlockSpec(w_r.shape, lambda b, i: (0, 0, 0)),
            pl.BlockSpec((1, LANES), const2),
        ],
        out_specs=[
            pl.BlockSpec((1, tm, D), tok3),
            pl.BlockSpec((1, LOCAL_ROWS, D + LANES), lambda b, i: (b * n_j + i, 0, 0)),
            pl.BlockSpec((1, 1, 1, LANES), lambda b, i: (b * n_j + i, 0, 0, 0)),
            pl.BlockSpec((1, tm, LANES), tok3),
        ],
        compiler_params=_cparams(("arbitrary", "arbitrary")),
        name="postattn",
    )(x, o_moba, o_mla, mod3, g_om, g_ol, w_o, g_post, g_pre, w_r, b_r)


def _start_chunk_gather(idx_ref, base, n_chunks, src_hbm, dst_ref, sem):
    for k in range(n_chunks):
        row = pl.multiple_of(idx_ref[base + k], CHUNK)
        pltpu.make_async_copy(src_hbm.at[pl.ds(row, CHUNK), :], dst_ref.at[pl.ds(k * CHUNK, CHUNK), :], sem).start()


def _wait_chunk_gather(n_chunks, src_hbm, dst_ref, sem):
    pltpu.make_async_copy(src_hbm.at[pl.ds(0, n_chunks * CHUNK), :], dst_ref, sem).wait()


def _moe_kernel(te_ref, nu_ref, src_ref, xl_hbm, wgu_ref, wd_ref, y_ref, xbuf, sem):
    tm = y_ref.shape[0]
    n_chunks = tm // CHUNK
    i = pl.program_id(0)
    n_used = nu_ref[0]
    slot = i % 2

    @pl.when((i == 0) & (n_used > 0))
    def _():
        _start_chunk_gather(src_ref, 0, n_chunks, xl_hbm, xbuf.at[0], sem.at[0])

    @pl.when(i + 1 < n_used)
    def _():
        _start_chunk_gather(src_ref, (i + 1) * n_chunks, n_chunks, xl_hbm, xbuf.at[1 - slot], sem.at[1 - slot])

    @pl.when(i < n_used)
    def _():
        _wait_chunk_gather(n_chunks, xl_hbm, xbuf.at[slot], sem.at[slot])
        xb = xbuf[slot, :, 0:D_MODEL].astype(BF16)
        gate = xbuf[slot, :, D_MODEL:D_MODEL + LANES]
        gu = jnp.dot(xb, wgu_ref[0], preferred_element_type=F32)
        g = gu[:, :D_EXPERT]
        u = gu[:, D_EXPERT:]
        hid = g * jax.nn.sigmoid(g) * u
        y = jnp.dot(hid.astype(BF16), wd_ref[0], preferred_element_type=F32)
        y_ref[...] = jnp.concatenate([gate] * (D_MODEL // LANES), axis=1) * y

    @pl.when(i >= n_used)
    def _():
        y_ref[...] = jnp.zeros_like(y_ref)


def _moe(tile_expert, n_used, chunk_src, x_local, w_gu, w_d):
    D = D_MODEL
    n_tiles = tile_expert.shape[0]
    tm = TM_MOE
    grid_spec = pltpu.PrefetchScalarGridSpec(
        num_scalar_prefetch=3,
        grid=(n_tiles,),
        in_specs=[
            pl.BlockSpec(memory_space=pl.ANY),
            pl.BlockSpec((1, D, 2 * D_EXPERT), lambda i, te, nu, src: (te[i], 0, 0)),
            pl.BlockSpec((1, D_EXPERT, D), lambda i, te, nu, src: (te[i], 0, 0)),
        ],
        out_specs=pl.BlockSpec((tm, D), lambda i, te, nu, src: (i, 0)),
        scratch_shapes=[pltpu.VMEM((2, tm, D + LANES), F32), pltpu.SemaphoreType.DMA((2,))],
    )
    return pl.pallas_call(
        _moe_kernel,
        out_shape=jax.ShapeDtypeStruct((n_tiles * tm, D), F32),
        grid_spec=grid_spec,
        compiler_params=_cparams(("arbitrary",)),
        name="moe",
    )(tile_expert, n_used, chunk_src, x_local, w_gu, w_d)


def _combine_kernel(src_ref, y_hbm, x1_ref, ri_ref, mod_ref, g_ref, o_ref, ybuf, sem):
    tm = x1_ref.shape[1]
    n_chunks = LOCAL_ROWS // CHUNK
    n_j = pl.num_programs(1)
    step = pl.program_id(0) * n_j + pl.program_id(1)
    n_steps = pl.num_programs(0) * n_j
    slot = step % 2

    @pl.when(step == 0)
    def _():
        _start_chunk_gather(src_ref, 0, n_chunks, y_hbm, ybuf.at[0], sem.at[0])

    @pl.when(step + 1 < n_steps)
    def _():
        _start_chunk_gather(src_ref, (step + 1) * n_chunks, n_chunks, y_hbm, ybuf.at[1 - slot], sem.at[1 - slot])

    _wait_chunk_gather(n_chunks, y_hbm, ybuf.at[slot], sem.at[slot])
    yl = ybuf[slot]
    y_hi = yl.astype(BF16)
    y_lo = (yl - y_hi.astype(F32)).astype(BF16)
    col = lax.broadcasted_iota(jnp.int32, (tm, LOCAL_ROWS), 1)
    pick = jnp.where((col == ri_ref[0, :, 2:3]) | (col == ri_ref[0, :, 3:4]), 1.0, 0.0).astype(BF16)
    y = jnp.dot(pick, y_hi, preferred_element_type=F32) + jnp.dot(pick, y_lo, preferred_element_type=F32)
    gt_f = mod_ref[0, 5:6, :]
    o_ref[0] = x1_ref[0] + gt_f * _rms(y, g_ref[...])


def _combine(chunk_src, y_sorted, x1, route_i, mod3, g_post):
    B, S, D = x1.shape
    tm = TM_IN
    grid_spec = pltpu.PrefetchScalarGridSpec(
        num_scalar_prefetch=1,
        grid=(B, S // tm),
        in_specs=[
            pl.BlockSpec(memory_space=pl.ANY),
            pl.BlockSpec((1, tm, D), lambda b, j, src: (b, j, 0)),
            pl.BlockSpec((1, tm, LANES), lambda b, j, src: (b, j, 0)),
            pl.BlockSpec((1, 6, D), lambda b, j, src: (b, 0, 0)),
            pl.BlockSpec((1, D), lambda b, j, src: (0, 0)),
        ],
        out_specs=pl.BlockSpec((1, tm, D), lambda b, j, src: (b, j, 0)),
        scratch_shapes=[pltpu.VMEM((2, LOCAL_ROWS, D), F32), pltpu.SemaphoreType.DMA((2,))],
    )
    return pl.pallas_call(
        _combine_kernel,
        out_shape=jax.ShapeDtypeStruct((B, S, D), F32),
        grid_spec=grid_spec,
        compiler_params=_cparams(("arbitrary", "arbitrary")),
        name="combine",
    )(chunk_src, y_sorted, x1, route_i, mod3, g_post)


def _rope_angles(positions):
    def cos_sin(dim):
        half = dim // 2
        inv_freq = jnp.exp(-math.log(ROPE_THETA) * jnp.arange(half, dtype=F32) * (2.0 / dim))
        ang = positions.astype(F32)[..., None] * inv_freq
        return [jnp.cos(ang), jnp.sin(ang)]
    B, S = positions.shape
    parts = cos_sin(MOBA_ROT_DIMS) + cos_sin(MLA_ROPE_DIM) + [jnp.ones((B, S, 1), F32)]
    used = MOBA_ROT_DIMS + MLA_ROPE_DIM + 1
    return jnp.concatenate(parts + [jnp.zeros((B, S, LANES - used), F32)], axis=-1)


def _routing_metadata(counts, tm):
    i32 = jnp.int32
    n_tt = counts.shape[0]
    n_tiles = (2 * n_tt * TM_IN + n_tt * N_EXPERTS * (CHUNK - 1)) // tm + N_EXPERTS + 1
    seg = (counts + CHUNK - 1) // CHUNK * CHUNK
    loc_start = jnp.cumsum(seg, axis=1) - seg
    loc_total = jnp.sum(seg, axis=1)
    run = jnp.cumsum(seg, axis=0)
    rows_e = run[-1]
    tiles_per = (rows_e + tm - 1) // tm
    tile_end = jnp.cumsum(tiles_per)
    n_used = tile_end[-1]
    first_row = (tile_end - tiles_per) * tm
    tile_ids = jnp.arange(n_tiles, dtype=i32)
    te = jnp.minimum(jnp.sum((tile_ids[:, None] >= tile_end[None, :]).astype(i32), axis=1), N_EXPERTS - 1)
    last_e = jnp.sum(jnp.where(tile_ids == n_used - 1, te, 0))
    tile_expert = jnp.where(tile_ids < n_used, te, last_e).astype(i32)

    cpt = tm // CHUNK
    is_e = (tile_expert[:, None] == jnp.arange(N_EXPERTS, dtype=i32)[None, :]).astype(i32)
    of_tile = lambda a: jnp.sum(is_e[:, None, :] * a[None, :, :], axis=2)
    run_t, seg_t, loc_t = of_tile(run), of_tile(seg), of_tile(loc_start)
    first_t = jnp.sum(is_e * first_row[None, :], axis=1)
    rows_t = jnp.sum(is_e * rows_e[None, :], axis=1)
    off = (tile_ids * tm - first_t)[:, None] + jnp.arange(cpt, dtype=i32)[None, :] * CHUNK
    tau = jnp.minimum(jnp.sum((off[:, :, None] >= run_t[:, None, :]).astype(i32), axis=2), n_tt - 1)
    is_tau = (tau[:, :, None] == jnp.arange(n_tt, dtype=i32)[None, None, :]).astype(i32)
    at_tau = lambda a_t: jnp.sum(is_tau * a_t[:, None, :], axis=2)
    src = tau * LOCAL_ROWS + at_tau(loc_t) + off - (at_tau(run_t) - at_tau(seg_t))
    real = (tile_ids < n_used)[:, None] & (off < rows_t[:, None])
    zero_local = LOCAL_ROWS - CHUNK
    moe_src = jnp.where(real, src, zero_local).astype(i32).reshape(-1)

    lrow = jnp.arange(LOCAL_ROWS // CHUNK, dtype=i32) * CHUNK
    loc_end = loc_start + seg
    e_v = jnp.minimum(jnp.sum((lrow[None, :, None] >= loc_end[:, None, :]).astype(i32), axis=2), N_EXPERTS - 1)
    is_ev = (e_v[:, :, None] == jnp.arange(N_EXPERTS, dtype=i32)[None, None, :]).astype(i32)
    seg_first = first_row[None, :] + run - seg
    src_v = jnp.sum(is_ev * (seg_first - loc_start)[:, None, :], axis=2) + lrow[None, :]
    zero_sorted = (n_tiles - 1) * tm
    cmb_src = jnp.where(lrow[None, :] < loc_total[:, None], src_v, zero_sorted).astype(i32).reshape(-1)
    return tile_expert, n_used.reshape(1).astype(i32), moe_src, cmb_src


def kernel(x, c, positions, w_ada, b_ada, g_pre_attn, g_post_attn, g_pre_ffn, g_post_ffn, w_in, g_cq, w_uq,
           g_ckv, w_ukv, g_out_moba, g_out_mla, w_o, w_group, b_group, w_expert, b_expert, w_e_gate, w_e_up,
           w_e_down):
    B, S, D = x.shape
    depth = w_ada.shape[0]
    rope_cs = _rope_angles(positions)
    rope_exp = jnp.asarray(_rope_expansion(), BF16)
    row = lambda v: v.reshape(1, -1)

    for l in range(depth):
        n_main = 3 * MOBA_WIDTH + MLA_Q_RANK + MLA_KV_RANK
        w_in_p = jnp.concatenate([w_in[l][:, :n_main], jnp.tile(w_in[l][:, n_main:], (1, LANES // MLA_ROPE_DIM))],
                                 axis=1).astype(BF16)
        wq = w_uq[l].reshape(MLA_Q_RANK, MLA_HEADS, MLA_NOPE_DIM + MLA_ROPE_DIM)
        wq_n = wq[:, :, :MLA_NOPE_DIM]
        zero_n = jnp.zeros_like(wq_n)
        even = (jnp.arange(MLA_HEADS) % 2 == 0)[None, :, None]
        w_uq_n = jnp.concatenate([jnp.where(even, wq_n, zero_n), jnp.where(even, zero_n, wq_n)], axis=-1)
        w_uq_n = w_uq_n.reshape(MLA_Q_RANK, MLA_HEADS * LANES).astype(BF16)
        w_uq_p = wq[:, :, MLA_NOPE_DIM:].reshape(MLA_Q_RANK, MLA_HEADS * MLA_ROPE_DIM).astype(BF16)
        wkv = w_ukv[l].reshape(MLA_KV_RANK, MLA_HEADS, MLA_NOPE_DIM + MLA_V_DIM)
        w_ukv_p = jnp.concatenate([wkv[:, :, :MLA_NOPE_DIM].reshape(MLA_KV_RANK, -1),
                                   wkv[:, :, MLA_NOPE_DIM:].reshape(MLA_KV_RANK, -1)], axis=1).astype(BF16)
        w_r = jnp.concatenate([w_expert[l], w_group[l],
                               jnp.zeros((D, LANES - N_EXPERTS - N_GROUPS), F32)], axis=1)
        w_r_hi = w_r.astype(BF16)
        w_r = jnp.stack([w_r_hi, (w_r - w_r_hi.astype(F32)).astype(BF16)])
        b_r = jnp.concatenate([b_expert[l], b_group[l], jnp.zeros((LANES - N_EXPERTS - N_GROUPS,), F32)]).reshape(1, LANES)
        w_gu = jnp.concatenate([w_e_gate[l], w_e_up[l]], axis=-1).astype(BF16)
        w_d = w_e_down[l].astype(BF16)

        mod3 = _adaln(c, w_ada[l], b_ada[l]).reshape(B, 6, D)

        qa, ka, vat, kmean, qcat, kcat, vmt = _inproj(
            x, mod3, row(g_pre_attn[l]), w_in_p, rope_cs, rope_exp, row(g_cq[l]), w_uq_n, w_uq_p,
            row(g_ckv[l]), w_ukv_p)
        o_moba = _moba(qa, ka, vat, kmean)
        o_mla = _mla(qcat, kcat, vmt)
        x1, x_local, counts, route_i = _postattn(
            x, o_moba, o_mla, mod3, row(g_out_moba[l]), row(g_out_mla[l]), w_o[l].astype(BF16),
            row(g_post_attn[l]), row(g_pre_ffn[l]), w_r, b_r)

        counts = counts.reshape(-1, LANES)[:, :N_EXPERTS].astype(jnp.int32)
        tile_expert, n_used, moe_src, cmb_src = _routing_metadata(counts, TM_MOE)
        y_sorted = _moe(tile_expert, n_used, moe_src, x_local.reshape(-1, D + LANES), w_gu, w_d)
        x = _combine(cmb_src, y_sorted, x1, route_i, mod3, row(g_post_ffn[l]))
    return x
```

```python
import functools
import math

import jax
import jax.numpy as jnp
import numpy as np
from jax import lax
from jax.experimental import pallas as pl
from jax.experimental.pallas import tpu as pltpu

F32 = jnp.float32
BF16 = jnp.bfloat16

D_MODEL = 1024
MOBA_HEADS = 8
MOBA_HEAD_DIM = 64
MOBA_WIDTH = MOBA_HEADS * MOBA_HEAD_DIM
MOBA_BLOCK = 256
MOBA_TOPK = 3
MOBA_ROT_DIMS = MOBA_HEAD_DIM // 4
MLA_HEADS = 8
MLA_Q_RANK = 256
MLA_KV_RANK = 128
MLA_NOPE_DIM = 64
MLA_ROPE_DIM = 32
MLA_V_DIM = 64
MLA_WIDTH = MLA_HEADS * MLA_V_DIM
ROPE_THETA = 500000.0
N_GROUPS = 4
EXPERTS_PER_GROUP = 8
N_EXPERTS = N_GROUPS * EXPERTS_PER_GROUP
D_EXPERT = 256
EPS = 1e-6
NEG = -1e30
LOG2E = math.log2(math.e)

LANES = 128
HEADS_PER_STEP = LANES // MOBA_HEAD_DIM
N_PAIRS = MOBA_HEADS // HEADS_PER_STEP
VMEM_LIMIT = 56 * 1024 * 1024

TM_IN = 512
KV_GROUP = 4
STAGES_PER_TRIP = 4
TM_MOE = 256
CHUNK = 8
LOCAL_ROWS = 2 * TM_IN + N_EXPERTS * CHUNK


def _cparams(sem):
    return pltpu.CompilerParams(dimension_semantics=sem, vmem_limit_bytes=VMEM_LIMIT)


def _rms(x, g):
    return x * lax.rsqrt(jnp.mean(x * x, axis=-1, keepdims=True) + EPS) * g


def _adaln_kernel(c_ref, w_ref, b_ref, o_ref):
    c = c_ref[...]
    ca = c * jax.nn.sigmoid(c)
    o_ref[...] = jnp.dot(ca, w_ref[...], preferred_element_type=F32,
                         precision=lax.Precision.HIGHEST) + b_ref[...]


def _adaln(c, w, b):
    B, D = c.shape
    N = w.shape[1]
    tn = 1536 if N % 1536 == 0 else N
    return pl.pallas_call(
        _adaln_kernel,
        out_shape=jax.ShapeDtypeStruct((B, N), F32),
        grid=(N // tn,),
        in_specs=[pl.BlockSpec((B, D), lambda j: (0, 0)),
                  pl.BlockSpec((D, tn), lambda j: (0, j)),
                  pl.BlockSpec((1, tn), lambda j: (0, j))],
        out_specs=pl.BlockSpec((B, tn), lambda j: (0, j)),
        compiler_params=_cparams(("arbitrary",)),
        name="adaln",
    )(c, w, b.reshape(1, N))


def _rope128(xc, tab, shift):
    c = tab[:, 0:LANES]
    s1 = tab[:, LANES:2 * LANES]
    s2 = tab[:, 2 * LANES:3 * LANES]
    return xc * c + pltpu.roll(xc, LANES - shift, axis=1) * s1 + pltpu.roll(xc, shift, axis=1) * s2


def _rope_expansion():
    e = np.zeros((LANES, 6 * LANES), np.float32)
    ha, hm = MOBA_ROT_DIMS // 2, MLA_ROPE_DIM // 2
    cos_a, sin_a, cos_m, sin_m, one = 0, ha, 2 * ha, 2 * ha + hm, 2 * ha + 2 * hm
    for lane in range(LANES):
        d = lane % MOBA_HEAD_DIM
        if d < ha:
            e[cos_a + d, lane] = 1.0
            e[sin_a + d, LANES + lane] = -1.0
        elif d < 2 * ha:
            e[cos_a + d - ha, lane] = 1.0
            e[sin_a + d - ha, 2 * LANES + lane] = 1.0
        else:
            e[one, lane] = 1.0
        d = lane % MLA_ROPE_DIM
        e[cos_m + d % hm, 3 * LANES + lane] = 1.0
        if d < hm:
            e[sin_m + d, 4 * LANES + lane] = -1.0
        else:
            e[sin_m + d - hm, 5 * LANES + lane] = 1.0
    return e


def _expand_exact(v, e_ref):
    hi = v.astype(BF16)
    r1 = v - hi.astype(F32)
    mid = r1.astype(BF16)
    lo = (r1 - mid.astype(F32)).astype(BF16)
    e = e_ref[...]
    return (jnp.dot(hi, e, preferred_element_type=F32) + jnp.dot(mid, e, preferred_element_type=F32)
            + jnp.dot(lo, e, preferred_element_type=F32))


def _store_vt(src, col0, vt_ref):
    for p in range(N_PAIRS):
        for r in range(src.shape[0] // MOBA_BLOCK):
            blk = src[r * MOBA_BLOCK:(r + 1) * MOBA_BLOCK, col0 + p * LANES:col0 + (p + 1) * LANES]
            vt_ref[0, p, r] = blk.T.astype(BF16)


def _inproj_kernel(x_ref, mod_ref, gpre_ref, win_ref, cs_ref, rexp_ref, gcq_ref, wuqn_ref, wuqp_ref,
                   gckv_ref, wukv_ref,
                   qa_ref, ka_ref, vat_ref, kmean_ref, qcat_ref, kcat_ref, vmt_ref):
    tm = x_ref.shape[1]
    rope_tab = _expand_exact(cs_ref[0].T, rexp_ref)
    tab_a = rope_tab[:, 0:3 * LANES]
    tab_m = rope_tab[:, 3 * LANES:6 * LANES]
    x = x_ref[0]
    sh = mod_ref[0, 0:1, :]
    sc = mod_ref[0, 1:2, :]
    h = _rms(x, gpre_ref[...]) * (1.0 + sc) + sh
    z = jnp.dot(h.astype(BF16), win_ref[...], preferred_element_type=F32)

    moba_scale = MOBA_HEAD_DIM ** -0.5 * LOG2E
    half_a = MOBA_ROT_DIMS // 2
    n_chunks = MOBA_WIDTH // LANES
    for cidx in range(n_chunks):
        lo = cidx * LANES
        q = _rope128(z[:, lo:lo + LANES], tab_a, half_a)
        qa_ref[0, :, lo:lo + LANES] = (q * moba_scale).astype(BF16)
        k = _rope128(z[:, MOBA_WIDTH + lo:MOBA_WIDTH + lo + LANES], tab_a, half_a)
        ka_ref[0, :, lo:lo + LANES] = k.astype(BF16)
        for r in range(tm // MOBA_BLOCK):
            kmean_ref[0, r, :, lo:lo + LANES] = jnp.mean(
                k[r * MOBA_BLOCK:(r + 1) * MOBA_BLOCK], axis=0, keepdims=True)
    _store_vt(z, 2 * MOBA_WIDTH, vat_ref)

    o_cq = 3 * MOBA_WIDTH
    o_ckv = o_cq + MLA_Q_RANK
    o_kpe = o_ckv + MLA_KV_RANK
    mla_scale = (MLA_NOPE_DIM + MLA_ROPE_DIM) ** -0.5 * LOG2E
    half_m = MLA_ROPE_DIM // 2

    cqn = _rms(z[:, o_cq:o_cq + MLA_Q_RANK], gcq_ref[...]).astype(BF16)
    qn = jnp.dot(cqn, wuqn_ref[...], preferred_element_type=F32) * mla_scale
    qp = jnp.dot(cqn, wuqp_ref[...], preferred_element_type=F32)
    lane = lax.broadcasted_iota(jnp.int32, (tm, LANES), 1)
    heads_per_chunk = LANES // MLA_ROPE_DIM
    qp_chunks = [_rope128(qp[:, g * LANES:(g + 1) * LANES], tab_m, half_m) * mla_scale
                 for g in range(MLA_HEADS // heads_per_chunk)]
    for hd in range(MLA_HEADS):
        qcat_ref[0, hd, :, 0:LANES] = qn[:, hd * LANES:(hd + 1) * LANES].astype(BF16)
        sub = hd % heads_per_chunk
        keep = (lane >= sub * MLA_ROPE_DIM) & (lane < (sub + 1) * MLA_ROPE_DIM)
        qcat_ref[0, hd, :, LANES:2 * LANES] = jnp.where(keep, qp_chunks[hd // heads_per_chunk], 0.0).astype(BF16)

    ckvn = _rms(z[:, o_ckv:o_ckv + MLA_KV_RANK], gckv_ref[...]).astype(BF16)
    kv = jnp.dot(ckvn, wukv_ref[...], preferred_element_type=F32)
    kpe = _rope128(z[:, o_kpe:o_kpe + LANES], tab_m, half_m).astype(BF16)
    for p in range(N_PAIRS):
        kcat_ref[0, p, :, 0:LANES] = kv[:, p * LANES:(p + 1) * LANES].astype(BF16)
        kcat_ref[0, p, :, LANES:2 * LANES] = kpe
    _store_vt(kv, MLA_WIDTH, vmt_ref)


def _inproj(x, mod3, g_pre, w_in_p, rope_cs, rope_exp, g_cq, w_uq_n, w_uq_p, g_ckv, w_ukv_p):
    B, S, D = x.shape
    tm = min(TM_IN, S)
    nb = S // MOBA_BLOCK
    rb = tm // MOBA_BLOCK
    const2 = lambda b, i: (0, 0)
    tok3 = lambda b, i: (b, i, 0)
    outs = pl.pallas_call(
        _inproj_kernel,
        out_shape=[
            jax.ShapeDtypeStruct((B, S, MOBA_WIDTH), BF16),
            jax.ShapeDtypeStruct((B, S, MOBA_WIDTH), BF16),
            jax.ShapeDtypeStruct((B, N_PAIRS, nb, LANES, MOBA_BLOCK), BF16),
            jax.ShapeDtypeStruct((B, nb, 1, MOBA_WIDTH), F32),
            jax.ShapeDtypeStruct((B, MLA_HEADS, S, 2 * LANES), BF16),
            jax.ShapeDtypeStruct((B, N_PAIRS, S, 2 * LANES), BF16),
            jax.ShapeDtypeStruct((B, N_PAIRS, nb, LANES, MOBA_BLOCK), BF16),
        ],
        grid=(B, S // tm),
        in_specs=[
            pl.BlockSpec((1, tm, D), tok3),
            pl.BlockSpec((1, 6, D), lambda b, i: (b, 0, 0)),
            pl.BlockSpec((1, D), const2),
            pl.BlockSpec(w_in_p.shape, const2),
            pl.BlockSpec((1, LANES, tm), lambda b, i: (b, 0, i)),
            pl.BlockSpec(rope_exp.shape, const2),
            pl.BlockSpec((1, MLA_Q_RANK), const2),
            pl.BlockSpec(w_uq_n.shape, const2),
            pl.BlockSpec(w_uq_p.shape, const2),
            pl.BlockSpec((1, MLA_KV_RANK), const2),
            pl.BlockSpec(w_ukv_p.shape, const2),
        ],
        out_specs=[
            pl.BlockSpec((1, tm, MOBA_WIDTH), tok3),
            pl.BlockSpec((1, tm, MOBA_WIDTH), tok3),
            pl.BlockSpec((1, N_PAIRS, rb, LANES, MOBA_BLOCK), lambda b, i: (b, 0, i, 0, 0)),
            pl.BlockSpec((1, rb, 1, MOBA_WIDTH), lambda b, i: (b, i, 0, 0)),
            pl.BlockSpec((1, MLA_HEADS, tm, 2 * LANES), lambda b, i: (b, 0, i, 0)),
            pl.BlockSpec((1, N_PAIRS, tm, 2 * LANES), lambda b, i: (b, 0, i, 0)),
            pl.BlockSpec((1, N_PAIRS, rb, LANES, MOBA_BLOCK), lambda b, i: (b, 0, i, 0, 0)),
        ],
        compiler_params=_cparams(("arbitrary", "arbitrary")),
        name="inproj",
    )(x, mod3, g_pre, w_in_p, rope_cs, rope_exp, g_cq, w_uq_n, w_uq_p, g_ckv, w_ukv_p)
    return outs


def _nt_dot(a, b, **kw):
    return lax.dot_general(a, b, (((1,), (1,)), ((), ())), preferred_element_type=F32, **kw)


def _group_blocks(i, t):
    out = []
    for g in range(KV_GROUP):
        j = i - KV_GROUP * t - (KV_GROUP - 1 - g)
        out.append((jnp.maximum(j, 0), j >= 0))
    return out


def _stage_tables(nb):
    qi, grp = [], []
    for i in range(nb):
        for t in range((i + KV_GROUP) // KV_GROUP):
            qi.append(i)
            grp.append(t)
    n = len(qi)
    qi.append(qi[-1])
    grp.append(grp[-1])
    return np.asarray(qi, np.int32), np.asarray(grp, np.int32), n


def _flash_sweep(tables, n_stages, q_tile, k_tile, vt_tile, tile_bias, write_out, scratch):
    qi_ref, grp_ref = tables
    st_a, cm_a, st_b, cm_b, m_ref, l_ref, acc_ref, ot_ref = scratch
    L = MOBA_BLOCK
    key_minus_qry = (lax.broadcasted_iota(jnp.int32, (L, L), 0) - lax.broadcasted_iota(jnp.int32, (L, L), 1))

    def score_tile(i, blocks, g, hd, st_buf, cm_buf):
        j, _ = blocks[g]
        st = _nt_dot(k_tile(j), q_tile(i, hd))
        if g == KV_GROUP - 1:
            st = jnp.where(key_minus_qry <= (i - j) * L, st, NEG)
        st_buf[g, hd] = st
        cm_buf[g, hd] = jnp.max(st, axis=0, keepdims=True)

    def stage(s, cur, nxt):
        i, t = qi_ref[s], grp_ref[s]
        i_nxt = qi_ref[s + 1]
        blocks = _group_blocks(i, t)
        nxt_blocks = _group_blocks(i_nxt, grp_ref[s + 1])

        fresh = t == 0
        for hd in range(HEADS_PER_STEP):
            biases = [tile_bias(i, hd, j, real) for j, real in blocks]
            m_old = jnp.where(fresh, -jnp.inf, m_ref[hd])
            m_new = m_old
            for g in range(KV_GROUP):
                m_new = jnp.maximum(m_new, cur[1][g, hd] + biases[g])
            alpha = jnp.exp2(m_old - m_new)
            l_new = alpha * jnp.where(fresh, 0.0, l_ref[hd])
            pv = None
            for g, (j, _) in enumerate(blocks):
                score_tile(i_nxt, nxt_blocks, g, hd, *nxt)
                p = jnp.exp2(cur[0][g, hd] - (m_new - biases[g]))
                l_new = l_new + jnp.sum(p, axis=0, keepdims=True)
                d = jnp.dot(vt_tile(j, hd), p.astype(BF16), preferred_element_type=F32)
                pv = d if pv is None else pv + d
            acc_new = alpha * jnp.where(fresh, 0.0, acc_ref[hd]) + pv
            m_ref[hd] = m_new
            l_ref[hd] = l_new
            acc_ref[hd] = acc_new
            dv = acc_new.shape[0]
            ot_ref[i, hd * dv:(hd + 1) * dv, :] = acc_new / l_new

    bufs = ((st_a, cm_a), (st_b, cm_b))
    first = _group_blocks(qi_ref[0], grp_ref[0])
    for g in range(KV_GROUP):
        for hd in range(HEADS_PER_STEP):
            score_tile(qi_ref[0], first, g, hd, *bufs[0])

    def body(ss, carry):
        for k in range(STAGES_PER_TRIP):
            stage(STAGES_PER_TRIP * ss + k, bufs[k % 2], bufs[(k + 1) % 2])
        return carry

    n_trips = n_stages // STAGES_PER_TRIP
    lax.fori_loop(0, n_trips, body, 0)
    for s in range(n_trips * STAGES_PER_TRIP, n_stages):
        stage(s, bufs[s % 2], bufs[(s + 1) % 2])

    def emit(i, carry):
        write_out(i, ot_ref[i].T)
        return carry

    lax.fori_loop(0, ot_ref.shape[0], emit, 0)


def _attn_scratch(dv, nb):
    L = MOBA_BLOCK
    st = pltpu.VMEM((KV_GROUP, HEADS_PER_STEP, L, L), F32)
    cm = pltpu.VMEM((KV_GROUP, HEADS_PER_STEP, 1, L), F32)
    vec = pltpu.VMEM((HEADS_PER_STEP, 1, L), F32)
    return [st, cm, st, cm, vec, vec, pltpu.VMEM((HEADS_PER_STEP, dv, L), F32),
            pltpu.VMEM((nb, HEADS_PER_STEP * dv, L), F32)]


def _moba_kernel(qi_ref, grp_ref, q_ref, k_ref, vt_ref, kmean_ref, o_ref, qh_ref, bias_ref, *scratch,
                 n_stages):
    L = MOBA_BLOCK
    nb = kmean_ref.shape[1]
    lane = lax.broadcasted_iota(jnp.int32, (L, LANES), 1)
    kmean = kmean_ref[0, :, 0, :]
    blk = lax.broadcasted_iota(jnp.int32, (nb, L), 0)

    def select_blocks(i, carry):
        rows = pl.ds(pl.multiple_of(i * L, L), L)
        q = q_ref[0, rows, :]
        past = blk < i
        for hd in range(HEADS_PER_STEP):
            own = (lane >= hd * MOBA_HEAD_DIM) & (lane < (hd + 1) * MOBA_HEAD_DIM)
            qh = jnp.where(own, q, jnp.zeros_like(q))
            qh_ref[hd, rows, :] = qh
            gate = _nt_dot(kmean, qh.astype(F32), precision=lax.Precision.HIGHEST)
            gate = jnp.where(past, gate, -jnp.inf)
            rank = jnp.zeros((nb, L), jnp.int32)
            for j in range(nb):
                gj = gate[j:j + 1, :]
                ahead = (gj > gate) | ((gj == gate) & (j < blk))
                rank = rank + ahead.astype(jnp.int32)
            bias_ref[hd, i] = jnp.where((blk == i) | (past & (rank < MOBA_TOPK)), 0.0, NEG)
        return carry

    lax.fori_loop(0, nb, select_blocks, 0)

    def write_out(i, o):
        o_ref[0, pl.ds(pl.multiple_of(i * L, L), L), :] = o

    _flash_sweep(
        (qi_ref, grp_ref), n_stages,
        q_tile=lambda i, hd: qh_ref[hd, pl.ds(pl.multiple_of(i * L, L), L), :],
        k_tile=lambda j: k_ref[0, pl.ds(pl.multiple_of(j * L, L), L), :],
        vt_tile=lambda j, hd: vt_ref[0, 0, j, pl.ds(hd * MOBA_HEAD_DIM, MOBA_HEAD_DIM), :],
        tile_bias=lambda i, hd, j, real: jnp.where(real, bias_ref[hd, i, pl.ds(j, 1), :], NEG),
        write_out=write_out, scratch=scratch)


def _moba(qa, ka, vat, kmean):
    B, S, _ = qa.shape
    L = MOBA_BLOCK
    nb = S // L
    qi, grp, n_stages = _stage_tables(nb)
    pair3 = lambda b, p, *_: (b, 0, p)
    grid_spec = pltpu.PrefetchScalarGridSpec(
        num_scalar_prefetch=2,
        grid=(B, N_PAIRS),
        in_specs=[
            pl.BlockSpec((1, S, LANES), pair3),
            pl.BlockSpec((1, S, LANES), pair3),
            pl.BlockSpec((1, 1, nb, LANES, L), lambda b, p, *_: (b, p, 0, 0, 0)),
            pl.BlockSpec((1, nb, 1, LANES), lambda b, p, *_: (b, 0, 0, p)),
        ],
        out_specs=pl.BlockSpec((1, S, LANES), pair3),
        scratch_shapes=[
            pltpu.VMEM((HEADS_PER_STEP, S, LANES), BF16),
            pltpu.VMEM((HEADS_PER_STEP, nb, nb, L), F32),
        ] + _attn_scratch(MOBA_HEAD_DIM, nb),
    )
    return pl.pallas_call(
        functools.partial(_moba_kernel, n_stages=n_stages),
        out_shape=jax.ShapeDtypeStruct((B, S, MOBA_WIDTH), F32),
        grid_spec=grid_spec,
        compiler_params=_cparams(("arbitrary", "arbitrary")),
        name="moba",
    )(jnp.asarray(qi), jnp.asarray(grp), qa, ka, vat, kmean)


def _mla_kernel(qi_ref, grp_ref, q_ref, k_ref, vt_ref, o_ref, *scratch, n_stages):
    L = MOBA_BLOCK

    def write_out(i, o):
        o_ref[0, pl.ds(pl.multiple_of(i * L, L), L), :] = o

    _flash_sweep(
        (qi_ref, grp_ref), n_stages,
        q_tile=lambda i, hd: q_ref[0, hd, pl.ds(pl.multiple_of(i * L, L), L), :],
        k_tile=lambda j: k_ref[0, 0, pl.ds(pl.multiple_of(j * L, L), L), :],
        vt_tile=lambda j, hd: vt_ref[0, 0, j, pl.ds(hd * MLA_V_DIM, MLA_V_DIM), :],
        tile_bias=lambda i, hd, j, real: jnp.where(real, 0.0, NEG),
        write_out=write_out, scratch=scratch)


def _mla(qcat, kcat, vmt):
    B, _, S, W = qcat.shape
    L = MOBA_BLOCK
    nb = S // L
    qi, grp, n_stages = _stage_tables(nb)
    grid_spec = pltpu.PrefetchScalarGridSpec(
        num_scalar_prefetch=2,
        grid=(B, N_PAIRS),
        in_specs=[
            pl.BlockSpec((1, HEADS_PER_STEP, S, W), lambda b, p, *_: (b, p, 0, 0)),
            pl.BlockSpec((1, 1, S, W), lambda b, p, *_: (b, p, 0, 0)),
            pl.BlockSpec((1, 1, nb, LANES, L), lambda b, p, *_: (b, p, 0, 0, 0)),
        ],
        out_specs=pl.BlockSpec((1, S, LANES), lambda b, p, *_: (b, 0, p)),
        scratch_shapes=_attn_scratch(MLA_V_DIM, nb),
    )
    return pl.pallas_call(
        functools.partial(_mla_kernel, n_stages=n_stages),
        out_shape=jax.ShapeDtypeStruct((B, S, MLA_WIDTH), F32),
        grid_spec=grid_spec,
        compiler_params=_cparams(("arbitrary", "arbitrary")),
        name="mla",
    )(jnp.asarray(qi), jnp.asarray(grp), qcat, kcat, vmt)


def _postattn_kernel(x_ref, om_ref, ol_ref, mod_ref, gom_ref, gol_ref, wo_ref, gpost_ref, gpre_ref,
                     wr_ref, br_ref, x1_ref, xl_ref, cnt_ref, ri_ref):
    tm = x_ref.shape[1]
    gt_a = mod_ref[0, 2:3, :]
    sh_f = mod_ref[0, 3:4, :]
    sc_f = mod_ref[0, 4:5, :]
    a = jnp.concatenate([_rms(om_ref[0], gom_ref[...]), _rms(ol_ref[0], gol_ref[...])], axis=-1)
    o = jnp.dot(a.astype(BF16), wo_ref[...], preferred_element_type=F32)
    x1 = x_ref[0] + gt_a * _rms(o, gpost_ref[...])
    x1_ref[0] = x1
    h2 = _rms(x1, gpre_ref[...]) * (1.0 + sc_f) + sh_f

    h2_hi = h2.astype(BF16)
    h2_lo = (h2 - h2_hi.astype(F32)).astype(BF16)
    logit = (jnp.dot(h2_hi, wr_ref[0], preferred_element_type=F32)
             + jnp.dot(h2_lo, wr_ref[0], preferred_element_type=F32)
             + jnp.dot(h2_hi, wr_ref[1], preferred_element_type=F32))
    biased = logit + br_ref[...]
    lane = lax.broadcasted_iota(jnp.int32, (tm, LANES), 1)
    big = jnp.int32(LANES)

    def first_argmax(v):
        mx = jnp.max(v, axis=-1, keepdims=True)
        return jnp.min(jnp.where(v == mx, lane, big), axis=-1, keepdims=True)

    def pick(v, idx):
        return jnp.sum(jnp.where(lane == idx, v, 0.0), axis=-1, keepdims=True)

    gmask = (lane >= N_EXPERTS) & (lane < N_EXPERTS + N_GROUPS)
    gl = jnp.where(gmask, logit, -jnp.inf)
    pg = jnp.exp(gl - jnp.max(gl, axis=-1, keepdims=True))
    g_lane = first_argmax(jnp.where(gmask, biased, -jnp.inf))
    g_prob = pick(pg, g_lane) / jnp.sum(pg, axis=-1, keepdims=True)
    e_lo = (g_lane - N_EXPERTS) * EXPERTS_PER_GROUP
    emask = (lane >= e_lo) & (lane < e_lo + EXPERTS_PER_GROUP)
    el = jnp.where(emask, logit, -jnp.inf)
    pe = jnp.exp(el - jnp.max(el, axis=-1, keepdims=True))
    eb = jnp.where(emask, biased, -jnp.inf)
    i1 = first_argmax(eb)
    i2 = first_argmax(jnp.where(lane == i1, -jnp.inf, eb))
    p1 = pick(pe, i1)
    p2 = pick(pe, i2)
    denom = p1 + p2
    w1 = p1 / denom * g_prob
    w2 = p2 / denom * g_prob

    hit1 = lane == i1
    hit2 = lane == i2
    onehot = jnp.where(hit1 | hit2, 1.0, 0.0).astype(BF16)
    tok_r = lax.broadcasted_iota(jnp.int32, (tm, tm), 0)
    tok_c = lax.broadcasted_iota(jnp.int32, (tm, tm), 1)
    csum = jnp.dot(jnp.where(tok_c <= tok_r, 1.0, 0.0).astype(BF16), onehot,
                   preferred_element_type=F32)
    cnt = csum[tm - 1:tm, :]
    seg = jnp.floor((cnt + (CHUNK - 1.0)) * (1.0 / CHUNK)) * CHUNK
    ex_r = lax.broadcasted_iota(jnp.int32, (LANES, LANES), 0)
    ex_c = lax.broadcasted_iota(jnp.int32, (LANES, LANES), 1)
    seg_start = jnp.dot(jnp.broadcast_to(seg, (8, LANES)).astype(BF16),
                        jnp.where(ex_r < ex_c, 1.0, 0.0).astype(BF16),
                        preferred_element_type=F32)[0:1, :]
    place = seg_start + csum - 1.0
    slot1 = jnp.sum(jnp.where(hit1, place, 0.0), axis=-1, keepdims=True)
    slot2 = jnp.sum(jnp.where(hit2, place, 0.0), axis=-1, keepdims=True)
    ri_ref[0] = jnp.where(lane == 0, i1, jnp.where(lane == 1, i2, jnp.where(
        lane == 2, slot1.astype(jnp.int32), jnp.where(lane == 3, slot2.astype(jnp.int32), 0))))
    cnt_ref[0, 0] = cnt

    packed = jnp.where(lane == 0, slot1, jnp.where(lane == 1, slot2, jnp.where(
        lane == 2, w1, jnp.where(lane == 3, w2, 0.0))))
    packed_t = packed.T
    srow = lax.broadcasted_iota(jnp.int32, (LOCAL_ROWS, tm), 0).astype(F32)
    at1 = srow == packed_t[0:1, :]
    at2 = srow == packed_t[1:2, :]
    perm = jnp.where(at1 | at2, 1.0, 0.0).astype(BF16)
    xl_ref[0, :, 0:D_MODEL] = jnp.dot(perm, h2_hi, preferred_element_type=F32)
    gate = jnp.sum(jnp.where(at1, packed_t[2:3, :], 0.0) + jnp.where(at2, packed_t[3:4, :], 0.0),
                   axis=-1, keepdims=True)
    xl_ref[0, :, D_MODEL:D_MODEL + LANES] = jnp.broadcast_to(gate, (LOCAL_ROWS, LANES))


def _postattn(x, o_moba, o_mla, mod3, g_om, g_ol, w_o, g_post, g_pre, w_r, b_r):
    B, S, D = x.shape
    tm = TM_IN
    assert S % tm == 0 and D == D_MODEL
    const2 = lambda b, i: (0, 0)
    tok3 = lambda b, i: (b, i, 0)
    n_j = S // tm
    return pl.pallas_call(
        _postattn_kernel,
        out_shape=[
            jax.ShapeDtypeStruct((B, S, D), F32),
            jax.ShapeDtypeStruct((B * n_j, LOCAL_ROWS, D + LANES), F32),
            jax.ShapeDtypeStruct((B * n_j, 1, 1, LANES), F32),
            jax.ShapeDtypeStruct((B, S, LANES), jnp.int32),
        ],
        grid=(B, S // tm),
        in_specs=[
            pl.BlockSpec((1, tm, D), tok3),
            pl.BlockSpec((1, tm, MOBA_WIDTH), tok3),
            pl.BlockSpec((1, tm, MLA_WIDTH), tok3),
            pl.BlockSpec((1, 6, D), lambda b, i: (b, 0, 0)),
            pl.BlockSpec((1, MOBA_WIDTH), const2),
            pl.BlockSpec((1, MLA_WIDTH), const2),
            pl.BlockSpec(w_o.shape, const2),
            pl.BlockSpec((1, D), const2),
            pl.BlockSpec((1, D), const2),
            pl.BlockSpec(w_r.shape, lambda b, i: (0, 0, 0)),
            pl.BlockSpec((1, LANES), const2),
        ],
        out_specs=[
            pl.BlockSpec((1, tm, D), tok3),
            pl.BlockSpec((1, LOCAL_ROWS, D + LANES), lambda b, i: (b * n_j + i, 0, 0)),
            pl.BlockSpec((1, 1, 1, LANES), lambda b, i: (b * n_j + i, 0, 0, 0)),
            pl.BlockSpec((1, tm, LANES), tok3),
        ],
        compiler_params=_cparams(("arbitrary", "arbitrary")),
        name="postattn",
    )(x, o_moba, o_mla, mod3, g_om, g_ol, w_o, g_post, g_pre, w_r, b_r)


def _start_chunk_gather(idx_ref, base, n_chunks, src_hbm, dst_ref, sem):
    for k in range(n_chunks):
        row = pl.multiple_of(idx_ref[base + k], CHUNK)
        pltpu.make_async_copy(src_hbm.at[pl.ds(row, CHUNK), :], dst_ref.at[pl.ds(k * CHUNK, CHUNK), :], sem).start()


def _wait_chunk_gather(n_chunks, src_hbm, dst_ref, sem):
    pltpu.make_async_copy(src_hbm.at[pl.ds(0, n_chunks * CHUNK), :], dst_ref, sem).wait()


def _moe_kernel(te_ref, nu_ref, src_ref, xl_hbm, wgu_ref, wd_ref, y_ref, xbuf, sem):
    tm = y_ref.shape[0]
    n_chunks = tm // CHUNK
    i = pl.program_id(0)
    n_used = nu_ref[0]
    slot = i % 2

    @pl.when((i == 0) & (n_used > 0))
    def _():
        _start_chunk_gather(src_ref, 0, n_chunks, xl_hbm, xbuf.at[0], sem.at[0])

    @pl.when(i + 1 < n_used)
    def _():
        _start_chunk_gather(src_ref, (i + 1) * n_chunks, n_chunks, xl_hbm, xbuf.at[1 - slot], sem.at[1 - slot])

    @pl.when(i < n_used)
    def _():
        _wait_chunk_gather(n_chunks, xl_hbm, xbuf.at[slot], sem.at[slot])
        xb = xbuf[slot, :, 0:D_MODEL].astype(BF16)
        gate = xbuf[slot, :, D_MODEL:D_MODEL + LANES]
        gu = jnp.dot(xb, wgu_ref[0], preferred_element_type=F32)
        g = gu[:, :D_EXPERT]
        u = gu[:, D_EXPERT:]
        hid = g * jax.nn.sigmoid(g) * u
        y = jnp.dot(hid.astype(BF16), wd_ref[0], preferred_element_type=F32)
        y_ref[...] = jnp.concatenate([gate] * (D_MODEL // LANES), axis=1) * y

    @pl.when(i >= n_used)
    def _():
        y_ref[...] = jnp.zeros_like(y_ref)


def _moe(tile_expert, n_used, chunk_src, x_local, w_gu, w_d):
    D = D_MODEL
    n_tiles = tile_expert.shape[0]
    tm = TM_MOE
    grid_spec = pltpu.PrefetchScalarGridSpec(
        num_scalar_prefetch=3,
        grid=(n_tiles,),
        in_specs=[
            pl.BlockSpec(memory_space=pl.ANY),
            pl.BlockSpec((1, D, 2 * D_EXPERT), lambda i, te, nu, src: (te[i], 0, 0)),
            pl.BlockSpec((1, D_EXPERT, D), lambda i, te, nu, src: (te[i], 0, 0)),
        ],
        out_specs=pl.BlockSpec((tm, D), lambda i, te, nu, src: (i, 0)),
        scratch_shapes=[pltpu.VMEM((2, tm, D + LANES), F32), pltpu.SemaphoreType.DMA((2,))],
    )
    return pl.pallas_call(
        _moe_kernel,
        out_shape=jax.ShapeDtypeStruct((n_tiles * tm, D), F32),
        grid_spec=grid_spec,
        compiler_params=_cparams(("arbitrary",)),
        name="moe",
    )(tile_expert, n_used, chunk_src, x_local, w_gu, w_d)


def _combine_kernel(src_ref, y_hbm, x1_ref, ri_ref, mod_ref, g_ref, o_ref, ybuf, sem):
    tm = x1_ref.shape[1]
    n_chunks = LOCAL_ROWS // CHUNK
    n_j = pl.num_programs(1)
    step = pl.program_id(0) * n_j + pl.program_id(1)
    n_steps = pl.num_programs(0) * n_j
    slot = step % 2

    @pl.when(step == 0)
    def _():
        _start_chunk_gather(src_ref, 0, n_chunks, y_hbm, ybuf.at[0], sem.at[0])

    @pl.when(step + 1 < n_steps)
    def _():
        _start_chunk_gather(src_ref, (step + 1) * n_chunks, n_chunks, y_hbm, ybuf.at[1 - slot], sem.at[1 - slot])

    _wait_chunk_gather(n_chunks, y_hbm, ybuf.at[slot], sem.at[slot])
    yl = ybuf[slot]
    y_hi = yl.astype(BF16)
    y_lo = (yl - y_hi.astype(F32)).astype(BF16)
    col = lax.broadcasted_iota(jnp.int32, (tm, LOCAL_ROWS), 1)
    pick = jnp.where((col == ri_ref[0, :, 2:3]) | (col == ri_ref[0, :, 3:4]), 1.0, 0.0).astype(BF16)
    y = jnp.dot(pick, y_hi, preferred_element_type=F32) + jnp.dot(pick, y_lo, preferred_element_type=F32)
    gt_f = mod_ref[0, 5:6, :]
    o_ref[0] = x1_ref[0] + gt_f * _rms(y, g_ref[...])


def _combine(chunk_src, y_sorted, x1, route_i, mod3, g_post):
    B, S, D = x1.shape
    tm = TM_IN
    grid_spec = pltpu.PrefetchScalarGridSpec(
        num_scalar_prefetch=1,
        grid=(B, S // tm),
        in_specs=[
            pl.BlockSpec(memory_space=pl.ANY),
            pl.BlockSpec((1, tm, D), lambda b, j, src: (b, j, 0)),
            pl.BlockSpec((1, tm, LANES), lambda b, j, src: (b, j, 0)),
            pl.BlockSpec((1, 6, D), lambda b, j, src: (b, 0, 0)),
            pl.BlockSpec((1, D), lambda b, j, src: (0, 0)),
        ],
        out_specs=pl.BlockSpec((1, tm, D), lambda b, j, src: (b, j, 0)),
        scratch_shapes=[pltpu.VMEM((2, LOCAL_ROWS, D), F32), pltpu.SemaphoreType.DMA((2,))],
    )
    return pl.pallas_call(
        _combine_kernel,
        out_shape=jax.ShapeDtypeStruct((B, S, D), F32),
        grid_spec=grid_spec,
        compiler_params=_cparams(("arbitrary", "arbitrary")),
        name="combine",
    )(chunk_src, y_sorted, x1, route_i, mod3, g_post)


def _rope_angles(positions):
    def cos_sin(dim):
        half = dim // 2
        inv_freq = jnp.exp(-math.log(ROPE_THETA) * jnp.arange(half, dtype=F32) * (2.0 / dim))
        ang = positions.astype(F32)[:, None, :] * inv_freq[None, :, None]
        return [jnp.cos(ang), jnp.sin(ang)]
    B, S = positions.shape
    parts = cos_sin(MOBA_ROT_DIMS) + cos_sin(MLA_ROPE_DIM) + [jnp.ones((B, 1, S), F32)]
    used = MOBA_ROT_DIMS + MLA_ROPE_DIM + 1
    return jnp.concatenate(parts + [jnp.zeros((B, LANES - used, S), F32)], axis=1)


def _routing_metadata(counts, tm):
    i32 = jnp.int32
    n_tt = counts.shape[0]
    n_tiles = (2 * n_tt * TM_IN + n_tt * N_EXPERTS * (CHUNK - 1)) // tm + N_EXPERTS + 1
    seg = (counts + CHUNK - 1) // CHUNK * CHUNK
    loc_start = jnp.cumsum(seg, axis=1) - seg
    loc_total = jnp.sum(seg, axis=1)
    run = jnp.cumsum(seg, axis=0)
    rows_e = run[-1]
    tiles_per = (rows_e + tm - 1) // tm
    tile_end = jnp.cumsum(tiles_per)
    n_used = tile_end[-1]
    first_row = (tile_end - tiles_per) * tm
    tile_ids = jnp.arange(n_tiles, dtype=i32)
    te = jnp.minimum(jnp.sum((tile_ids[:, None] >= tile_end[None, :]).astype(i32), axis=1), N_EXPERTS - 1)
    last_e = jnp.sum(jnp.where(tile_ids == n_used - 1, te, 0))
    tile_expert = jnp.where(tile_ids < n_used, te, last_e).astype(i32)

    cpt = tm // CHUNK
    is_e = (tile_expert[:, None] == jnp.arange(N_EXPERTS, dtype=i32)[None, :]).astype(i32)
    of_tile = lambda a: jnp.sum(is_e[:, None, :] * a[None, :, :], axis=2)
    run_t, seg_t, loc_t = of_tile(run), of_tile(seg), of_tile(loc_start)
    first_t = jnp.sum(is_e * first_row[None, :], axis=1)
    rows_t = jnp.sum(is_e * rows_e[None, :], axis=1)
    off = (tile_ids * tm - first_t)[:, None] + jnp.arange(cpt, dtype=i32)[None, :] * CHUNK
    tau = jnp.minimum(jnp.sum((off[:, :, None] >= run_t[:, None, :]).astype(i32), axis=2), n_tt - 1)
    is_tau = (tau[:, :, None] == jnp.arange(n_tt, dtype=i32)[None, None, :]).astype(i32)
    at_tau = lambda a_t: jnp.sum(is_tau * a_t[:, None, :], axis=2)
    src = tau * LOCAL_ROWS + at_tau(loc_t) + off - (at_tau(run_t) - at_tau(seg_t))
    real = (tile_ids < n_used)[:, None] & (off < rows_t[:, None])
    zero_local = LOCAL_ROWS - CHUNK
    moe_src = jnp.where(real, src, zero_local).astype(i32).reshape(-1)

    lrow = jnp.arange(LOCAL_ROWS // CHUNK, dtype=i32) * CHUNK
    loc_end = loc_start + seg
    e_v = jnp.minimum(jnp.sum((lrow[None, :, None] >= loc_end[:, None, :]).astype(i32), axis=2), N_EXPERTS - 1)
    is_ev = (e_v[:, :, None] == jnp.arange(N_EXPERTS, dtype=i32)[None, None, :]).astype(i32)
    seg_first = first_row[None, :] + run - seg
    src_v = jnp.sum(is_ev * (seg_first - loc_start)[:, None, :], axis=2) + lrow[None, :]
    zero_sorted = (n_tiles - 1) * tm
    cmb_src = jnp.where(lrow[None, :] < loc_total[:, None], src_v, zero_sorted).astype(i32).reshape(-1)
    return tile_expert, n_used.reshape(1).astype(i32), moe_src, cmb_src


def kernel(x, c, positions, w_ada, b_ada, g_pre_attn, g_post_attn, g_pre_ffn, g_post_ffn, w_in, g_cq, w_uq,
           g_ckv, w_ukv, g_out_moba, g_out_mla, w_o, w_group, b_group, w_expert, b_expert, w_e_gate, w_e_up,
           w_e_down):
    B, S, D = x.shape
    depth = w_ada.shape[0]
    rope_cs = _rope_angles(positions)
    rope_exp = jnp.asarray(_rope_expansion(), BF16)
    row = lambda v: v.reshape(1, -1)

    for l in range(depth):
        n_main = 3 * MOBA_WIDTH + MLA_Q_RANK + MLA_KV_RANK
        w_in_p = jnp.concatenate([w_in[l][:, :n_main], jnp.tile(w_in[l][:, n_main:], (1, LANES // MLA_ROPE_DIM))],
                                 axis=1).astype(BF16)
        wq = w_uq[l].reshape(MLA_Q_RANK, MLA_HEADS, MLA_NOPE_DIM + MLA_ROPE_DIM)
        wq_n = wq[:, :, :MLA_NOPE_DIM]
        zero_n = jnp.zeros_like(wq_n)
        even = (jnp.arange(MLA_HEADS) % 2 == 0)[None, :, None]
        w_uq_n = jnp.concatenate([jnp.where(even, wq_n, zero_n), jnp.where(even, zero_n, wq_n)], axis=-1)
        w_uq_n = w_uq_n.reshape(MLA_Q_RANK, MLA_HEADS * LANES).astype(BF16)
        w_uq_p = wq[:, :, MLA_NOPE_DIM:].reshape(MLA_Q_RANK, MLA_HEADS * MLA_ROPE_DIM).astype(BF16)
        wkv = w_ukv[l].reshape(MLA_KV_RANK, MLA_HEADS, MLA_NOPE_DIM + MLA_V_DIM)
        w_ukv_p = jnp.concatenate([wkv[:, :, :MLA_NOPE_DIM].reshape(MLA_KV_RANK, -1),
                                   wkv[:, :, MLA_NOPE_DIM:].reshape(MLA_KV_RANK, -1)], axis=1).astype(BF16)
        w_r = jnp.concatenate([w_expert[l], w_group[l],
                               jnp.zeros((D, LANES - N_EXPERTS - N_GROUPS), F32)], axis=1)
        w_r_hi = w_r.astype(BF16)
        w_r = jnp.stack([w_r_hi, (w_r - w_r_hi.astype(F32)).astype(BF16)])
        b_r = jnp.concatenate([b_expert[l], b_group[l], jnp.zeros((LANES - N_EXPERTS - N_GROUPS,), F32)]).reshape(1, LANES)
        w_gu = jnp.concatenate([w_e_gate[l], w_e_up[l]], axis=-1).astype(BF16)
        w_d = w_e_down[l].astype(BF16)

        mod3 = _adaln(c, w_ada[l], b_ada[l]).reshape(B, 6, D)

        qa, ka, vat, kmean, qcat, kcat, vmt = _inproj(
            x, mod3, row(g_pre_attn[l]), w_in_p, rope_cs, rope_exp, row(g_cq[l]), w_uq_n, w_uq_p,
            row(g_ckv[l]), w_ukv_p)
        o_moba = _moba(qa, ka, vat, kmean)
        o_mla = _mla(qcat, kcat, vmt)
        x1, x_local, counts, route_i = _postattn(
            x, o_moba, o_mla, mod3, row(g_out_moba[l]), row(g_out_mla[l]), w_o[l].astype(BF16),
            row(g_post_attn[l]), row(g_pre_ffn[l]), w_r, b_r)

        counts = counts.reshape(-1, LANES)[:, :N_EXPERTS].astype(jnp.int32)
        tile_expert, n_used, moe_src, cmb_src = _routing_metadata(counts, TM_MOE)
        y_sorted = _moe(tile_expert, n_used, moe_src, x_local.reshape(-1, D + LANES), w_gu, w_d)
        x = _combine(cmb_src, y_sorted, x1, route_i, mod3, row(g_post_ffn[l]))
    return x
```

```python
import functools
import math

import jax
import jax.numpy as jnp
import numpy as np
from jax import lax
from jax.experimental import pallas as pl
from jax.experimental.pallas import tpu as pltpu

F32 = jnp.float32
BF16 = jnp.bfloat16

D_MODEL = 1024
MOBA_HEADS = 8
MOBA_HEAD_DIM = 64
MOBA_WIDTH = MOBA_HEADS * MOBA_HEAD_DIM
MOBA_BLOCK = 256
MOBA_TOPK = 3
MOBA_ROT_DIMS = MOBA_HEAD_DIM // 4
MLA_HEADS = 8
MLA_Q_RANK = 256
MLA_KV_RANK = 128
MLA_NOPE_DIM = 64
MLA_ROPE_DIM = 32
MLA_V_DIM = 64
MLA_WIDTH = MLA_HEADS * MLA_V_DIM
ROPE_THETA = 500000.0
N_GROUPS = 4
EXPERTS_PER_GROUP = 8
N_EXPERTS = N_GROUPS * EXPERTS_PER_GROUP
D_EXPERT = 256
EPS = 1e-6
NEG = -1e30
LOG2E = math.log2(math.e)

LANES = 128
HEADS_PER_STEP = LANES // MOBA_HEAD_DIM
N_PAIRS = MOBA_HEADS // HEADS_PER_STEP
VMEM_LIMIT = 56 * 1024 * 1024

TM_IN = 512
KV_GROUP = 4
STAGES_PER_TRIP = 4
TM_MOE = 256
CHUNK = 8
LOCAL_ROWS = 2 * TM_IN + N_EXPERTS * CHUNK
XL_WIDTH = D_MODEL + LANES


def _cparams(sem):
    return pltpu.CompilerParams(dimension_semantics=sem, vmem_limit_bytes=VMEM_LIMIT)


def _rms(x, g):
    return x * lax.rsqrt(jnp.mean(x * x, axis=-1, keepdims=True) + EPS) * g


def _adaln_kernel(c_ref, w_ref, b_ref, o_ref):
    c = c_ref[...]
    ca = c * jax.nn.sigmoid(c)
    o_ref[...] = jnp.dot(ca, w_ref[...], preferred_element_type=F32,
                         precision=lax.Precision.HIGHEST) + b_ref[...]


def _adaln(c, w, b):
    B, D = c.shape
    N = w.shape[1]
    tn = 1536 if N % 1536 == 0 else N
    return pl.pallas_call(
        _adaln_kernel,
        out_shape=jax.ShapeDtypeStruct((B, N), F32),
        grid=(N // tn,),
        in_specs=[pl.BlockSpec((B, D), lambda j: (0, 0)),
                  pl.BlockSpec((D, tn), lambda j: (0, j)),
                  pl.BlockSpec((1, tn), lambda j: (0, j))],
        out_specs=pl.BlockSpec((B, tn), lambda j: (0, j)),
        compiler_params=_cparams(("arbitrary",)),
        name="adaln",
    )(c, w, b.reshape(1, N))


def _rope128(xc, tab, shift):
    c = tab[:, 0:LANES]
    s1 = tab[:, LANES:2 * LANES]
    s2 = tab[:, 2 * LANES:3 * LANES]
    return xc * c + pltpu.roll(xc, LANES - shift, axis=1) * s1 + pltpu.roll(xc, shift, axis=1) * s2


def _rope_expansion():
    e = np.zeros((LANES, 6 * LANES), np.float32)
    ha, hm = MOBA_ROT_DIMS // 2, MLA_ROPE_DIM // 2
    cos_a, sin_a, cos_m, sin_m, one = 0, ha, 2 * ha, 2 * ha + hm, 2 * ha + 2 * hm
    for lane in range(LANES):
        d = lane % MOBA_HEAD_DIM
        if d < ha:
            e[cos_a + d, lane] = 1.0
            e[sin_a + d, LANES + lane] = -1.0
        elif d < 2 * ha:
            e[cos_a + d - ha, lane] = 1.0
            e[sin_a + d - ha, 2 * LANES + lane] = 1.0
        else:
            e[one, lane] = 1.0
        d = lane % MLA_ROPE_DIM
        e[cos_m + d % hm, 3 * LANES + lane] = 1.0
        if d < hm:
            e[sin_m + d, 4 * LANES + lane] = -1.0
        else:
            e[sin_m + d - hm, 5 * LANES + lane] = 1.0
    return e


def _expand_exact(v, e_ref):
    hi = v.astype(BF16)
    r1 = v - hi.astype(F32)
    mid = r1.astype(BF16)
    lo = (r1 - mid.astype(F32)).astype(BF16)
    e = e_ref[...]
    return (jnp.dot(hi, e, preferred_element_type=F32) + jnp.dot(mid, e, preferred_element_type=F32)
            + jnp.dot(lo, e, preferred_element_type=F32))


def _store_vt(src, col0, vt_ref):
    for p in range(N_PAIRS):
        for r in range(src.shape[0] // MOBA_BLOCK):
            blk = src[r * MOBA_BLOCK:(r + 1) * MOBA_BLOCK, col0 + p * LANES:col0 + (p + 1) * LANES]
            vt_ref[0, p, r] = blk.T.astype(BF16)


def _inproj_kernel(x_ref, mod_ref, gpre_ref, win_ref, cs_ref, rexp_ref, gcq_ref, wuqn_ref, wuqp_ref,
                   gckv_ref, wukv_ref,
                   qa_ref, ka_ref, vat_ref, kmean_ref, qcat_ref, kcat_ref, vmt_ref):
    tm = x_ref.shape[1]
    rope_tab = _expand_exact(cs_ref[0].T, rexp_ref)
    tab_a = rope_tab[:, 0:3 * LANES]
    tab_m = rope_tab[:, 3 * LANES:6 * LANES]
    x = x_ref[0]
    sh = mod_ref[0, 0:1, :]
    sc = mod_ref[0, 1:2, :]
    h = _rms(x, gpre_ref[...]) * (1.0 + sc) + sh
    z = jnp.dot(h.astype(BF16), win_ref[...], preferred_element_type=F32)

    moba_scale = MOBA_HEAD_DIM ** -0.5 * LOG2E
    half_a = MOBA_ROT_DIMS // 2
    n_chunks = MOBA_WIDTH // LANES
    for cidx in range(n_chunks):
        lo = cidx * LANES
        q = _rope128(z[:, lo:lo + LANES], tab_a, half_a)
        qa_ref[0, :, lo:lo + LANES] = (q * moba_scale).astype(BF16)
        k = _rope128(z[:, MOBA_WIDTH + lo:MOBA_WIDTH + lo + LANES], tab_a, half_a)
        ka_ref[0, :, lo:lo + LANES] = k.astype(BF16)
        for r in range(tm // MOBA_BLOCK):
            kmean_ref[0, r, :, lo:lo + LANES] = jnp.mean(
                k[r * MOBA_BLOCK:(r + 1) * MOBA_BLOCK], axis=0, keepdims=True)
    _store_vt(z, 2 * MOBA_WIDTH, vat_ref)

    o_cq = 3 * MOBA_WIDTH
    o_ckv = o_cq + MLA_Q_RANK
    o_kpe = o_ckv + MLA_KV_RANK
    mla_scale = (MLA_NOPE_DIM + MLA_ROPE_DIM) ** -0.5 * LOG2E
    half_m = MLA_ROPE_DIM // 2

    cqn = _rms(z[:, o_cq:o_cq + MLA_Q_RANK], gcq_ref[...]).astype(BF16)
    qn = jnp.dot(cqn, wuqn_ref[...], preferred_element_type=F32) * mla_scale
    qp = jnp.dot(cqn, wuqp_ref[...], preferred_element_type=F32)
    lane = lax.broadcasted_iota(jnp.int32, (tm, LANES), 1)
    heads_per_chunk = LANES // MLA_ROPE_DIM
    qp_chunks = [_rope128(qp[:, g * LANES:(g + 1) * LANES], tab_m, half_m) * mla_scale
                 for g in range(MLA_HEADS // heads_per_chunk)]
    for hd in range(MLA_HEADS):
        qcat_ref[0, hd, :, 0:LANES] = qn[:, hd * LANES:(hd + 1) * LANES].astype(BF16)
        sub = hd % heads_per_chunk
        keep = (lane >= sub * MLA_ROPE_DIM) & (lane < (sub + 1) * MLA_ROPE_DIM)
        qcat_ref[0, hd, :, LANES:2 * LANES] = jnp.where(keep, qp_chunks[hd // heads_per_chunk], 0.0).astype(BF16)

    ckvn = _rms(z[:, o_ckv:o_ckv + MLA_KV_RANK], gckv_ref[...]).astype(BF16)
    kv = jnp.dot(ckvn, wukv_ref[...], preferred_element_type=F32)
    kpe = _rope128(z[:, o_kpe:o_kpe + LANES], tab_m, half_m).astype(BF16)
    for p in range(N_PAIRS):
        kcat_ref[0, p, :, 0:LANES] = kv[:, p * LANES:(p + 1) * LANES].astype(BF16)
        kcat_ref[0, p, :, LANES:2 * LANES] = kpe
    _store_vt(kv, MLA_WIDTH, vmt_ref)


def _inproj(x, mod3, g_pre, w_in_p, rope_cs, rope_exp, g_cq, w_uq_n, w_uq_p, g_ckv, w_ukv_p):
    B, S, D = x.shape
    tm = min(TM_IN, S)
    nb = S // MOBA_BLOCK
    rb = tm // MOBA_BLOCK
    const2 = lambda b, i: (0, 0)
    tok3 = lambda b, i: (b, i, 0)
    outs = pl.pallas_call(
        _inproj_kernel,
        out_shape=[
            jax.ShapeDtypeStruct((B, S, MOBA_WIDTH), BF16),
            jax.ShapeDtypeStruct((B, S, MOBA_WIDTH), BF16),
            jax.ShapeDtypeStruct((B, N_PAIRS, nb, LANES, MOBA_BLOCK), BF16),
            jax.ShapeDtypeStruct((B, nb, 1, MOBA_WIDTH), F32),
            jax.ShapeDtypeStruct((B, MLA_HEADS, S, 2 * LANES), BF16),
            jax.ShapeDtypeStruct((B, N_PAIRS, S, 2 * LANES), BF16),
            jax.ShapeDtypeStruct((B, N_PAIRS, nb, LANES, MOBA_BLOCK), BF16),
        ],
        grid=(B, S // tm),
        in_specs=[
            pl.BlockSpec((1, tm, D), tok3),
            pl.BlockSpec((1, 6, D), lambda b, i: (b, 0, 0)),
            pl.BlockSpec((1, D), const2),
            pl.BlockSpec(w_in_p.shape, const2),
            pl.BlockSpec((1, LANES, tm), lambda b, i: (b, 0, i)),
            pl.BlockSpec(rope_exp.shape, const2),
            pl.BlockSpec((1, MLA_Q_RANK), const2),
            pl.BlockSpec(w_uq_n.shape, const2),
            pl.BlockSpec(w_uq_p.shape, const2),
            pl.BlockSpec((1, MLA_KV_RANK), const2),
            pl.BlockSpec(w_ukv_p.shape, const2),
        ],
        out_specs=[
            pl.BlockSpec((1, tm, MOBA_WIDTH), tok3),
            pl.BlockSpec((1, tm, MOBA_WIDTH), tok3),
            pl.BlockSpec((1, N_PAIRS, rb, LANES, MOBA_BLOCK), lambda b, i: (b, 0, i, 0, 0)),
            pl.BlockSpec((1, rb, 1, MOBA_WIDTH), lambda b, i: (b, i, 0, 0)),
            pl.BlockSpec((1, MLA_HEADS, tm, 2 * LANES), lambda b, i: (b, 0, i, 0)),
            pl.BlockSpec((1, N_PAIRS, tm, 2 * LANES), lambda b, i: (b, 0, i, 0)),
            pl.BlockSpec((1, N_PAIRS, rb, LANES, MOBA_BLOCK), lambda b, i: (b, 0, i, 0, 0)),
        ],
        compiler_params=_cparams(("arbitrary", "arbitrary")),
        name="inproj",
    )(x, mod3, g_pre, w_in_p, rope_cs, rope_exp, g_cq, w_uq_n, w_uq_p, g_ckv, w_ukv_p)
    return outs


def _nt_dot(a, b, **kw):
    return lax.dot_general(a, b, (((1,), (1,)), ((), ())), preferred_element_type=F32, **kw)


def _group_blocks(i, t):
    out = []
    for g in range(KV_GROUP):
        j = i - KV_GROUP * t - (KV_GROUP - 1 - g)
        out.append((jnp.maximum(j, 0), j >= 0))
    return out


def _stage_tables(nb):
    qi, grp = [], []
    for i in range(nb):
        for t in range((i + KV_GROUP) // KV_GROUP):
            qi.append(i)
            grp.append(t)
    n = len(qi)
    qi.append(qi[-1])
    grp.append(grp[-1])
    return np.asarray(qi, np.int32), np.asarray(grp, np.int32), n


def _flash_sweep(tables, n_stages, q_tile, k_tile, vt_tile, tile_bias, write_out, scratch):
    qi_ref, grp_ref = tables
    st_a, cm_a, st_b, cm_b, m_ref, l_ref, acc_ref, ot_ref = scratch
    L = MOBA_BLOCK
    key_minus_qry = (lax.broadcasted_iota(jnp.int32, (L, L), 0) - lax.broadcasted_iota(jnp.int32, (L, L), 1))

    def score_tile(i, blocks, g, hd, st_buf, cm_buf):
        j, _ = blocks[g]
        st = _nt_dot(k_tile(j), q_tile(i, hd))
        if g == KV_GROUP - 1:
            st = jnp.where(key_minus_qry <= (i - j) * L, st, NEG)
        st_buf[g, hd] = st
        cm_buf[g, hd] = jnp.max(st, axis=0, keepdims=True)

    def stage(s, cur, nxt):
        i, t = qi_ref[s], grp_ref[s]
        i_nxt = qi_ref[s + 1]
        blocks = _group_blocks(i, t)
        nxt_blocks = _group_blocks(i_nxt, grp_ref[s + 1])

        fresh = t == 0
        for hd in range(HEADS_PER_STEP):
            biases = [tile_bias(i, hd, j, real) for j, real in blocks]
            m_old = jnp.where(fresh, -jnp.inf, m_ref[hd])
            m_new = m_old
            for g in range(KV_GROUP):
                m_new = jnp.maximum(m_new, cur[1][g, hd] + biases[g])
            alpha = jnp.exp2(m_old - m_new)
            l_new = alpha * jnp.where(fresh, 0.0, l_ref[hd])
            pv = None
            for g, (j, _) in enumerate(blocks):
                score_tile(i_nxt, nxt_blocks, g, hd, *nxt)
                p = jnp.exp2(cur[0][g, hd] - (m_new - biases[g]))
                l_new = l_new + jnp.sum(p, axis=0, keepdims=True)
                d = jnp.dot(vt_tile(j, hd), p.astype(BF16), preferred_element_type=F32)
                pv = d if pv is None else pv + d
            acc_new = alpha * jnp.where(fresh, 0.0, acc_ref[hd]) + pv
            m_ref[hd] = m_new
            l_ref[hd] = l_new
            acc_ref[hd] = acc_new
            dv = acc_new.shape[0]
            ot_ref[i, hd * dv:(hd + 1) * dv, :] = acc_new / l_new

    bufs = ((st_a, cm_a), (st_b, cm_b))
    first = _group_blocks(qi_ref[0], grp_ref[0])
    for g in range(KV_GROUP):
        for hd in range(HEADS_PER_STEP):
            score_tile(qi_ref[0], first, g, hd, *bufs[0])

    def body(ss, carry):
        for k in range(STAGES_PER_TRIP):
            stage(STAGES_PER_TRIP * ss + k, bufs[k % 2], bufs[(k + 1) % 2])
        return carry

    n_trips = n_stages // STAGES_PER_TRIP
    lax.fori_loop(0, n_trips, body, 0)
    for s in range(n_trips * STAGES_PER_TRIP, n_stages):
        stage(s, bufs[s % 2], bufs[(s + 1) % 2])

    def emit(i, carry):
        write_out(i, ot_ref[i].T)
        return carry

    lax.fori_loop(0, ot_ref.shape[0], emit, 0)


def _attn_scratch(dv, nb):
    L = MOBA_BLOCK
    st = pltpu.VMEM((KV_GROUP, HEADS_PER_STEP, L, L), F32)
    cm = pltpu.VMEM((KV_GROUP, HEADS_PER_STEP, 1, L), F32)
    vec = pltpu.VMEM((HEADS_PER_STEP, 1, L), F32)
    return [st, cm, st, cm, vec, vec, pltpu.VMEM((HEADS_PER_STEP, dv, L), F32),
            pltpu.VMEM((nb, HEADS_PER_STEP * dv, L), F32)]


def _moba_kernel(qi_ref, grp_ref, q_ref, k_ref, vt_ref, kmean_ref, o_ref, qh_ref, bias_ref, *scratch,
                 n_stages):
    L = MOBA_BLOCK
    nb = kmean_ref.shape[1]
    lane = lax.broadcasted_iota(jnp.int32, (L, LANES), 1)
    kmean = kmean_ref[0, :, 0, :]
    blk = lax.broadcasted_iota(jnp.int32, (nb, L), 0)

    def select_blocks(i, carry):
        rows = pl.ds(pl.multiple_of(i * L, L), L)
        q = q_ref[0, rows, :]
        past = blk < i
        for hd in range(HEADS_PER_STEP):
            own = (lane >= hd * MOBA_HEAD_DIM) & (lane < (hd + 1) * MOBA_HEAD_DIM)
            qh = jnp.where(own, q, jnp.zeros_like(q))
            qh_ref[hd, rows, :] = qh
            gate = _nt_dot(kmean, qh.astype(F32), precision=lax.Precision.HIGHEST)
            gate = jnp.where(past, gate, -jnp.inf)
            bias = jnp.where(blk == i, 0.0, NEG)
            for _ in range(MOBA_TOPK):
                top = jnp.max(gate, axis=0, keepdims=True)
                at = blk == jnp.min(jnp.where(gate == top, blk, nb), axis=0, keepdims=True)
                bias = jnp.where(at & (top > -jnp.inf), 0.0, bias)
                gate = jnp.where(at, -jnp.inf, gate)
            bias_ref[hd, i] = bias
        return carry

    lax.fori_loop(0, nb, select_blocks, 0, unroll=4 if nb % 4 == 0 else 1)

    def write_out(i, o):
        o_ref[0, pl.ds(pl.multiple_of(i * L, L), L), :] = o

    _flash_sweep(
        (qi_ref, grp_ref), n_stages,
        q_tile=lambda i, hd: qh_ref[hd, pl.ds(pl.multiple_of(i * L, L), L), :],
        k_tile=lambda j: k_ref[0, pl.ds(pl.multiple_of(j * L, L), L), :],
        vt_tile=lambda j, hd: vt_ref[0, 0, j, pl.ds(hd * MOBA_HEAD_DIM, MOBA_HEAD_DIM), :],
        tile_bias=lambda i, hd, j, real: jnp.where(real, bias_ref[hd, i, pl.ds(j, 1), :], NEG),
        write_out=write_out, scratch=scratch)


def _moba(qa, ka, vat, kmean):
    B, S, _ = qa.shape
    L = MOBA_BLOCK
    nb = S // L
    qi, grp, n_stages = _stage_tables(nb)
    pair3 = lambda b, p, *_: (b, 0, p)
    grid_spec = pltpu.PrefetchScalarGridSpec(
        num_scalar_prefetch=2,
        grid=(B, N_PAIRS),
        in_specs=[
            pl.BlockSpec((1, S, LANES), pair3),
            pl.BlockSpec((1, S, LANES), pair3),
            pl.BlockSpec((1, 1, nb, LANES, L), lambda b, p, *_: (b, p, 0, 0, 0)),
            pl.BlockSpec((1, nb, 1, LANES), lambda b, p, *_: (b, 0, 0, p)),
        ],
        out_specs=pl.BlockSpec((1, S, LANES), pair3),
        scratch_shapes=[
            pltpu.VMEM((HEADS_PER_STEP, S, LANES), BF16),
            pltpu.VMEM((HEADS_PER_STEP, nb, nb, L), F32),
        ] + _attn_scratch(MOBA_HEAD_DIM, nb),
    )
    return pl.pallas_call(
        functools.partial(_moba_kernel, n_stages=n_stages),
        out_shape=jax.ShapeDtypeStruct((B, S, MOBA_WIDTH), F32),
        grid_spec=grid_spec,
        compiler_params=_cparams(("arbitrary", "arbitrary")),
        name="moba",
    )(jnp.asarray(qi), jnp.asarray(grp), qa, ka, vat, kmean)


def _mla_kernel(qi_ref, grp_ref, q_ref, k_ref, vt_ref, o_ref, *scratch, n_stages):
    L = MOBA_BLOCK

    def write_out(i, o):
        o_ref[0, pl.ds(pl.multiple_of(i * L, L), L), :] = o

    _flash_sweep(
        (qi_ref, grp_ref), n_stages,
        q_tile=lambda i, hd: q_ref[0, hd, pl.ds(pl.multiple_of(i * L, L), L), :],
        k_tile=lambda j: k_ref[0, 0, pl.ds(pl.multiple_of(j * L, L), L), :],
        vt_tile=lambda j, hd: vt_ref[0, 0, j, pl.ds(hd * MLA_V_DIM, MLA_V_DIM), :],
        tile_bias=lambda i, hd, j, real: jnp.where(real, 0.0, NEG),
        write_out=write_out, scratch=scratch)


def _mla(qcat, kcat, vmt):
    B, _, S, W = qcat.shape
    L = MOBA_BLOCK
    nb = S // L
    qi, grp, n_stages = _stage_tables(nb)
    grid_spec = pltpu.PrefetchScalarGridSpec(
        num_scalar_prefetch=2,
        grid=(B, N_PAIRS),
        in_specs=[
            pl.BlockSpec((1, HEADS_PER_STEP, S, W), lambda b, p, *_: (b, p, 0, 0)),
            pl.BlockSpec((1, 1, S, W), lambda b, p, *_: (b, p, 0, 0)),
            pl.BlockSpec((1, 1, nb, LANES, L), lambda b, p, *_: (b, p, 0, 0, 0)),
        ],
        out_specs=pl.BlockSpec((1, S, LANES), lambda b, p, *_: (b, 0, p)),
        scratch_shapes=_attn_scratch(MLA_V_DIM, nb),
    )
    return pl.pallas_call(
        functools.partial(_mla_kernel, n_stages=n_stages),
        out_shape=jax.ShapeDtypeStruct((B, S, MLA_WIDTH), F32),
        grid_spec=grid_spec,
        compiler_params=_cparams(("arbitrary", "arbitrary")),
        name="mla",
    )(jnp.asarray(qi), jnp.asarray(grp), qcat, kcat, vmt)


def _postattn_kernel(x_ref, om_ref, ol_ref, mod_ref, gom_ref, gol_ref, wo_ref, gpost_ref, gpre_ref,
                     wr_ref, br_ref, x1_ref, xl_ref, cnt_ref, ri_ref):
    tm = x_ref.shape[1]
    gt_a = mod_ref[0, 2:3, :]
    sh_f = mod_ref[0, 3:4, :]
    sc_f = mod_ref[0, 4:5, :]
    a = jnp.concatenate([_rms(om_ref[0], gom_ref[...]), _rms(ol_ref[0], gol_ref[...])], axis=-1)
    o = jnp.dot(a.astype(BF16), wo_ref[...], preferred_element_type=F32)
    x1 = x_ref[0] + gt_a * _rms(o, gpost_ref[...])
    x1_ref[0] = x1
    h2 = _rms(x1, gpre_ref[...]) * (1.0 + sc_f) + sh_f

    h2_hi = h2.astype(BF16)
    h2_lo = (h2 - h2_hi.astype(F32)).astype(BF16)
    logit = (jnp.dot(h2_hi, wr_ref[0], preferred_element_type=F32)
             + jnp.dot(h2_lo, wr_ref[0], preferred_element_type=F32)
             + jnp.dot(h2_hi, wr_ref[1], preferred_element_type=F32))
    biased = logit + br_ref[...]
    lane = lax.broadcasted_iota(jnp.int32, (tm, LANES), 1)
    big = jnp.int32(LANES)

    def first_argmax(v):
        mx = jnp.max(v, axis=-1, keepdims=True)
        return jnp.min(jnp.where(v == mx, lane, big), axis=-1, keepdims=True)

    def pick(v, idx):
        return jnp.sum(jnp.where(lane == idx, v, 0.0), axis=-1, keepdims=True)

    gmask = (lane >= N_EXPERTS) & (lane < N_EXPERTS + N_GROUPS)
    gl = jnp.where(gmask, logit, -jnp.inf)
    pg = jnp.exp(gl - jnp.max(gl, axis=-1, keepdims=True))
    g_lane = first_argmax(jnp.where(gmask, biased, -jnp.inf))
    g_prob = pick(pg, g_lane) / jnp.sum(pg, axis=-1, keepdims=True)
    e_lo = (g_lane - N_EXPERTS) * EXPERTS_PER_GROUP
    emask = (lane >= e_lo) & (lane < e_lo + EXPERTS_PER_GROUP)
    el = jnp.where(emask, logit, -jnp.inf)
    pe = jnp.exp(el - jnp.max(el, axis=-1, keepdims=True))
    eb = jnp.where(emask, biased, -jnp.inf)
    i1 = first_argmax(eb)
    i2 = first_argmax(jnp.where(lane == i1, -jnp.inf, eb))
    p1 = pick(pe, i1)
    p2 = pick(pe, i2)
    denom = p1 + p2
    w1 = p1 / denom * g_prob
    w2 = p2 / denom * g_prob

    hit1 = lane == i1
    hit2 = lane == i2
    onehot = jnp.where(hit1 | hit2, 1.0, 0.0).astype(BF16)
    tok_r = lax.broadcasted_iota(jnp.int32, (tm, tm), 0)
    tok_c = lax.broadcasted_iota(jnp.int32, (tm, tm), 1)
    csum = jnp.dot(jnp.where(tok_c <= tok_r, 1.0, 0.0).astype(BF16), onehot,
                   preferred_element_type=F32)
    cnt = csum[tm - 1:tm, :]
    seg = jnp.floor((cnt + (CHUNK - 1.0)) * (1.0 / CHUNK)) * CHUNK
    ex_r = lax.broadcasted_iota(jnp.int32, (LANES, LANES), 0)
    ex_c = lax.broadcasted_iota(jnp.int32, (LANES, LANES), 1)
    seg_start = jnp.dot(jnp.broadcast_to(seg, (8, LANES)).astype(BF16),
                        jnp.where(ex_r < ex_c, 1.0, 0.0).astype(BF16),
                        preferred_element_type=F32)[0:1, :]
    place = seg_start + csum - 1.0
    slot1 = jnp.sum(jnp.where(hit1, place, 0.0), axis=-1, keepdims=True)
    slot2 = jnp.sum(jnp.where(hit2, place, 0.0), axis=-1, keepdims=True)
    ri_ref[0] = jnp.where(lane == 0, i1, jnp.where(lane == 1, i2, jnp.where(
        lane == 2, slot1.astype(jnp.int32), jnp.where(lane == 3, slot2.astype(jnp.int32), 0))))
    cnt_ref[0, 0] = cnt

    packed = jnp.where(lane == 0, slot1, jnp.where(lane == 1, slot2, jnp.where(
        lane == 2, w1, jnp.where(lane == 3, w2, 0.0))))
    packed_t = packed.T
    srow = lax.broadcasted_iota(jnp.int32, (LOCAL_ROWS, tm), 0).astype(F32)
    at1 = srow == packed_t[0:1, :]
    at2 = srow == packed_t[1:2, :]
    perm = jnp.where(at1 | at2, 1.0, 0.0).astype(BF16)
    xl_ref[0, :, 0:D_MODEL] = jnp.dot(perm, h2_hi, preferred_element_type=F32)
    gate = jnp.sum(jnp.where(at1, packed_t[2:3, :], 0.0) + jnp.where(at2, packed_t[3:4, :], 0.0),
                   axis=-1, keepdims=True)
    xl_ref[0, :, D_MODEL:XL_WIDTH] = jnp.broadcast_to(gate, (LOCAL_ROWS, LANES))


def _postattn(x, o_moba, o_mla, mod3, g_om, g_ol, w_o, g_post, g_pre, w_r, b_r):
    B, S, D = x.shape
    tm = TM_IN
    assert S % tm == 0 and D == D_MODEL
    const2 = lambda b, i: (0, 0)
    tok3 = lambda b, i: (b, i, 0)
    n_j = S // tm
    return pl.pallas_call(
        _postattn_kernel,
        out_shape=[
            jax.ShapeDtypeStruct((B, S, D), F32),
            jax.ShapeDtypeStruct((B * n_j, LOCAL_ROWS, XL_WIDTH), F32),
            jax.ShapeDtypeStruct((B * n_j, 1, 1, LANES), F32),
            jax.ShapeDtypeStruct((B, S, LANES), jnp.int32),
        ],
        grid=(B, S // tm),
        in_specs=[
            pl.BlockSpec((1, tm, D), tok3),
            pl.BlockSpec((1, tm, MOBA_WIDTH), tok3),
            pl.BlockSpec((1, tm, MLA_WIDTH), tok3),
            pl.BlockSpec((1, 6, D), lambda b, i: (b, 0, 0)),
            pl.BlockSpec((1, MOBA_WIDTH), const2),
            pl.BlockSpec((1, MLA_WIDTH), const2),
            pl.BlockSpec(w_o.shape, const2),
            pl.BlockSpec((1, D), const2),
            pl.BlockSpec((1, D), const2),
            pl.BlockSpec(w_r.shape, lambda b, i: (0, 0, 0)),
            pl.BlockSpec((1, LANES), const2),
        ],
        out_specs=[
            pl.BlockSpec((1, tm, D), tok3),
            pl.BlockSpec((1, LOCAL_ROWS, XL_WIDTH), lambda b, i: (b * n_j + i, 0, 0)),
            pl.BlockSpec((1, 1, 1, LANES), lambda b, i: (b * n_j + i, 0, 0, 0)),
            pl.BlockSpec((1, tm, LANES), tok3),
        ],
        compiler_params=_cparams(("arbitrary", "arbitrary")),
        name="postattn",
    )(x, o_moba, o_mla, mod3, g_om, g_ol, w_o, g_post, g_pre, w_r, b_r)


def _start_chunk_gather(idx_ref, base, n_chunks, src_hbm, dst_ref, sem):
    for k in range(n_chunks):
        row = pl.multiple_of(idx_ref[base + k], CHUNK)
        pltpu.make_async_copy(src_hbm.at[pl.ds(row, CHUNK), :], dst_ref.at[pl.ds(k * CHUNK, CHUNK), :], sem).start()


def _wait_chunk_gather(n_chunks, src_hbm, dst_ref, sem):
    pltpu.make_async_copy(src_hbm.at[pl.ds(0, n_chunks * CHUNK), :], dst_ref, sem).wait()


def _moe_kernel(te_ref, nu_ref, src_ref, xl_hbm, wgu_ref, wd_ref, y_ref, xbuf, sem):
    tm = y_ref.shape[0]
    n_chunks = tm // CHUNK
    i = pl.program_id(0)
    n_used = nu_ref[0]
    slot = i % 2

    @pl.when((i == 0) & (n_used > 0))
    def _():
        _start_chunk_gather(src_ref, 0, n_chunks, xl_hbm, xbuf.at[0], sem.at[0])

    @pl.when(i + 1 < n_used)
    def _():
        _start_chunk_gather(src_ref, (i + 1) * n_chunks, n_chunks, xl_hbm, xbuf.at[1 - slot], sem.at[1 - slot])

    @pl.when(i < n_used)
    def _():
        _wait_chunk_gather(n_chunks, xl_hbm, xbuf.at[slot], sem.at[slot])
        xb = xbuf[slot, :, 0:D_MODEL].astype(BF16)
        gate = xbuf[slot, :, D_MODEL:XL_WIDTH]
        gu = jnp.dot(xb, wgu_ref[0], preferred_element_type=F32)
        g = gu[:, :D_EXPERT]
        u = gu[:, D_EXPERT:]
        hid = g * jax.nn.sigmoid(g) * u
        y = jnp.dot(hid.astype(BF16), wd_ref[0], preferred_element_type=F32)
        y_ref[...] = jnp.concatenate([gate] * (D_MODEL // LANES), axis=1) * y

    @pl.when(i >= n_used)
    def _():
        y_ref[...] = jnp.zeros_like(y_ref)


def _moe(tile_expert, n_used, chunk_src, x_local, w_gu, w_d):
    D = D_MODEL
    n_tiles = tile_expert.shape[0]
    tm = TM_MOE
    grid_spec = pltpu.PrefetchScalarGridSpec(
        num_scalar_prefetch=3,
        grid=(n_tiles,),
        in_specs=[
            pl.BlockSpec(memory_space=pl.ANY),
            pl.BlockSpec((1, D, 2 * D_EXPERT), lambda i, te, nu, src: (te[i], 0, 0)),
            pl.BlockSpec((1, D_EXPERT, D), lambda i, te, nu, src: (te[i], 0, 0)),
        ],
        out_specs=pl.BlockSpec((tm, D), lambda i, te, nu, src: (i, 0)),
        scratch_shapes=[pltpu.VMEM((2, tm, XL_WIDTH), F32), pltpu.SemaphoreType.DMA((2,))],
    )
    return pl.pallas_call(
        _moe_kernel,
        out_shape=jax.ShapeDtypeStruct((n_tiles * tm, D), F32),
        grid_spec=grid_spec,
        compiler_params=_cparams(("arbitrary",)),
        name="moe",
    )(tile_expert, n_used, chunk_src, x_local, w_gu, w_d)


def _combine_kernel(src_ref, y_hbm, x1_ref, ri_ref, mod_ref, g_ref, o_ref, ybuf, sem):
    tm = x1_ref.shape[1]
    n_chunks = LOCAL_ROWS // CHUNK
    n_j = pl.num_programs(1)
    step = pl.program_id(0) * n_j + pl.program_id(1)
    n_steps = pl.num_programs(0) * n_j
    slot = step % 2

    @pl.when(step == 0)
    def _():
        _start_chunk_gather(src_ref, 0, n_chunks, y_hbm, ybuf.at[0], sem.at[0])

    @pl.when(step + 1 < n_steps)
    def _():
        _start_chunk_gather(src_ref, (step + 1) * n_chunks, n_chunks, y_hbm, ybuf.at[1 - slot], sem.at[1 - slot])

    _wait_chunk_gather(n_chunks, y_hbm, ybuf.at[slot], sem.at[slot])
    yl = ybuf[slot].astype(BF16)
    col = lax.broadcasted_iota(jnp.int32, (tm, LOCAL_ROWS), 1)
    pick = jnp.where((col == ri_ref[0, :, 2:3]) | (col == ri_ref[0, :, 3:4]), 1.0, 0.0).astype(BF16)
    y = jnp.dot(pick, yl, preferred_element_type=F32)
    gt_f = mod_ref[0, 5:6, :]
    o_ref[0] = x1_ref[0] + gt_f * _rms(y, g_ref[...])


def _combine(chunk_src, y_sorted, x1, route_i, mod3, g_post):
    B, S, D = x1.shape
    tm = TM_IN
    grid_spec = pltpu.PrefetchScalarGridSpec(
        num_scalar_prefetch=1,
        grid=(B, S // tm),
        in_specs=[
            pl.BlockSpec(memory_space=pl.ANY),
            pl.BlockSpec((1, tm, D), lambda b, j, src: (b, j, 0)),
            pl.BlockSpec((1, tm, LANES), lambda b, j, src: (b, j, 0)),
            pl.BlockSpec((1, 6, D), lambda b, j, src: (b, 0, 0)),
            pl.BlockSpec((1, D), lambda b, j, src: (0, 0)),
        ],
        out_specs=pl.BlockSpec((1, tm, D), lambda b, j, src: (b, j, 0)),
        scratch_shapes=[pltpu.VMEM((2, LOCAL_ROWS, D), F32), pltpu.SemaphoreType.DMA((2,))],
    )
    return pl.pallas_call(
        _combine_kernel,
        out_shape=jax.ShapeDtypeStruct((B, S, D), F32),
        grid_spec=grid_spec,
        compiler_params=_cparams(("arbitrary", "arbitrary")),
        name="combine",
    )(chunk_src, y_sorted, x1, route_i, mod3, g_post)


def _rope_angles(positions):
    def cos_sin(dim):
        half = dim // 2
        inv_freq = jnp.exp(-math.log(ROPE_THETA) * jnp.arange(half, dtype=F32) * (2.0 / dim))
        ang = positions.astype(F32)[:, None, :] * inv_freq[None, :, None]
        return [jnp.cos(ang), jnp.sin(ang)]
    B, S = positions.shape
    parts = cos_sin(MOBA_ROT_DIMS) + cos_sin(MLA_ROPE_DIM) + [jnp.ones((B, 1, S), F32)]
    used = MOBA_ROT_DIMS + MLA_ROPE_DIM + 1
    return jnp.concatenate(parts + [jnp.zeros((B, LANES - used, S), F32)], axis=1)


def _routing_metadata(counts, tm):
    i32 = jnp.int32
    n_tt = counts.shape[0]
    n_tiles = (2 * n_tt * TM_IN + n_tt * N_EXPERTS * (CHUNK - 1)) // tm + N_EXPERTS + 1
    seg = (counts + CHUNK - 1) // CHUNK * CHUNK
    loc_start = jnp.cumsum(seg, axis=1) - seg
    loc_total = jnp.sum(seg, axis=1)
    run = jnp.cumsum(seg, axis=0)
    rows_e = run[-1]
    tiles_per = (rows_e + tm - 1) // tm
    tile_end = jnp.cumsum(tiles_per)
    n_used = tile_end[-1]
    first_row = (tile_end - tiles_per) * tm
    tile_ids = jnp.arange(n_tiles, dtype=i32)
    te = jnp.minimum(jnp.sum((tile_ids[:, None] >= tile_end[None, :]).astype(i32), axis=1), N_EXPERTS - 1)
    last_e = jnp.sum(jnp.where(tile_ids == n_used - 1, te, 0))
    tile_expert = jnp.where(tile_ids < n_used, te, last_e).astype(i32)

    cpt = tm // CHUNK
    is_e = (tile_expert[:, None] == jnp.arange(N_EXPERTS, dtype=i32)[None, :]).astype(i32)
    of_tile = lambda a: jnp.sum(is_e[:, None, :] * a[None, :, :], axis=2)
    run_t, seg_t, loc_t = of_tile(run), of_tile(seg), of_tile(loc_start)
    first_t = jnp.sum(is_e * first_row[None, :], axis=1)
    rows_t = jnp.sum(is_e * rows_e[None, :], axis=1)
    off = (tile_ids * tm - first_t)[:, None] + jnp.arange(cpt, dtype=i32)[None, :] * CHUNK
    tau = jnp.minimum(jnp.sum((off[:, :, None] >= run_t[:, None, :]).astype(i32), axis=2), n_tt - 1)
    is_tau = (tau[:, :, None] == jnp.arange(n_tt, dtype=i32)[None, None, :]).astype(i32)
    at_tau = lambda a_t: jnp.sum(is_tau * a_t[:, None, :], axis=2)
    src = tau * LOCAL_ROWS + at_tau(loc_t) + off - (at_tau(run_t) - at_tau(seg_t))
    real = (tile_ids < n_used)[:, None] & (off < rows_t[:, None])
    zero_local = LOCAL_ROWS - CHUNK
    moe_src = jnp.where(real, src, zero_local).astype(i32).reshape(-1)

    lrow = jnp.arange(LOCAL_ROWS // CHUNK, dtype=i32) * CHUNK
    loc_end = loc_start + seg
    e_v = jnp.minimum(jnp.sum((lrow[None, :, None] >= loc_end[:, None, :]).astype(i32), axis=2), N_EXPERTS - 1)
    is_ev = (e_v[:, :, None] == jnp.arange(N_EXPERTS, dtype=i32)[None, None, :]).astype(i32)
    seg_first = first_row[None, :] + run - seg
    src_v = jnp.sum(is_ev * (seg_first - loc_start)[:, None, :], axis=2) + lrow[None, :]
    zero_sorted = (n_tiles - 1) * tm
    cmb_src = jnp.where(lrow[None, :] < loc_total[:, None], src_v, zero_sorted).astype(i32).reshape(-1)
    return tile_expert, n_used.reshape(1).astype(i32), moe_src, cmb_src


def kernel(x, c, positions, w_ada, b_ada, g_pre_attn, g_post_attn, g_pre_ffn, g_post_ffn, w_in, g_cq, w_uq,
           g_ckv, w_ukv, g_out_moba, g_out_mla, w_o, w_group, b_group, w_expert, b_expert, w_e_gate, w_e_up,
           w_e_down):
    B, S, D = x.shape
    depth = w_ada.shape[0]
    rope_cs = _rope_angles(positions)
    rope_exp = jnp.asarray(_rope_expansion(), BF16)
    row = lambda v: v.reshape(1, -1)

    for l in range(depth):
        n_main = 3 * MOBA_WIDTH + MLA_Q_RANK + MLA_KV_RANK
        w_in_p = jnp.concatenate([w_in[l][:, :n_main], jnp.tile(w_in[l][:, n_main:], (1, LANES // MLA_ROPE_DIM))],
                                 axis=1).astype(BF16)
        wq = w_uq[l].reshape(MLA_Q_RANK, MLA_HEADS, MLA_NOPE_DIM + MLA_ROPE_DIM)
        wq_n = wq[:, :, :MLA_NOPE_DIM]
        zero_n = jnp.zeros_like(wq_n)
        even = (jnp.arange(MLA_HEADS) % 2 == 0)[None, :, None]
        w_uq_n = jnp.concatenate([jnp.where(even, wq_n, zero_n), jnp.where(even, zero_n, wq_n)], axis=-1)
        w_uq_n = w_uq_n.reshape(MLA_Q_RANK, MLA_HEADS * LANES).astype(BF16)
        w_uq_p = wq[:, :, MLA_NOPE_DIM:].reshape(MLA_Q_RANK, MLA_HEADS * MLA_ROPE_DIM).astype(BF16)
        wkv = w_ukv[l].reshape(MLA_KV_RANK, MLA_HEADS, MLA_NOPE_DIM + MLA_V_DIM)
        w_ukv_p = jnp.concatenate([wkv[:, :, :MLA_NOPE_DIM].reshape(MLA_KV_RANK, -1),
                                   wkv[:, :, MLA_NOPE_DIM:].reshape(MLA_KV_RANK, -1)], axis=1).astype(BF16)
        w_r = jnp.concatenate([w_expert[l], w_group[l],
                               jnp.zeros((D, LANES - N_EXPERTS - N_GROUPS), F32)], axis=1)
        w_r_hi = w_r.astype(BF16)
        w_r = jnp.stack([w_r_hi, (w_r - w_r_hi.astype(F32)).astype(BF16)])
        b_r = jnp.concatenate([b_expert[l], b_group[l], jnp.zeros((LANES - N_EXPERTS - N_GROUPS,), F32)]).reshape(1, LANES)
        w_gu = jnp.concatenate([w_e_gate[l], w_e_up[l]], axis=-1).astype(BF16)
        w_d = w_e_down[l].astype(BF16)

        mod3 = _adaln(c, w_ada[l], b_ada[l]).reshape(B, 6, D)

        qa, ka, vat, kmean, qcat, kcat, vmt = _inproj(
            x, mod3, row(g_pre_attn[l]), w_in_p, rope_cs, rope_exp, row(g_cq[l]), w_uq_n, w_uq_p,
            row(g_ckv[l]), w_ukv_p)
        o_moba = _moba(qa, ka, vat, kmean)
        o_mla = _mla(qcat, kcat, vmt)
        x1, x_local, counts, route_i = _postattn(
            x, o_moba, o_mla, mod3, row(g_out_moba[l]), row(g_out_mla[l]), w_o[l].astype(BF16),
            row(g_post_attn[l]), row(g_pre_ffn[l]), w_r, b_r)

        counts = counts.reshape(-1, LANES)[:, :N_EXPERTS].astype(jnp.int32)
        tile_expert, n_used, moe_src, cmb_src = _routing_metadata(counts, TM_MOE)
        y_sorted = _moe(tile_expert, n_used, moe_src, x_local.reshape(-1, XL_WIDTH), w_gu, w_d)
        x = _combine(cmb_src, y_sorted, x1, route_i, mod3, row(g_post_ffn[l]))
    return x
```

```python
import functools
import math

import jax
import jax.numpy as jnp
import numpy as np
from jax import lax
from jax.experimental import pallas as pl
from jax.experimental.pallas import tpu as pltpu

F32 = jnp.float32
BF16 = jnp.bfloat16

D_MODEL = 1024
MOBA_HEADS = 8
MOBA_HEAD_DIM = 64
MOBA_WIDTH = MOBA_HEADS * MOBA_HEAD_DIM
MOBA_BLOCK = 256
MOBA_TOPK = 3
MOBA_ROT_DIMS = MOBA_HEAD_DIM // 4
MLA_HEADS = 8
MLA_Q_RANK = 256
MLA_KV_RANK = 128
MLA_NOPE_DIM = 64
MLA_ROPE_DIM = 32
MLA_V_DIM = 64
MLA_WIDTH = MLA_HEADS * MLA_V_DIM
ROPE_THETA = 500000.0
N_GROUPS = 4
EXPERTS_PER_GROUP = 8
N_EXPERTS = N_GROUPS * EXPERTS_PER_GROUP
D_EXPERT = 256
EPS = 1e-6
NEG = -1e30
LOG2E = math.log2(math.e)

LANES = 128
HEADS_PER_STEP = LANES // MOBA_HEAD_DIM
N_PAIRS = MOBA_HEADS // HEADS_PER_STEP
VMEM_LIMIT = 56 * 1024 * 1024

TM_IN = 512
KV_GROUP = 4
STAGES_PER_TRIP = 4
TM_MOE = 256
CHUNK = 16
LOCAL_ROWS = 2 * TM_IN + N_EXPERTS * CHUNK


def _cparams(sem):
    return pltpu.CompilerParams(dimension_semantics=sem, vmem_limit_bytes=VMEM_LIMIT)


def _rms(x, g):
    return x * lax.rsqrt(jnp.mean(x * x, axis=-1, keepdims=True) + EPS) * g


def _adaln_kernel(c_ref, w_ref, b_ref, o_ref):
    c = c_ref[...]
    ca = c * jax.nn.sigmoid(c)
    o_ref[...] = jnp.dot(ca, w_ref[...], preferred_element_type=F32,
                         precision=lax.Precision.HIGHEST) + b_ref[...]


def _adaln(c, w, b):
    B, D = c.shape
    N = w.shape[1]
    tn = 1536 if N % 1536 == 0 else N
    return pl.pallas_call(
        _adaln_kernel,
        out_shape=jax.ShapeDtypeStruct((B, N), F32),
        grid=(N // tn,),
        in_specs=[pl.BlockSpec((B, D), lambda j: (0, 0)),
                  pl.BlockSpec((D, tn), lambda j: (0, j)),
                  pl.BlockSpec((1, tn), lambda j: (0, j))],
        out_specs=pl.BlockSpec((B, tn), lambda j: (0, j)),
        compiler_params=_cparams(("arbitrary",)),
        name="adaln",
    )(c, w, b.reshape(1, N))


def _rope128(xc, tab, shift):
    c = tab[:, 0:LANES]
    s1 = tab[:, LANES:2 * LANES]
    s2 = tab[:, 2 * LANES:3 * LANES]
    return xc * c + pltpu.roll(xc, LANES - shift, axis=1) * s1 + pltpu.roll(xc, shift, axis=1) * s2


def _rope_expansion():
    e = np.zeros((LANES, 6 * LANES), np.float32)
    ha, hm = MOBA_ROT_DIMS // 2, MLA_ROPE_DIM // 2
    cos_a, sin_a, cos_m, sin_m, one = 0, ha, 2 * ha, 2 * ha + hm, 2 * ha + 2 * hm
    for lane in range(LANES):
        d = lane % MOBA_HEAD_DIM
        if d < ha:
            e[cos_a + d, lane] = 1.0
            e[sin_a + d, LANES + lane] = -1.0
        elif d < 2 * ha:
            e[cos_a + d - ha, lane] = 1.0
            e[sin_a + d - ha, 2 * LANES + lane] = 1.0
        else:
            e[one, lane] = 1.0
        d = lane % MLA_ROPE_DIM
        e[cos_m + d % hm, 3 * LANES + lane] = 1.0
        if d < hm:
            e[sin_m + d, 4 * LANES + lane] = -1.0
        else:
            e[sin_m + d - hm, 5 * LANES + lane] = 1.0
    return e


def _expand_exact(v, e_ref):
    hi = v.astype(BF16)
    r1 = v - hi.astype(F32)
    mid = r1.astype(BF16)
    lo = (r1 - mid.astype(F32)).astype(BF16)
    e = e_ref[...]
    return (jnp.dot(hi, e, preferred_element_type=F32) + jnp.dot(mid, e, preferred_element_type=F32)
            + jnp.dot(lo, e, preferred_element_type=F32))


def _store_vt(src, col0, vt_ref):
    for p in range(N_PAIRS):
        for r in range(src.shape[0] // MOBA_BLOCK):
            blk = src[r * MOBA_BLOCK:(r + 1) * MOBA_BLOCK, col0 + p * LANES:col0 + (p + 1) * LANES]
            vt_ref[0, p, r] = blk.T.astype(BF16)


def _inproj_kernel(x_ref, mod_ref, gpre_ref, win_ref, cs_ref, rexp_ref, gcq_ref, wuqn_ref, wuqp_ref,
                   gckv_ref, wukv_ref,
                   qa_ref, ka_ref, vat_ref, kmean_ref, qcat_ref, kcat_ref, vmt_ref):
    tm = x_ref.shape[1]
    rope_tab = _expand_exact(cs_ref[0].T, rexp_ref)
    tab_a = rope_tab[:, 0:3 * LANES]
    tab_m = rope_tab[:, 3 * LANES:6 * LANES]
    x = x_ref[0]
    sh = mod_ref[0, 0:1, :]
    sc = mod_ref[0, 1:2, :]
    h = _rms(x, gpre_ref[...]) * (1.0 + sc) + sh
    z = jnp.dot(h.astype(BF16), win_ref[...], preferred_element_type=F32)

    moba_scale = MOBA_HEAD_DIM ** -0.5 * LOG2E
    half_a = MOBA_ROT_DIMS // 2
    n_chunks = MOBA_WIDTH // LANES
    for cidx in range(n_chunks):
        lo = cidx * LANES
        q = _rope128(z[:, lo:lo + LANES], tab_a, half_a)
        qa_ref[0, :, lo:lo + LANES] = (q * moba_scale).astype(BF16)
        k = _rope128(z[:, MOBA_WIDTH + lo:MOBA_WIDTH + lo + LANES], tab_a, half_a)
        ka_ref[0, :, lo:lo + LANES] = k.astype(BF16)
        for r in range(tm // MOBA_BLOCK):
            kmean_ref[0, r, :, lo:lo + LANES] = jnp.mean(
                k[r * MOBA_BLOCK:(r + 1) * MOBA_BLOCK], axis=0, keepdims=True)
    _store_vt(z, 2 * MOBA_WIDTH, vat_ref)

    o_cq = 3 * MOBA_WIDTH
    o_ckv = o_cq + MLA_Q_RANK
    o_kpe = o_ckv + MLA_KV_RANK
    mla_scale = (MLA_NOPE_DIM + MLA_ROPE_DIM) ** -0.5 * LOG2E
    half_m = MLA_ROPE_DIM // 2

    cqn = _rms(z[:, o_cq:o_cq + MLA_Q_RANK], gcq_ref[...]).astype(BF16)
    qn = jnp.dot(cqn, wuqn_ref[...], preferred_element_type=F32) * mla_scale
    qp = jnp.dot(cqn, wuqp_ref[...], preferred_element_type=F32)
    lane = lax.broadcasted_iota(jnp.int32, (tm, LANES), 1)
    heads_per_chunk = LANES // MLA_ROPE_DIM
    qp_chunks = [_rope128(qp[:, g * LANES:(g + 1) * LANES], tab_m, half_m) * mla_scale
                 for g in range(MLA_HEADS // heads_per_chunk)]
    for hd in range(MLA_HEADS):
        qcat_ref[0, hd, :, 0:LANES] = qn[:, hd * LANES:(hd + 1) * LANES].astype(BF16)
        sub = hd % heads_per_chunk
        keep = (lane >= sub * MLA_ROPE_DIM) & (lane < (sub + 1) * MLA_ROPE_DIM)
        qcat_ref[0, hd, :, LANES:2 * LANES] = jnp.where(keep, qp_chunks[hd // heads_per_chunk], 0.0).astype(BF16)

    ckvn = _rms(z[:, o_ckv:o_ckv + MLA_KV_RANK], gckv_ref[...]).astype(BF16)
    kv = jnp.dot(ckvn, wukv_ref[...], preferred_element_type=F32)
    kpe = _rope128(z[:, o_kpe:o_kpe + LANES], tab_m, half_m).astype(BF16)
    for p in range(N_PAIRS):
        kcat_ref[0, p, :, 0:LANES] = kv[:, p * LANES:(p + 1) * LANES].astype(BF16)
        kcat_ref[0, p, :, LANES:2 * LANES] = kpe
    _store_vt(kv, MLA_WIDTH, vmt_ref)


def _inproj(x, mod3, g_pre, w_in_p, rope_cs, rope_exp, g_cq, w_uq_n, w_uq_p, g_ckv, w_ukv_p):
    B, S, D = x.shape
    tm = min(TM_IN, S)
    nb = S // MOBA_BLOCK
    rb = tm // MOBA_BLOCK
    const2 = lambda b, i: (0, 0)
    tok3 = lambda b, i: (b, i, 0)
    outs = pl.pallas_call(
        _inproj_kernel,
        out_shape=[
            jax.ShapeDtypeStruct((B, S, MOBA_WIDTH), BF16),
            jax.ShapeDtypeStruct((B, S, MOBA_WIDTH), BF16),
            jax.ShapeDtypeStruct((B, N_PAIRS, nb, LANES, MOBA_BLOCK), BF16),
            jax.ShapeDtypeStruct((B, nb, 1, MOBA_WIDTH), F32),
            jax.ShapeDtypeStruct((B, MLA_HEADS, S, 2 * LANES), BF16),
            jax.ShapeDtypeStruct((B, N_PAIRS, S, 2 * LANES), BF16),
            jax.ShapeDtypeStruct((B, N_PAIRS, nb, LANES, MOBA_BLOCK), BF16),
        ],
        grid=(B, S // tm),
        in_specs=[
            pl.BlockSpec((1, tm, D), tok3),
            pl.BlockSpec((1, 6, D), lambda b, i: (b, 0, 0)),
            pl.BlockSpec((1, D), const2),
            pl.BlockSpec(w_in_p.shape, const2),
            pl.BlockSpec((1, LANES, tm), lambda b, i: (b, 0, i)),
            pl.BlockSpec(rope_exp.shape, const2),
            pl.BlockSpec((1, MLA_Q_RANK), const2),
            pl.BlockSpec(w_uq_n.shape, const2),
            pl.BlockSpec(w_uq_p.shape, const2),
            pl.BlockSpec((1, MLA_KV_RANK), const2),
            pl.BlockSpec(w_ukv_p.shape, const2),
        ],
        out_specs=[
            pl.BlockSpec((1, tm, MOBA_WIDTH), tok3),
            pl.BlockSpec((1, tm, MOBA_WIDTH), tok3),
            pl.BlockSpec((1, N_PAIRS, rb, LANES, MOBA_BLOCK), lambda b, i: (b, 0, i, 0, 0)),
            pl.BlockSpec((1, rb, 1, MOBA_WIDTH), lambda b, i: (b, i, 0, 0)),
            pl.BlockSpec((1, MLA_HEADS, tm, 2 * LANES), lambda b, i: (b, 0, i, 0)),
            pl.BlockSpec((1, N_PAIRS, tm, 2 * LANES), lambda b, i: (b, 0, i, 0)),
            pl.BlockSpec((1, N_PAIRS, rb, LANES, MOBA_BLOCK), lambda b, i: (b, 0, i, 0, 0)),
        ],
        compiler_params=_cparams(("arbitrary", "arbitrary")),
        name="inproj",
    )(x, mod3, g_pre, w_in_p, rope_cs, rope_exp, g_cq, w_uq_n, w_uq_p, g_ckv, w_ukv_p)
    return outs


def _nt_dot(a, b, **kw):
    return lax.dot_general(a, b, (((1,), (1,)), ((), ())), preferred_element_type=F32, **kw)


def _group_blocks(i, t):
    out = []
    for g in range(KV_GROUP):
        j = i - KV_GROUP * t - (KV_GROUP - 1 - g)
        out.append((jnp.maximum(j, 0), j >= 0))
    return out


def _stage_tables(nb):
    qi, grp = [], []
    for i in range(nb):
        for t in range((i + KV_GROUP) // KV_GROUP):
            qi.append(i)
            grp.append(t)
    n = len(qi)
    qi.append(qi[-1])
    grp.append(grp[-1])
    return np.asarray(qi, np.int32), np.asarray(grp, np.int32), n


def _flash_sweep(tables, n_stages, q_tile, k_tile, vt_tile, tile_bias, write_out, scratch):
    qi_ref, grp_ref = tables
    st_a, cm_a, st_b, cm_b, m_ref, l_ref, acc_ref, ot_ref = scratch
    L = MOBA_BLOCK
    key_minus_qry = (lax.broadcasted_iota(jnp.int32, (L, L), 0) - lax.broadcasted_iota(jnp.int32, (L, L), 1))

    def score_tile(i, blocks, g, hd, st_buf, cm_buf):
        j, _ = blocks[g]
        st = _nt_dot(k_tile(j), q_tile(i, hd))
        if g == KV_GROUP - 1:
            st = jnp.where(key_minus_qry <= (i - j) * L, st, NEG)
        st_buf[g, hd] = st
        cm_buf[g, hd] = jnp.max(st, axis=0, keepdims=True)

    def stage(s, cur, nxt):
        i, t = qi_ref[s], grp_ref[s]
        i_nxt = qi_ref[s + 1]
        blocks = _group_blocks(i, t)
        nxt_blocks = _group_blocks(i_nxt, grp_ref[s + 1])

        fresh = t == 0
        for hd in range(HEADS_PER_STEP):
            biases = [tile_bias(i, hd, j, real) for j, real in blocks]
            m_old = jnp.where(fresh, -jnp.inf, m_ref[hd])
            m_new = m_old
            for g in range(KV_GROUP):
                m_new = jnp.maximum(m_new, cur[1][g, hd] + biases[g])
            alpha = jnp.exp2(m_old - m_new)
            l_new = alpha * jnp.where(fresh, 0.0, l_ref[hd])
            pv = None
            for g, (j, _) in enumerate(blocks):
                score_tile(i_nxt, nxt_blocks, g, hd, *nxt)
                p = jnp.exp2(cur[0][g, hd] - (m_new - biases[g]))
                l_new = l_new + jnp.sum(p, axis=0, keepdims=True)
                d = jnp.dot(vt_tile(j, hd), p.astype(BF16), preferred_element_type=F32)
                pv = d if pv is None else pv + d
            acc_new = alpha * jnp.where(fresh, 0.0, acc_ref[hd]) + pv
            m_ref[hd] = m_new
            l_ref[hd] = l_new
            acc_ref[hd] = acc_new
            dv = acc_new.shape[0]
            ot_ref[i, hd * dv:(hd + 1) * dv, :] = acc_new / l_new

    bufs = ((st_a, cm_a), (st_b, cm_b))
    first = _group_blocks(qi_ref[0], grp_ref[0])
    for g in range(KV_GROUP):
        for hd in range(HEADS_PER_STEP):
            score_tile(qi_ref[0], first, g, hd, *bufs[0])

    def body(ss, carry):
        for k in range(STAGES_PER_TRIP):
            stage(STAGES_PER_TRIP * ss + k, bufs[k % 2], bufs[(k + 1) % 2])
        return carry

    n_trips = n_stages // STAGES_PER_TRIP
    lax.fori_loop(0, n_trips, body, 0)
    for s in range(n_trips * STAGES_PER_TRIP, n_stages):
        stage(s, bufs[s % 2], bufs[(s + 1) % 2])

    def emit(i, carry):
        write_out(i, ot_ref[i].T)
        return carry

    lax.fori_loop(0, ot_ref.shape[0], emit, 0)


def _attn_scratch(dv, nb):
    L = MOBA_BLOCK
    st = pltpu.VMEM((KV_GROUP, HEADS_PER_STEP, L, L), F32)
    cm = pltpu.VMEM((KV_GROUP, HEADS_PER_STEP, 1, L), F32)
    vec = pltpu.VMEM((HEADS_PER_STEP, 1, L), F32)
    return [st, cm, st, cm, vec, vec, pltpu.VMEM((HEADS_PER_STEP, dv, L), F32),
            pltpu.VMEM((nb, HEADS_PER_STEP * dv, L), F32)]


def _moba_kernel(qi_ref, grp_ref, q_ref, k_ref, vt_ref, kmean_ref, o_ref, qh_ref, bias_ref, *scratch,
                 n_stages):
    L = MOBA_BLOCK
    nb = kmean_ref.shape[1]
    lane = lax.broadcasted_iota(jnp.int32, (L, LANES), 1)
    kmean = kmean_ref[0, :, 0, :]
    blk = lax.broadcasted_iota(jnp.int32, (nb, L), 0)

    def select_blocks(i, carry):
        rows = pl.ds(pl.multiple_of(i * L, L), L)
        q = q_ref[0, rows, :]
        past = blk < i
        for hd in range(HEADS_PER_STEP):
            own = (lane >= hd * MOBA_HEAD_DIM) & (lane < (hd + 1) * MOBA_HEAD_DIM)
            qh = jnp.where(own, q, jnp.zeros_like(q))
            qh_ref[hd, rows, :] = qh
            gate = _nt_dot(kmean, qh.astype(F32), precision=lax.Precision.HIGHEST)
            gate = jnp.where(past, gate, -jnp.inf)
            bias = jnp.where(blk == i, 0.0, NEG)
            for _ in range(MOBA_TOPK):
                top = jnp.max(gate, axis=0, keepdims=True)
                at = blk == jnp.min(jnp.where(gate == top, blk, nb), axis=0, keepdims=True)
                bias = jnp.where(at & (top > -jnp.inf), 0.0, bias)
                gate = jnp.where(at, -jnp.inf, gate)
            bias_ref[hd, i] = bias
        return carry

    lax.fori_loop(0, nb, select_blocks, 0, unroll=4 if nb % 4 == 0 else 1)

    def write_out(i, o):
        o_ref[0, pl.ds(pl.multiple_of(i * L, L), L), :] = o

    _flash_sweep(
        (qi_ref, grp_ref), n_stages,
        q_tile=lambda i, hd: qh_ref[hd, pl.ds(pl.multiple_of(i * L, L), L), :],
        k_tile=lambda j: k_ref[0, pl.ds(pl.multiple_of(j * L, L), L), :],
        vt_tile=lambda j, hd: vt_ref[0, 0, j, pl.ds(hd * MOBA_HEAD_DIM, MOBA_HEAD_DIM), :],
        tile_bias=lambda i, hd, j, real: jnp.where(real, bias_ref[hd, i, pl.ds(j, 1), :], NEG),
        write_out=write_out, scratch=scratch)


def _moba(qa, ka, vat, kmean):
    B, S, _ = qa.shape
    L = MOBA_BLOCK
    nb = S // L
    qi, grp, n_stages = _stage_tables(nb)
    pair3 = lambda b, p, *_: (b, 0, p)
    grid_spec = pltpu.PrefetchScalarGridSpec(
        num_scalar_prefetch=2,
        grid=(B, N_PAIRS),
        in_specs=[
            pl.BlockSpec((1, S, LANES), pair3),
            pl.BlockSpec((1, S, LANES), pair3),
            pl.BlockSpec((1, 1, nb, LANES, L), lambda b, p, *_: (b, p, 0, 0, 0)),
            pl.BlockSpec((1, nb, 1, LANES), lambda b, p, *_: (b, 0, 0, p)),
        ],
        out_specs=pl.BlockSpec((1, S, LANES), pair3),
        scratch_shapes=[
            pltpu.VMEM((HEADS_PER_STEP, S, LANES), BF16),
            pltpu.VMEM((HEADS_PER_STEP, nb, nb, L), F32),
        ] + _attn_scratch(MOBA_HEAD_DIM, nb),
    )
    return pl.pallas_call(
        functools.partial(_moba_kernel, n_stages=n_stages),
        out_shape=jax.ShapeDtypeStruct((B, S, MOBA_WIDTH), F32),
        grid_spec=grid_spec,
        compiler_params=_cparams(("arbitrary", "arbitrary")),
        name="moba",
    )(jnp.asarray(qi), jnp.asarray(grp), qa, ka, vat, kmean)


def _mla_kernel(qi_ref, grp_ref, q_ref, k_ref, vt_ref, o_ref, *scratch, n_stages):
    L = MOBA_BLOCK

    def write_out(i, o):
        o_ref[0, pl.ds(pl.multiple_of(i * L, L), L), :] = o

    _flash_sweep(
        (qi_ref, grp_ref), n_stages,
        q_tile=lambda i, hd: q_ref[0, hd, pl.ds(pl.multiple_of(i * L, L), L), :],
        k_tile=lambda j: k_ref[0, 0, pl.ds(pl.multiple_of(j * L, L), L), :],
        vt_tile=lambda j, hd: vt_ref[0, 0, j, pl.ds(hd * MLA_V_DIM, MLA_V_DIM), :],
        tile_bias=lambda i, hd, j, real: jnp.where(real, 0.0, NEG),
        write_out=write_out, scratch=scratch)


def _mla(qcat, kcat, vmt):
    B, _, S, W = qcat.shape
    L = MOBA_BLOCK
    nb = S // L
    qi, grp, n_stages = _stage_tables(nb)
    grid_spec = pltpu.PrefetchScalarGridSpec(
        num_scalar_prefetch=2,
        grid=(B, N_PAIRS),
        in_specs=[
            pl.BlockSpec((1, HEADS_PER_STEP, S, W), lambda b, p, *_: (b, p, 0, 0)),
            pl.BlockSpec((1, 1, S, W), lambda b, p, *_: (b, p, 0, 0)),
            pl.BlockSpec((1, 1, nb, LANES, L), lambda b, p, *_: (b, p, 0, 0, 0)),
        ],
        out_specs=pl.BlockSpec((1, S, LANES), lambda b, p, *_: (b, 0, p)),
        scratch_shapes=_attn_scratch(MLA_V_DIM, nb),
    )
    return pl.pallas_call(
        functools.partial(_mla_kernel, n_stages=n_stages),
        out_shape=jax.ShapeDtypeStruct((B, S, MLA_WIDTH), F32),
        grid_spec=grid_spec,
        compiler_params=_cparams(("arbitrary", "arbitrary")),
        name="mla",
    )(jnp.asarray(qi), jnp.asarray(grp), qcat, kcat, vmt)


def _postattn_kernel(x_ref, om_ref, ol_ref, mod_ref, gom_ref, gol_ref, wo_ref, gpost_ref, gpre_ref,
                     wr_ref, br_ref, x1_ref, xl_ref, gl_ref, cnt_ref, ri_ref):
    tm = x_ref.shape[1]
    gt_a = mod_ref[0, 2:3, :]
    sh_f = mod_ref[0, 3:4, :]
    sc_f = mod_ref[0, 4:5, :]
    a = jnp.concatenate([_rms(om_ref[0], gom_ref[...]), _rms(ol_ref[0], gol_ref[...])], axis=-1)
    o = jnp.dot(a.astype(BF16), wo_ref[...], preferred_element_type=F32)
    x1 = x_ref[0] + gt_a * _rms(o, gpost_ref[...])
    x1_ref[0] = x1
    h2 = _rms(x1, gpre_ref[...]) * (1.0 + sc_f) + sh_f

    h2_hi = h2.astype(BF16)
    h2_lo = (h2 - h2_hi.astype(F32)).astype(BF16)
    logit = (jnp.dot(h2_hi, wr_ref[0], preferred_element_type=F32)
             + jnp.dot(h2_lo, wr_ref[0], preferred_element_type=F32)
             + jnp.dot(h2_hi, wr_ref[1], preferred_element_type=F32))
    biased = logit + br_ref[...]
    lane = lax.broadcasted_iota(jnp.int32, (tm, LANES), 1)
    big = jnp.int32(LANES)

    def first_argmax(v):
        mx = jnp.max(v, axis=-1, keepdims=True)
        return jnp.min(jnp.where(v == mx, lane, big), axis=-1, keepdims=True)

    def pick(v, idx):
        return jnp.sum(jnp.where(lane == idx, v, 0.0), axis=-1, keepdims=True)

    gmask = (lane >= N_EXPERTS) & (lane < N_EXPERTS + N_GROUPS)
    gl = jnp.where(gmask, logit, -jnp.inf)
    pg = jnp.exp(gl - jnp.max(gl, axis=-1, keepdims=True))
    g_lane = first_argmax(jnp.where(gmask, biased, -jnp.inf))
    g_prob = pick(pg, g_lane) / jnp.sum(pg, axis=-1, keepdims=True)
    e_lo = (g_lane - N_EXPERTS) * EXPERTS_PER_GROUP
    emask = (lane >= e_lo) & (lane < e_lo + EXPERTS_PER_GROUP)
    el = jnp.where(emask, logit, -jnp.inf)
    pe = jnp.exp(el - jnp.max(el, axis=-1, keepdims=True))
    eb = jnp.where(emask, biased, -jnp.inf)
    i1 = first_argmax(eb)
    i2 = first_argmax(jnp.where(lane == i1, -jnp.inf, eb))
    p1 = pick(pe, i1)
    p2 = pick(pe, i2)
    denom = p1 + p2
    w1 = p1 / denom * g_prob
    w2 = p2 / denom * g_prob

    hit1 = lane == i1
    hit2 = lane == i2
    onehot = jnp.where(hit1 | hit2, 1.0, 0.0).astype(BF16)
    tok_r = lax.broadcasted_iota(jnp.int32, (tm, tm), 0)
    tok_c = lax.broadcasted_iota(jnp.int32, (tm, tm), 1)
    csum = jnp.dot(jnp.where(tok_c <= tok_r, 1.0, 0.0).astype(BF16), onehot,
                   preferred_element_type=F32)
    cnt = csum[tm - 1:tm, :]
    seg = jnp.floor((cnt + (CHUNK - 1.0)) * (1.0 / CHUNK)) * CHUNK
    ex_r = lax.broadcasted_iota(jnp.int32, (LANES, LANES), 0)
    ex_c = lax.broadcasted_iota(jnp.int32, (LANES, LANES), 1)
    seg_start = jnp.dot(jnp.broadcast_to(seg, (8, LANES)).astype(BF16),
                        jnp.where(ex_r < ex_c, 1.0, 0.0).astype(BF16),
                        preferred_element_type=F32)[0:1, :]
    place = seg_start + csum - 1.0
    slot1 = jnp.sum(jnp.where(hit1, place, 0.0), axis=-1, keepdims=True)
    slot2 = jnp.sum(jnp.where(hit2, place, 0.0), axis=-1, keepdims=True)
    ri_ref[0] = jnp.where(lane == 0, i1, jnp.where(lane == 1, i2, jnp.where(
        lane == 2, slot1.astype(jnp.int32), jnp.where(lane == 3, slot2.astype(jnp.int32), 0))))
    cnt_ref[0, 0] = cnt

    packed = jnp.where(lane == 0, slot1, jnp.where(lane == 1, slot2, jnp.where(
        lane == 2, w1, jnp.where(lane == 3, w2, 0.0))))
    packed_t = packed.T
    srow = lax.broadcasted_iota(jnp.int32, (LOCAL_ROWS, tm), 0).astype(F32)
    at1 = srow == packed_t[0:1, :]
    at2 = srow == packed_t[1:2, :]
    perm = jnp.where(at1 | at2, 1.0, 0.0).astype(BF16)
    xl_ref[0] = jnp.dot(perm, h2_hi, preferred_element_type=F32).astype(BF16)
    gate = jnp.sum(jnp.where(at1, packed_t[2:3, :], 0.0) + jnp.where(at2, packed_t[3:4, :], 0.0),
                   axis=-1, keepdims=True)
    gl_ref[0] = jnp.broadcast_to(gate, (LOCAL_ROWS, LANES))


def _postattn(x, o_moba, o_mla, mod3, g_om, g_ol, w_o, g_post, g_pre, w_r, b_r):
    B, S, D = x.shape
    tm = TM_IN
    assert S % tm == 0 and D == D_MODEL
    const2 = lambda b, i: (0, 0)
    tok3 = lambda b, i: (b, i, 0)
    n_j = S // tm
    return pl.pallas_call(
        _postattn_kernel,
        out_shape=[
            jax.ShapeDtypeStruct((B, S, D), F32),
            jax.ShapeDtypeStruct((B * n_j, LOCAL_ROWS, D), BF16),
            jax.ShapeDtypeStruct((B * n_j, LOCAL_ROWS, LANES), F32),
            jax.ShapeDtypeStruct((B * n_j, 1, 1, LANES), F32),
            jax.ShapeDtypeStruct((B, S, LANES), jnp.int32),
        ],
        grid=(B, S // tm),
        in_specs=[
            pl.BlockSpec((1, tm, D), tok3),
            pl.BlockSpec((1, tm, MOBA_WIDTH), tok3),
            pl.BlockSpec((1, tm, MLA_WIDTH), tok3),
            pl.BlockSpec((1, 6, D), lambda b, i: (b, 0, 0)),
            pl.BlockSpec((1, MOBA_WIDTH), const2),
            pl.BlockSpec((1, MLA_WIDTH), const2),
            pl.BlockSpec(w_o.shape, const2),
            pl.BlockSpec((1, D), const2),
            pl.BlockSpec((1, D), const2),
            pl.BlockSpec(w_r.shape, lambda b, i: (0, 0, 0)),
            pl.BlockSpec((1, LANES), const2),
        ],
        out_specs=[
            pl.BlockSpec((1, tm, D), tok3),
            pl.BlockSpec((1, LOCAL_ROWS, D), lambda b, i: (b * n_j + i, 0, 0)),
            pl.BlockSpec((1, LOCAL_ROWS, LANES), lambda b, i: (b * n_j + i, 0, 0)),
            pl.BlockSpec((1, 1, 1, LANES), lambda b, i: (b * n_j + i, 0, 0, 0)),
            pl.BlockSpec((1, tm, LANES), tok3),
        ],
        compiler_params=_cparams(("arbitrary", "arbitrary")),
        name="postattn",
    )(x, o_moba, o_mla, mod3, g_om, g_ol, w_o, g_post, g_pre, w_r, b_r)


def _start_chunk_gather(idx_ref, base, n_chunks, src_hbm, dst_ref, sem):
    for k in range(n_chunks):
        row = pl.multiple_of(idx_ref[base + k], CHUNK)
        pltpu.make_async_copy(src_hbm.at[pl.ds(row, CHUNK), :], dst_ref.at[pl.ds(k * CHUNK, CHUNK), :], sem).start()


def _wait_chunk_gather(n_chunks, src_hbm, dst_ref, sem):
    pltpu.make_async_copy(src_hbm.at[pl.ds(0, n_chunks * CHUNK), :], dst_ref, sem).wait()


def _moe_kernel(te_ref, nu_ref, src_ref, xl_hbm, gl_hbm, wgu_ref, wd_ref, y_ref, xbuf, gbuf, xsem, gsem):
    tm = y_ref.shape[0]
    n_chunks = tm // CHUNK
    i = pl.program_id(0)
    n_used = nu_ref[0]
    slot = i % 2

    def start(tile, s):
        _start_chunk_gather(src_ref, tile * n_chunks, n_chunks, xl_hbm, xbuf.at[s], xsem.at[s])
        _start_chunk_gather(src_ref, tile * n_chunks, n_chunks, gl_hbm, gbuf.at[s], gsem.at[s])

    @pl.when((i == 0) & (n_used > 0))
    def _():
        start(0, 0)

    @pl.when(i + 1 < n_used)
    def _():
        start(i + 1, 1 - slot)

    @pl.when(i < n_used)
    def _():
        _wait_chunk_gather(n_chunks, xl_hbm, xbuf.at[slot], xsem.at[slot])
        _wait_chunk_gather(n_chunks, gl_hbm, gbuf.at[slot], gsem.at[slot])
        gate = gbuf[slot]
        gu = jnp.dot(xbuf[slot], wgu_ref[0], preferred_element_type=F32)
        g = gu[:, :D_EXPERT]
        u = gu[:, D_EXPERT:]
        hid = g * jax.nn.sigmoid(g) * u
        y = jnp.dot(hid.astype(BF16), wd_ref[0], preferred_element_type=F32)
        y_ref[...] = (jnp.concatenate([gate] * (D_MODEL // LANES), axis=1) * y).astype(BF16)

    @pl.when(i >= n_used)
    def _():
        y_ref[...] = jnp.zeros_like(y_ref)


def _moe(tile_expert, n_used, chunk_src, x_local, g_local, w_gu, w_d):
    D = D_MODEL
    n_tiles = tile_expert.shape[0]
    tm = TM_MOE
    grid_spec = pltpu.PrefetchScalarGridSpec(
        num_scalar_prefetch=3,
        grid=(n_tiles,),
        in_specs=[
            pl.BlockSpec(memory_space=pl.ANY),
            pl.BlockSpec(memory_space=pl.ANY),
            pl.BlockSpec((1, D, 2 * D_EXPERT), lambda i, te, nu, src: (te[i], 0, 0)),
            pl.BlockSpec((1, D_EXPERT, D), lambda i, te, nu, src: (te[i], 0, 0)),
        ],
        out_specs=pl.BlockSpec((tm, D), lambda i, te, nu, src: (i, 0)),
        scratch_shapes=[pltpu.VMEM((2, tm, D), BF16), pltpu.VMEM((2, tm, LANES), F32),
                        pltpu.SemaphoreType.DMA((2,)), pltpu.SemaphoreType.DMA((2,))],
    )
    return pl.pallas_call(
        _moe_kernel,
        out_shape=jax.ShapeDtypeStruct((n_tiles * tm, D), BF16),
        grid_spec=grid_spec,
        compiler_params=_cparams(("arbitrary",)),
        name="moe",
    )(tile_expert, n_used, chunk_src, x_local, g_local, w_gu, w_d)


def _combine_kernel(src_ref, y_hbm, x1_ref, ri_ref, mod_ref, g_ref, o_ref, ybuf, sem):
    tm = x1_ref.shape[1]
    n_chunks = LOCAL_ROWS // CHUNK
    n_j = pl.num_programs(1)
    step = pl.program_id(0) * n_j + pl.program_id(1)
    n_steps = pl.num_programs(0) * n_j
    slot = step % 2

    @pl.when(step == 0)
    def _():
        _start_chunk_gather(src_ref, 0, n_chunks, y_hbm, ybuf.at[0], sem.at[0])

    @pl.when(step + 1 < n_steps)
    def _():
        _start_chunk_gather(src_ref, (step + 1) * n_chunks, n_chunks, y_hbm, ybuf.at[1 - slot], sem.at[1 - slot])

    _wait_chunk_gather(n_chunks, y_hbm, ybuf.at[slot], sem.at[slot])
    yl = ybuf[slot]
    col = lax.broadcasted_iota(jnp.int32, (tm, LOCAL_ROWS), 1)
    pick = jnp.where((col == ri_ref[0, :, 2:3]) | (col == ri_ref[0, :, 3:4]), 1.0, 0.0).astype(BF16)
    y = jnp.dot(pick, yl, preferred_element_type=F32)
    gt_f = mod_ref[0, 5:6, :]
    o_ref[0] = x1_ref[0] + gt_f * _rms(y, g_ref[...])


def _combine(chunk_src, y_sorted, x1, route_i, mod3, g_post):
    B, S, D = x1.shape
    tm = TM_IN
    grid_spec = pltpu.PrefetchScalarGridSpec(
        num_scalar_prefetch=1,
        grid=(B, S // tm),
        in_specs=[
            pl.BlockSpec(memory_space=pl.ANY),
            pl.BlockSpec((1, tm, D), lambda b, j, src: (b, j, 0)),
            pl.BlockSpec((1, tm, LANES), lambda b, j, src: (b, j, 0)),
            pl.BlockSpec((1, 6, D), lambda b, j, src: (b, 0, 0)),
            pl.BlockSpec((1, D), lambda b, j, src: (0, 0)),
        ],
        out_specs=pl.BlockSpec((1, tm, D), lambda b, j, src: (b, j, 0)),
        scratch_shapes=[pltpu.VMEM((2, LOCAL_ROWS, D), BF16), pltpu.SemaphoreType.DMA((2,))],
    )
    return pl.pallas_call(
        _combine_kernel,
        out_shape=jax.ShapeDtypeStruct((B, S, D), F32),
        grid_spec=grid_spec,
        compiler_params=_cparams(("arbitrary", "arbitrary")),
        name="combine",
    )(chunk_src, y_sorted, x1, route_i, mod3, g_post)


def _rope_angles(positions):
    def cos_sin(dim):
        half = dim // 2
        inv_freq = jnp.exp(-math.log(ROPE_THETA) * jnp.arange(half, dtype=F32) * (2.0 / dim))
        ang = positions.astype(F32)[:, None, :] * inv_freq[None, :, None]
        return [jnp.cos(ang), jnp.sin(ang)]
    B, S = positions.shape
    parts = cos_sin(MOBA_ROT_DIMS) + cos_sin(MLA_ROPE_DIM) + [jnp.ones((B, 1, S), F32)]
    used = MOBA_ROT_DIMS + MLA_ROPE_DIM + 1
    return jnp.concatenate(parts + [jnp.zeros((B, LANES - used, S), F32)], axis=1)


def _routing_metadata(counts, tm):
    i32 = jnp.int32
    n_tt = counts.shape[0]
    n_tiles = (2 * n_tt * TM_IN + n_tt * N_EXPERTS * (CHUNK - 1)) // tm + N_EXPERTS + 1
    seg = (counts + CHUNK - 1) // CHUNK * CHUNK
    loc_start = jnp.cumsum(seg, axis=1) - seg
    loc_total = jnp.sum(seg, axis=1)
    run = jnp.cumsum(seg, axis=0)
    rows_e = run[-1]
    tiles_per = (rows_e + tm - 1) // tm
    tile_end = jnp.cumsum(tiles_per)
    n_used = tile_end[-1]
    first_row = (tile_end - tiles_per) * tm
    tile_ids = jnp.arange(n_tiles, dtype=i32)
    te = jnp.minimum(jnp.sum((tile_ids[:, None] >= tile_end[None, :]).astype(i32), axis=1), N_EXPERTS - 1)
    last_e = jnp.sum(jnp.where(tile_ids == n_used - 1, te, 0))
    tile_expert = jnp.where(tile_ids < n_used, te, last_e).astype(i32)

    cpt = tm // CHUNK
    is_e = (tile_expert[:, None] == jnp.arange(N_EXPERTS, dtype=i32)[None, :]).astype(i32)
    of_tile = lambda a: jnp.sum(is_e[:, None, :] * a[None, :, :], axis=2)
    run_t, seg_t, loc_t = of_tile(run), of_tile(seg), of_tile(loc_start)
    first_t = jnp.sum(is_e * first_row[None, :], axis=1)
    rows_t = jnp.sum(is_e * rows_e[None, :], axis=1)
    off = (tile_ids * tm - first_t)[:, None] + jnp.arange(cpt, dtype=i32)[None, :] * CHUNK
    tau = jnp.minimum(jnp.sum((off[:, :, None] >= run_t[:, None, :]).astype(i32), axis=2), n_tt - 1)
    is_tau = (tau[:, :, None] == jnp.arange(n_tt, dtype=i32)[None, None, :]).astype(i32)
    at_tau = lambda a_t: jnp.sum(is_tau * a_t[:, None, :], axis=2)
    src = tau * LOCAL_ROWS + at_tau(loc_t) + off - (at_tau(run_t) - at_tau(seg_t))
    real = (tile_ids < n_used)[:, None] & (off < rows_t[:, None])
    zero_local = LOCAL_ROWS - CHUNK
    moe_src = jnp.where(real, src, zero_local).astype(i32).reshape(-1)

    lrow = jnp.arange(LOCAL_ROWS // CHUNK, dtype=i32) * CHUNK
    loc_end = loc_start + seg
    e_v = jnp.minimum(jnp.sum((lrow[None, :, None] >= loc_end[:, None, :]).astype(i32), axis=2), N_EXPERTS - 1)
    is_ev = (e_v[:, :, None] == jnp.arange(N_EXPERTS, dtype=i32)[None, None, :]).astype(i32)
    seg_first = first_row[None, :] + run - seg
    src_v = jnp.sum(is_ev * (seg_first - loc_start)[:, None, :], axis=2) + lrow[None, :]
    zero_sorted = (n_tiles - 1) * tm
    cmb_src = jnp.where(lrow[None, :] < loc_total[:, None], src_v, zero_sorted).astype(i32).reshape(-1)
    return tile_expert, n_used.reshape(1).astype(i32), moe_src, cmb_src


def kernel(x, c, positions, w_ada, b_ada, g_pre_attn, g_post_attn, g_pre_ffn, g_post_ffn, w_in, g_cq, w_uq,
           g_ckv, w_ukv, g_out_moba, g_out_mla, w_o, w_group, b_group, w_expert, b_expert, w_e_gate, w_e_up,
           w_e_down):
    B, S, D = x.shape
    depth = w_ada.shape[0]
    rope_cs = _rope_angles(positions)
    rope_exp = jnp.asarray(_rope_expansion(), BF16)
    row = lambda v: v.reshape(1, -1)

    for l in range(depth):
        n_main = 3 * MOBA_WIDTH + MLA_Q_RANK + MLA_KV_RANK
        w_in_p = jnp.concatenate([w_in[l][:, :n_main], jnp.tile(w_in[l][:, n_main:], (1, LANES // MLA_ROPE_DIM))],
                                 axis=1).astype(BF16)
        wq = w_uq[l].reshape(MLA_Q_RANK, MLA_HEADS, MLA_NOPE_DIM + MLA_ROPE_DIM)
        wq_n = wq[:, :, :MLA_NOPE_DIM]
        zero_n = jnp.zeros_like(wq_n)
        even = (jnp.arange(MLA_HEADS) % 2 == 0)[None, :, None]
        w_uq_n = jnp.concatenate([jnp.where(even, wq_n, zero_n), jnp.where(even, zero_n, wq_n)], axis=-1)
        w_uq_n = w_uq_n.reshape(MLA_Q_RANK, MLA_HEADS * LANES).astype(BF16)
        w_uq_p = wq[:, :, MLA_NOPE_DIM:].reshape(MLA_Q_RANK, MLA_HEADS * MLA_ROPE_DIM).astype(BF16)
        wkv = w_ukv[l].reshape(MLA_KV_RANK, MLA_HEADS, MLA_NOPE_DIM + MLA_V_DIM)
        w_ukv_p = jnp.concatenate([wkv[:, :, :MLA_NOPE_DIM].reshape(MLA_KV_RANK, -1),
                                   wkv[:, :, MLA_NOPE_DIM:].reshape(MLA_KV_RANK, -1)], axis=1).astype(BF16)
        w_r = jnp.concatenate([w_expert[l], w_group[l],
                               jnp.zeros((D, LANES - N_EXPERTS - N_GROUPS), F32)], axis=1)
        w_r_hi = w_r.astype(BF16)
        w_r = jnp.stack([w_r_hi, (w_r - w_r_hi.astype(F32)).astype(BF16)])
        b_r = jnp.concatenate([b_expert[l], b_group[l], jnp.zeros((LANES - N_EXPERTS - N_GROUPS,), F32)]).reshape(1, LANES)
        w_gu = jnp.concatenate([w_e_gate[l], w_e_up[l]], axis=-1).astype(BF16)
        w_d = w_e_down[l].astype(BF16)

        mod3 = _adaln(c, w_ada[l], b_ada[l]).reshape(B, 6, D)

        qa, ka, vat, kmean, qcat, kcat, vmt = _inproj(
            x, mod3, row(g_pre_attn[l]), w_in_p, rope_cs, rope_exp, row(g_cq[l]), w_uq_n, w_uq_p,
            row(g_ckv[l]), w_ukv_p)
        o_moba = _moba(qa, ka, vat, kmean)
        o_mla = _mla(qcat, kcat, vmt)
        x1, x_local, g_local, counts, route_i = _postattn(
            x, o_moba, o_mla, mod3, row(g_out_moba[l]), row(g_out_mla[l]), w_o[l].astype(BF16),
            row(g_post_attn[l]), row(g_pre_ffn[l]), w_r, b_r)

        counts = counts.reshape(-1, LANES)[:, :N_EXPERTS].astype(jnp.int32)
        tile_expert, n_used, moe_src, cmb_src = _routing_metadata(counts, TM_MOE)
        y_sorted = _moe(tile_expert, n_used, moe_src, x_local.reshape(-1, D), g_local.reshape(-1, LANES), w_gu, w_d)
        x = _combine(cmb_src, y_sorted, x1, route_i, mod3, row(g_post_ffn[l]))
    return x
```

```python
import functools
import math

import jax
import jax.numpy as jnp
import numpy as np
from jax import lax
from jax.experimental import pallas as pl
from jax.experimental.pallas import tpu as pltpu

F32 = jnp.float32
BF16 = jnp.bfloat16

D_MODEL = 1024
MOBA_HEADS = 8
MOBA_HEAD_DIM = 64
MOBA_WIDTH = MOBA_HEADS * MOBA_HEAD_DIM
MOBA_BLOCK = 256
MOBA_TOPK = 3
MOBA_ROT_DIMS = MOBA_HEAD_DIM // 4
MLA_HEADS = 8
MLA_Q_RANK = 256
MLA_KV_RANK = 128
MLA_NOPE_DIM = 64
MLA_ROPE_DIM = 32
MLA_V_DIM = 64
MLA_WIDTH = MLA_HEADS * MLA_V_DIM
ROPE_THETA = 500000.0
N_GROUPS = 4
EXPERTS_PER_GROUP = 8
N_EXPERTS = N_GROUPS * EXPERTS_PER_GROUP
D_EXPERT = 256
EPS = 1e-6
NEG = -1e30
LOG2E = math.log2(math.e)

LANES = 128
HEADS_PER_STEP = LANES // MOBA_HEAD_DIM
N_PAIRS = MOBA_HEADS // HEADS_PER_STEP
VMEM_LIMIT = 56 * 1024 * 1024

TM_IN = 512
KV_GROUP = 4
STAGES_PER_TRIP = 4
TM_MOE = 512
CHUNK = 16
LOCAL_ROWS = 2 * TM_IN + N_EXPERTS * CHUNK


def _cparams(sem):
    return pltpu.CompilerParams(dimension_semantics=sem, vmem_limit_bytes=VMEM_LIMIT)


def _rms(x, g):
    return x * lax.rsqrt(jnp.mean(x * x, axis=-1, keepdims=True) + EPS) * g


def _adaln_kernel(c_ref, w_ref, b_ref, o_ref):
    c = c_ref[...]
    ca = c * jax.nn.sigmoid(c)
    o_ref[...] = jnp.dot(ca, w_ref[...], preferred_element_type=F32,
                         precision=lax.Precision.HIGHEST) + b_ref[...]


def _adaln(c, w, b):
    B, D = c.shape
    N = w.shape[1]
    tn = 1536 if N % 1536 == 0 else N
    return pl.pallas_call(
        _adaln_kernel,
        out_shape=jax.ShapeDtypeStruct((B, N), F32),
        grid=(N // tn,),
        in_specs=[pl.BlockSpec((B, D), lambda j: (0, 0)),
                  pl.BlockSpec((D, tn), lambda j: (0, j)),
                  pl.BlockSpec((1, tn), lambda j: (0, j))],
        out_specs=pl.BlockSpec((B, tn), lambda j: (0, j)),
        compiler_params=_cparams(("arbitrary",)),
        name="adaln",
    )(c, w, b.reshape(1, N))


def _rope128(xc, tab, shift):
    c = tab[:, 0:LANES]
    s1 = tab[:, LANES:2 * LANES]
    s2 = tab[:, 2 * LANES:3 * LANES]
    return xc * c + pltpu.roll(xc, LANES - shift, axis=1) * s1 + pltpu.roll(xc, shift, axis=1) * s2


def _rope_expansion():
    e = np.zeros((LANES, 6 * LANES), np.float32)
    ha, hm = MOBA_ROT_DIMS // 2, MLA_ROPE_DIM // 2
    cos_a, sin_a, cos_m, sin_m, one = 0, ha, 2 * ha, 2 * ha + hm, 2 * ha + 2 * hm
    for lane in range(LANES):
        d = lane % MOBA_HEAD_DIM
        if d < ha:
            e[cos_a + d, lane] = 1.0
            e[sin_a + d, LANES + lane] = -1.0
        elif d < 2 * ha:
            e[cos_a + d - ha, lane] = 1.0
            e[sin_a + d - ha, 2 * LANES + lane] = 1.0
        else:
            e[one, lane] = 1.0
        d = lane % MLA_ROPE_DIM
        e[cos_m + d % hm, 3 * LANES + lane] = 1.0
        if d < hm:
            e[sin_m + d, 4 * LANES + lane] = -1.0
        else:
            e[sin_m + d - hm, 5 * LANES + lane] = 1.0
    return e


def _expand_exact(v, e_ref):
    hi = v.astype(BF16)
    r1 = v - hi.astype(F32)
    mid = r1.astype(BF16)
    lo = (r1 - mid.astype(F32)).astype(BF16)
    e = e_ref[...]
    return (jnp.dot(hi, e, preferred_element_type=F32) + jnp.dot(mid, e, preferred_element_type=F32)
            + jnp.dot(lo, e, preferred_element_type=F32))


def _store_vt(src, col0, vt_ref):
    for p in range(N_PAIRS):
        for r in range(src.shape[0] // MOBA_BLOCK):
            blk = src[r * MOBA_BLOCK:(r + 1) * MOBA_BLOCK, col0 + p * LANES:col0 + (p + 1) * LANES]
            vt_ref[0, p, r] = blk.T.astype(BF16)


def _inproj_kernel(x_ref, mod_ref, gpre_ref, win_ref, cs_ref, rexp_ref, gcq_ref, wuqn_ref, wuqp_ref,
                   gckv_ref, wukv_ref,
                   qa_ref, ka_ref, vat_ref, kmean_ref, qcat_ref, kcat_ref, vmt_ref):
    tm = x_ref.shape[1]
    rope_tab = _expand_exact(cs_ref[0].T, rexp_ref)
    tab_a = rope_tab[:, 0:3 * LANES]
    tab_m = rope_tab[:, 3 * LANES:6 * LANES]
    x = x_ref[0]
    sh = mod_ref[0, 0:1, :]
    sc = mod_ref[0, 1:2, :]
    h = _rms(x, gpre_ref[...]) * (1.0 + sc) + sh
    z = jnp.dot(h.astype(BF16), win_ref[...], preferred_element_type=F32)

    moba_scale = MOBA_HEAD_DIM ** -0.5 * LOG2E
    half_a = MOBA_ROT_DIMS // 2
    n_chunks = MOBA_WIDTH // LANES
    for cidx in range(n_chunks):
        lo = cidx * LANES
        q = _rope128(z[:, lo:lo + LANES], tab_a, half_a)
        qa_ref[0, :, lo:lo + LANES] = (q * moba_scale).astype(BF16)
        k = _rope128(z[:, MOBA_WIDTH + lo:MOBA_WIDTH + lo + LANES], tab_a, half_a)
        ka_ref[0, :, lo:lo + LANES] = k.astype(BF16)
        for r in range(tm // MOBA_BLOCK):
            kmean_ref[0, r, :, lo:lo + LANES] = jnp.mean(
                k[r * MOBA_BLOCK:(r + 1) * MOBA_BLOCK], axis=0, keepdims=True)
    _store_vt(z, 2 * MOBA_WIDTH, vat_ref)

    o_cq = 3 * MOBA_WIDTH
    o_ckv = o_cq + MLA_Q_RANK
    o_kpe = o_ckv + MLA_KV_RANK
    mla_scale = (MLA_NOPE_DIM + MLA_ROPE_DIM) ** -0.5 * LOG2E
    half_m = MLA_ROPE_DIM // 2

    cqn = _rms(z[:, o_cq:o_cq + MLA_Q_RANK], gcq_ref[...]).astype(BF16)
    qn = jnp.dot(cqn, wuqn_ref[...], preferred_element_type=F32) * mla_scale
    qp = jnp.dot(cqn, wuqp_ref[...], preferred_element_type=F32)
    lane = lax.broadcasted_iota(jnp.int32, (tm, LANES), 1)
    heads_per_chunk = LANES // MLA_ROPE_DIM
    qp_chunks = [_rope128(qp[:, g * LANES:(g + 1) * LANES], tab_m, half_m) * mla_scale
                 for g in range(MLA_HEADS // heads_per_chunk)]
    for hd in range(MLA_HEADS):
        qcat_ref[0, hd, :, 0:LANES] = qn[:, hd * LANES:(hd + 1) * LANES].astype(BF16)
        sub = hd % heads_per_chunk
        keep = (lane >= sub * MLA_ROPE_DIM) & (lane < (sub + 1) * MLA_ROPE_DIM)
        qcat_ref[0, hd, :, LANES:2 * LANES] = jnp.where(keep, qp_chunks[hd // heads_per_chunk], 0.0).astype(BF16)

    ckvn = _rms(z[:, o_ckv:o_ckv + MLA_KV_RANK], gckv_ref[...]).astype(BF16)
    kv = jnp.dot(ckvn, wukv_ref[...], preferred_element_type=F32)
    kpe = _rope128(z[:, o_kpe:o_kpe + LANES], tab_m, half_m).astype(BF16)
    for p in range(N_PAIRS):
        kcat_ref[0, p, :, 0:LANES] = kv[:, p * LANES:(p + 1) * LANES].astype(BF16)
        kcat_ref[0, p, :, LANES:2 * LANES] = kpe
    _store_vt(kv, MLA_WIDTH, vmt_ref)


def _inproj(x, mod3, g_pre, w_in_p, rope_cs, rope_exp, g_cq, w_uq_n, w_uq_p, g_ckv, w_ukv_p):
    B, S, D = x.shape
    tm = min(TM_IN, S)
    nb = S // MOBA_BLOCK
    rb = tm // MOBA_BLOCK
    const2 = lambda b, i: (0, 0)
    tok3 = lambda b, i: (b, i, 0)
    outs = pl.pallas_call(
        _inproj_kernel,
        out_shape=[
            jax.ShapeDtypeStruct((B, S, MOBA_WIDTH), BF16),
            jax.ShapeDtypeStruct((B, S, MOBA_WIDTH), BF16),
            jax.ShapeDtypeStruct((B, N_PAIRS, nb, LANES, MOBA_BLOCK), BF16),
            jax.ShapeDtypeStruct((B, nb, 1, MOBA_WIDTH), F32),
            jax.ShapeDtypeStruct((B, MLA_HEADS, S, 2 * LANES), BF16),
            jax.ShapeDtypeStruct((B, N_PAIRS, S, 2 * LANES), BF16),
            jax.ShapeDtypeStruct((B, N_PAIRS, nb, LANES, MOBA_BLOCK), BF16),
        ],
        grid=(B, S // tm),
        in_specs=[
            pl.BlockSpec((1, tm, D), tok3),
            pl.BlockSpec((1, 6, D), lambda b, i: (b, 0, 0)),
            pl.BlockSpec((1, D), const2),
            pl.BlockSpec(w_in_p.shape, const2),
            pl.BlockSpec((1, LANES, tm), lambda b, i: (b, 0, i)),
            pl.BlockSpec(rope_exp.shape, const2),
            pl.BlockSpec((1, MLA_Q_RANK), const2),
            pl.BlockSpec(w_uq_n.shape, const2),
            pl.BlockSpec(w_uq_p.shape, const2),
            pl.BlockSpec((1, MLA_KV_RANK), const2),
            pl.BlockSpec(w_ukv_p.shape, const2),
        ],
        out_specs=[
            pl.BlockSpec((1, tm, MOBA_WIDTH), tok3),
            pl.BlockSpec((1, tm, MOBA_WIDTH), tok3),
            pl.BlockSpec((1, N_PAIRS, rb, LANES, MOBA_BLOCK), lambda b, i: (b, 0, i, 0, 0)),
            pl.BlockSpec((1, rb, 1, MOBA_WIDTH), lambda b, i: (b, i, 0, 0)),
            pl.BlockSpec((1, MLA_HEADS, tm, 2 * LANES), lambda b, i: (b, 0, i, 0)),
            pl.BlockSpec((1, N_PAIRS, tm, 2 * LANES), lambda b, i: (b, 0, i, 0)),
            pl.BlockSpec((1, N_PAIRS, rb, LANES, MOBA_BLOCK), lambda b, i: (b, 0, i, 0, 0)),
        ],
        compiler_params=_cparams(("arbitrary", "arbitrary")),
        name="inproj",
    )(x, mod3, g_pre, w_in_p, rope_cs, rope_exp, g_cq, w_uq_n, w_uq_p, g_ckv, w_ukv_p)
    return outs


def _nt_dot(a, b, **kw):
    return lax.dot_general(a, b, (((1,), (1,)), ((), ())), preferred_element_type=F32, **kw)


def _group_blocks(i, t):
    out = []
    for g in range(KV_GROUP):
        j = i - KV_GROUP * t - (KV_GROUP - 1 - g)
        out.append((jnp.maximum(j, 0), j >= 0))
    return out


def _stage_tables(nb):
    qi, grp = [], []
    for i in range(nb):
        for t in range((i + KV_GROUP) // KV_GROUP):
            qi.append(i)
            grp.append(t)
    n = len(qi)
    qi.append(qi[-1])
    grp.append(grp[-1])
    return np.asarray(qi, np.int32), np.asarray(grp, np.int32), n


def _flash_sweep(tables, n_stages, q_tile, k_tile, vt_tile, tile_bias, write_out, scratch):
    qi_ref, grp_ref = tables
    st_a, cm_a, st_b, cm_b, m_ref, l_ref, acc_ref, ot_ref = scratch
    L = MOBA_BLOCK
    key_minus_qry = (lax.broadcasted_iota(jnp.int32, (L, L), 0) - lax.broadcasted_iota(jnp.int32, (L, L), 1))

    def score_tile(i, blocks, g, hd, st_buf, cm_buf):
        j, _ = blocks[g]
        st = _nt_dot(k_tile(j), q_tile(i, hd))
        if g == KV_GROUP - 1:
            st = jnp.where(key_minus_qry <= (i - j) * L, st, NEG)
        st_buf[g, hd] = st
        cm_buf[g, hd] = jnp.max(st, axis=0, keepdims=True)

    def stage(s, cur, nxt):
        i, t = qi_ref[s], grp_ref[s]
        i_nxt = qi_ref[s + 1]
        blocks = _group_blocks(i, t)
        nxt_blocks = _group_blocks(i_nxt, grp_ref[s + 1])

        fresh = t == 0
        for hd in range(HEADS_PER_STEP):
            biases = [tile_bias(i, hd, j, real) for j, real in blocks]
            m_old = jnp.where(fresh, -jnp.inf, m_ref[hd])
            m_new = m_old
            for g in range(KV_GROUP):
                m_new = jnp.maximum(m_new, cur[1][g, hd] + biases[g])
            alpha = jnp.exp2(m_old - m_new)
            l_new = alpha * jnp.where(fresh, 0.0, l_ref[hd])
            pv = None
            for g, (j, _) in enumerate(blocks):
                score_tile(i_nxt, nxt_blocks, g, hd, *nxt)
                p = jnp.exp2(cur[0][g, hd] - (m_new - biases[g]))
                l_new = l_new + jnp.sum(p, axis=0, keepdims=True)
                d = jnp.dot(vt_tile(j, hd), p.astype(BF16), preferred_element_type=F32)
                pv = d if pv is None else pv + d
            acc_new = alpha * jnp.where(fresh, 0.0, acc_ref[hd]) + pv
            m_ref[hd] = m_new
            l_ref[hd] = l_new
            acc_ref[hd] = acc_new
            dv = acc_new.shape[0]
            ot_ref[i, hd * dv:(hd + 1) * dv, :] = acc_new / l_new

    bufs = ((st_a, cm_a), (st_b, cm_b))
    first = _group_blocks(qi_ref[0], grp_ref[0])
    for g in range(KV_GROUP):
        for hd in range(HEADS_PER_STEP):
            score_tile(qi_ref[0], first, g, hd, *bufs[0])

    def body(ss, carry):
        for k in range(STAGES_PER_TRIP):
            stage(STAGES_PER_TRIP * ss + k, bufs[k % 2], bufs[(k + 1) % 2])
        return carry

    n_trips = n_stages // STAGES_PER_TRIP
    lax.fori_loop(0, n_trips, body, 0)
    for s in range(n_trips * STAGES_PER_TRIP, n_stages):
        stage(s, bufs[s % 2], bufs[(s + 1) % 2])

    def emit(i, carry):
        write_out(i, ot_ref[i].T)
        return carry

    lax.fori_loop(0, ot_ref.shape[0], emit, 0)


def _attn_scratch(dv, nb):
    L = MOBA_BLOCK
    st = pltpu.VMEM((KV_GROUP, HEADS_PER_STEP, L, L), F32)
    cm = pltpu.VMEM((KV_GROUP, HEADS_PER_STEP, 1, L), F32)
    vec = pltpu.VMEM((HEADS_PER_STEP, 1, L), F32)
    return [st, cm, st, cm, vec, vec, pltpu.VMEM((HEADS_PER_STEP, dv, L), F32),
            pltpu.VMEM((nb, HEADS_PER_STEP * dv, L), F32)]


def _moba_kernel(qi_ref, grp_ref, q_ref, k_ref, vt_ref, kmean_ref, o_ref, qh_ref, bias_ref, *scratch,
                 n_stages):
    L = MOBA_BLOCK
    nb = kmean_ref.shape[1]
    lane = lax.broadcasted_iota(jnp.int32, (L, LANES), 1)
    kmean = kmean_ref[0, :, 0, :]
    blk = lax.broadcasted_iota(jnp.int32, (nb, L), 0)

    def select_blocks(i, carry):
        rows = pl.ds(pl.multiple_of(i * L, L), L)
        q = q_ref[0, rows, :]
        past = blk < i
        for hd in range(HEADS_PER_STEP):
            own = (lane >= hd * MOBA_HEAD_DIM) & (lane < (hd + 1) * MOBA_HEAD_DIM)
            qh = jnp.where(own, q, jnp.zeros_like(q))
            qh_ref[hd, rows, :] = qh
            gate = _nt_dot(kmean, qh.astype(F32), precision=lax.Precision.HIGHEST)
            gate = jnp.where(past, gate, -jnp.inf)
            bias = jnp.where(blk == i, 0.0, NEG)
            for _ in range(MOBA_TOPK):
                top = jnp.max(gate, axis=0, keepdims=True)
                at = blk == jnp.min(jnp.where(gate == top, blk, nb), axis=0, keepdims=True)
                bias = jnp.where(at & (top > -jnp.inf), 0.0, bias)
                gate = jnp.where(at, -jnp.inf, gate)
            bias_ref[hd, i] = bias
        return carry

    lax.fori_loop(0, nb, select_blocks, 0, unroll=4 if nb % 4 == 0 else 1)

    def write_out(i, o):
        o_ref[0, pl.ds(pl.multiple_of(i * L, L), L), :] = o

    _flash_sweep(
        (qi_ref, grp_ref), n_stages,
        q_tile=lambda i, hd: qh_ref[hd, pl.ds(pl.multiple_of(i * L, L), L), :],
        k_tile=lambda j: k_ref[0, pl.ds(pl.multiple_of(j * L, L), L), :],
        vt_tile=lambda j, hd: vt_ref[0, 0, j, pl.ds(hd * MOBA_HEAD_DIM, MOBA_HEAD_DIM), :],
        tile_bias=lambda i, hd, j, real: jnp.where(real, bias_ref[hd, i, pl.ds(j, 1), :], NEG),
        write_out=write_out, scratch=scratch)


def _moba(qa, ka, vat, kmean):
    B, S, _ = qa.shape
    L = MOBA_BLOCK
    nb = S // L
    qi, grp, n_stages = _stage_tables(nb)
    pair3 = lambda b, p, *_: (b, 0, p)
    grid_spec = pltpu.PrefetchScalarGridSpec(
        num_scalar_prefetch=2,
        grid=(B, N_PAIRS),
        in_specs=[
            pl.BlockSpec((1, S, LANES), pair3),
            pl.BlockSpec((1, S, LANES), pair3),
            pl.BlockSpec((1, 1, nb, LANES, L), lambda b, p, *_: (b, p, 0, 0, 0)),
            pl.BlockSpec((1, nb, 1, LANES), lambda b, p, *_: (b, 0, 0, p)),
        ],
        out_specs=pl.BlockSpec((1, S, LANES), pair3),
        scratch_shapes=[
            pltpu.VMEM((HEADS_PER_STEP, S, LANES), BF16),
            pltpu.VMEM((HEADS_PER_STEP, nb, nb, L), F32),
        ] + _attn_scratch(MOBA_HEAD_DIM, nb),
    )
    return pl.pallas_call(
        functools.partial(_moba_kernel, n_stages=n_stages),
        out_shape=jax.ShapeDtypeStruct((B, S, MOBA_WIDTH), F32),
        grid_spec=grid_spec,
        compiler_params=_cparams(("arbitrary", "arbitrary")),
        name="moba",
    )(jnp.asarray(qi), jnp.asarray(grp), qa, ka, vat, kmean)


def _mla_kernel(qi_ref, grp_ref, q_ref, k_ref, vt_ref, o_ref, *scratch, n_stages):
    L = MOBA_BLOCK

    def write_out(i, o):
        o_ref[0, pl.ds(pl.multiple_of(i * L, L), L), :] = o

    _flash_sweep(
        (qi_ref, grp_ref), n_stages,
        q_tile=lambda i, hd: q_ref[0, hd, pl.ds(pl.multiple_of(i * L, L), L), :],
        k_tile=lambda j: k_ref[0, 0, pl.ds(pl.multiple_of(j * L, L), L), :],
        vt_tile=lambda j, hd: vt_ref[0, 0, j, pl.ds(hd * MLA_V_DIM, MLA_V_DIM), :],
        tile_bias=lambda i, hd, j, real: jnp.where(real, 0.0, NEG),
        write_out=write_out, scratch=scratch)


def _mla(qcat, kcat, vmt):
    B, _, S, W = qcat.shape
    L = MOBA_BLOCK
    nb = S // L
    qi, grp, n_stages = _stage_tables(nb)
    grid_spec = pltpu.PrefetchScalarGridSpec(
        num_scalar_prefetch=2,
        grid=(B, N_PAIRS),
        in_specs=[
            pl.BlockSpec((1, HEADS_PER_STEP, S, W), lambda b, p, *_: (b, p, 0, 0)),
            pl.BlockSpec((1, 1, S, W), lambda b, p, *_: (b, p, 0, 0)),
            pl.BlockSpec((1, 1, nb, LANES, L), lambda b, p, *_: (b, p, 0, 0, 0)),
        ],
        out_specs=pl.BlockSpec((1, S, LANES), lambda b, p, *_: (b, 0, p)),
        scratch_shapes=_attn_scratch(MLA_V_DIM, nb),
    )
    return pl.pallas_call(
        functools.partial(_mla_kernel, n_stages=n_stages),
        out_shape=jax.ShapeDtypeStruct((B, S, MLA_WIDTH), F32),
        grid_spec=grid_spec,
        compiler_params=_cparams(("arbitrary", "arbitrary")),
        name="mla",
    )(jnp.asarray(qi), jnp.asarray(grp), qcat, kcat, vmt)


def _postattn_kernel(x_ref, om_ref, ol_ref, mod_ref, gom_ref, gol_ref, wo_ref, gpost_ref, gpre_ref,
                     wr_ref, br_ref, x1_ref, xl_ref, gl_ref, cnt_ref, ri_ref):
    tm = x_ref.shape[1]
    gt_a = mod_ref[0, 2:3, :]
    sh_f = mod_ref[0, 3:4, :]
    sc_f = mod_ref[0, 4:5, :]
    a = jnp.concatenate([_rms(om_ref[0], gom_ref[...]), _rms(ol_ref[0], gol_ref[...])], axis=-1)
    o = jnp.dot(a.astype(BF16), wo_ref[...], preferred_element_type=F32)
    x1 = x_ref[0] + gt_a * _rms(o, gpost_ref[...])
    x1_ref[0] = x1
    h2 = _rms(x1, gpre_ref[...]) * (1.0 + sc_f) + sh_f

    h2_hi = h2.astype(BF16)
    h2_lo = (h2 - h2_hi.astype(F32)).astype(BF16)
    logit = (jnp.dot(h2_hi, wr_ref[0], preferred_element_type=F32)
             + jnp.dot(h2_lo, wr_ref[0], preferred_element_type=F32)
             + jnp.dot(h2_hi, wr_ref[1], preferred_element_type=F32))
    biased = logit + br_ref[...]
    lane = lax.broadcasted_iota(jnp.int32, (tm, LANES), 1)
    big = jnp.int32(LANES)

    def first_argmax(v):
        mx = jnp.max(v, axis=-1, keepdims=True)
        return jnp.min(jnp.where(v == mx, lane, big), axis=-1, keepdims=True)

    def pick(v, idx):
        return jnp.sum(jnp.where(lane == idx, v, 0.0), axis=-1, keepdims=True)

    gmask = (lane >= N_EXPERTS) & (lane < N_EXPERTS + N_GROUPS)
    gl = jnp.where(gmask, logit, -jnp.inf)
    pg = jnp.exp(gl - jnp.max(gl, axis=-1, keepdims=True))
    g_lane = first_argmax(jnp.where(gmask, biased, -jnp.inf))
    g_prob = pick(pg, g_lane) / jnp.sum(pg, axis=-1, keepdims=True)
    e_lo = (g_lane - N_EXPERTS) * EXPERTS_PER_GROUP
    emask = (lane >= e_lo) & (lane < e_lo + EXPERTS_PER_GROUP)
    el = jnp.where(emask, logit, -jnp.inf)
    pe = jnp.exp(el - jnp.max(el, axis=-1, keepdims=True))
    eb = jnp.where(emask, biased, -jnp.inf)
    i1 = first_argmax(eb)
    i2 = first_argmax(jnp.where(lane == i1, -jnp.inf, eb))
    p1 = pick(pe, i1)
    p2 = pick(pe, i2)
    denom = p1 + p2
    w1 = p1 / denom * g_prob
    w2 = p2 / denom * g_prob

    hit1 = lane == i1
    hit2 = lane == i2
    onehot = jnp.where(hit1 | hit2, 1.0, 0.0).astype(BF16)
    tok_r = lax.broadcasted_iota(jnp.int32, (tm, tm), 0)
    tok_c = lax.broadcasted_iota(jnp.int32, (tm, tm), 1)
    csum = jnp.dot(jnp.where(tok_c <= tok_r, 1.0, 0.0).astype(BF16), onehot,
                   preferred_element_type=F32)
    cnt = csum[tm - 1:tm, :]
    seg = jnp.floor((cnt + (CHUNK - 1.0)) * (1.0 / CHUNK)) * CHUNK
    ex_r = lax.broadcasted_iota(jnp.int32, (LANES, LANES), 0)
    ex_c = lax.broadcasted_iota(jnp.int32, (LANES, LANES), 1)
    seg_start = jnp.dot(jnp.broadcast_to(seg, (8, LANES)).astype(BF16),
                        jnp.where(ex_r < ex_c, 1.0, 0.0).astype(BF16),
                        preferred_element_type=F32)[0:1, :]
    place = seg_start + csum - 1.0
    slot1 = jnp.sum(jnp.where(hit1, place, 0.0), axis=-1, keepdims=True)
    slot2 = jnp.sum(jnp.where(hit2, place, 0.0), axis=-1, keepdims=True)
    ri_ref[0] = jnp.where(lane == 0, i1, jnp.where(lane == 1, i2, jnp.where(
        lane == 2, slot1.astype(jnp.int32), jnp.where(lane == 3, slot2.astype(jnp.int32), 0))))
    cnt_ref[0, 0] = cnt

    packed = jnp.where(lane == 0, slot1, jnp.where(lane == 1, slot2, jnp.where(
        lane == 2, w1, jnp.where(lane == 3, w2, 0.0))))
    packed_t = packed.T
    srow = lax.broadcasted_iota(jnp.int32, (LOCAL_ROWS, tm), 0).astype(F32)
    at1 = srow == packed_t[0:1, :]
    at2 = srow == packed_t[1:2, :]
    perm = jnp.where(at1 | at2, 1.0, 0.0).astype(BF16)
    xl_ref[0] = jnp.dot(perm, h2_hi, preferred_element_type=F32).astype(BF16)
    gate = jnp.sum(jnp.where(at1, packed_t[2:3, :], 0.0) + jnp.where(at2, packed_t[3:4, :], 0.0),
                   axis=-1, keepdims=True)
    gl_ref[0] = jnp.broadcast_to(gate, (LOCAL_ROWS, LANES))


def _postattn(x, o_moba, o_mla, mod3, g_om, g_ol, w_o, g_post, g_pre, w_r, b_r):
    B, S, D = x.shape
    tm = TM_IN
    assert S % tm == 0 and D == D_MODEL
    const2 = lambda b, i: (0, 0)
    tok3 = lambda b, i: (b, i, 0)
    n_j = S // tm
    return pl.pallas_call(
        _postattn_kernel,
        out_shape=[
            jax.ShapeDtypeStruct((B, S, D), F32),
            jax.ShapeDtypeStruct((B * n_j, LOCAL_ROWS, D), BF16),
            jax.ShapeDtypeStruct((B * n_j, LOCAL_ROWS, LANES), F32),
            jax.ShapeDtypeStruct((B * n_j, 1, 1, LANES), F32),
            jax.ShapeDtypeStruct((B, S, LANES), jnp.int32),
        ],
        grid=(B, S // tm),
        in_specs=[
            pl.BlockSpec((1, tm, D), tok3),
            pl.BlockSpec((1, tm, MOBA_WIDTH), tok3),
            pl.BlockSpec((1, tm, MLA_WIDTH), tok3),
            pl.BlockSpec((1, 6, D), lambda b, i: (b, 0, 0)),
            pl.BlockSpec((1, MOBA_WIDTH), const2),
            pl.BlockSpec((1, MLA_WIDTH), const2),
            pl.BlockSpec(w_o.shape, const2),
            pl.BlockSpec((1, D), const2),
            pl.BlockSpec((1, D), const2),
            pl.BlockSpec(w_r.shape, lambda b, i: (0, 0, 0)),
            pl.BlockSpec((1, LANES), const2),
        ],
        out_specs=[
            pl.BlockSpec((1, tm, D), tok3),
            pl.BlockSpec((1, LOCAL_ROWS, D), lambda b, i: (b * n_j + i, 0, 0)),
            pl.BlockSpec((1, LOCAL_ROWS, LANES), lambda b, i: (b * n_j + i, 0, 0)),
            pl.BlockSpec((1, 1, 1, LANES), lambda b, i: (b * n_j + i, 0, 0, 0)),
            pl.BlockSpec((1, tm, LANES), tok3),
        ],
        compiler_params=_cparams(("arbitrary", "arbitrary")),
        name="postattn",
    )(x, o_moba, o_mla, mod3, g_om, g_ol, w_o, g_post, g_pre, w_r, b_r)


def _start_chunk_gather(idx_ref, base, n_chunks, src_hbm, dst_ref, sem):
    for k in range(n_chunks):
        row = pl.multiple_of(idx_ref[base + k], CHUNK)
        pltpu.make_async_copy(src_hbm.at[pl.ds(row, CHUNK), :], dst_ref.at[pl.ds(k * CHUNK, CHUNK), :], sem).start()


def _wait_chunk_gather(n_chunks, src_hbm, dst_ref, sem):
    pltpu.make_async_copy(src_hbm.at[pl.ds(0, n_chunks * CHUNK), :], dst_ref, sem).wait()


def _moe_kernel(te_ref, nu_ref, src_ref, xl_hbm, gl_hbm, wgu_ref, wd_ref, y_ref, xbuf, gbuf, xsem, gsem):
    tm = y_ref.shape[0]
    n_chunks = tm // CHUNK
    i = pl.program_id(0)
    n_used = nu_ref[0]
    slot = i % 2

    def start(tile, s):
        _start_chunk_gather(src_ref, tile * n_chunks, n_chunks, xl_hbm, xbuf.at[s], xsem.at[s])
        _start_chunk_gather(src_ref, tile * n_chunks, n_chunks, gl_hbm, gbuf.at[s], gsem.at[s])

    @pl.when((i == 0) & (n_used > 0))
    def _():
        start(0, 0)

    @pl.when(i + 1 < n_used)
    def _():
        start(i + 1, 1 - slot)

    @pl.when(i < n_used)
    def _():
        _wait_chunk_gather(n_chunks, xl_hbm, xbuf.at[slot], xsem.at[slot])
        _wait_chunk_gather(n_chunks, gl_hbm, gbuf.at[slot], gsem.at[slot])
        gate = gbuf[slot]
        gu = jnp.dot(xbuf[slot], wgu_ref[0], preferred_element_type=F32)
        g = gu[:, :D_EXPERT]
        u = gu[:, D_EXPERT:]
        hid = g * jax.nn.sigmoid(g) * u
        y = jnp.dot(hid.astype(BF16), wd_ref[0], preferred_element_type=F32)
        y_ref[...] = (jnp.concatenate([gate] * (D_MODEL // LANES), axis=1) * y).astype(BF16)

    @pl.when(i >= n_used)
    def _():
        y_ref[...] = jnp.zeros_like(y_ref)


def _moe(tile_expert, n_used, chunk_src, x_local, g_local, w_gu, w_d):
    D = D_MODEL
    n_tiles = tile_expert.shape[0]
    tm = TM_MOE
    grid_spec = pltpu.PrefetchScalarGridSpec(
        num_scalar_prefetch=3,
        grid=(n_tiles,),
        in_specs=[
            pl.BlockSpec(memory_space=pl.ANY),
            pl.BlockSpec(memory_space=pl.ANY),
            pl.BlockSpec((1, D, 2 * D_EXPERT), lambda i, te, nu, src: (te[i], 0, 0)),
            pl.BlockSpec((1, D_EXPERT, D), lambda i, te, nu, src: (te[i], 0, 0)),
        ],
        out_specs=pl.BlockSpec((tm, D), lambda i, te, nu, src: (i, 0)),
        scratch_shapes=[pltpu.VMEM((2, tm, D), BF16), pltpu.VMEM((2, tm, LANES), F32),
                        pltpu.SemaphoreType.DMA((2,)), pltpu.SemaphoreType.DMA((2,))],
    )
    return pl.pallas_call(
        _moe_kernel,
        out_shape=jax.ShapeDtypeStruct((n_tiles * tm, D), BF16),
        grid_spec=grid_spec,
        compiler_params=_cparams(("arbitrary",)),
        name="moe",
    )(tile_expert, n_used, chunk_src, x_local, g_local, w_gu, w_d)


def _combine_kernel(src_ref, y_hbm, x1_ref, ri_ref, mod_ref, g_ref, o_ref, ybuf, sem):
    tm = x1_ref.shape[1]
    n_chunks = LOCAL_ROWS // CHUNK
    n_j = pl.num_programs(1)
    step = pl.program_id(0) * n_j + pl.program_id(1)
    n_steps = pl.num_programs(0) * n_j
    slot = step % 2

    @pl.when(step == 0)
    def _():
        _start_chunk_gather(src_ref, 0, n_chunks, y_hbm, ybuf.at[0], sem.at[0])

    @pl.when(step + 1 < n_steps)
    def _():
        _start_chunk_gather(src_ref, (step + 1) * n_chunks, n_chunks, y_hbm, ybuf.at[1 - slot], sem.at[1 - slot])

    _wait_chunk_gather(n_chunks, y_hbm, ybuf.at[slot], sem.at[slot])
    yl = ybuf[slot]
    col = lax.broadcasted_iota(jnp.int32, (tm, LOCAL_ROWS), 1)
    pick = jnp.where((col == ri_ref[0, :, 2:3]) | (col == ri_ref[0, :, 3:4]), 1.0, 0.0).astype(BF16)
    y = jnp.dot(pick, yl, preferred_element_type=F32)
    gt_f = mod_ref[0, 5:6, :]
    o_ref[0] = x1_ref[0] + gt_f * _rms(y, g_ref[...])


def _combine(chunk_src, y_sorted, x1, route_i, mod3, g_post):
    B, S, D = x1.shape
    tm = TM_IN
    grid_spec = pltpu.PrefetchScalarGridSpec(
        num_scalar_prefetch=1,
        grid=(B, S // tm),
        in_specs=[
            pl.BlockSpec(memory_space=pl.ANY),
            pl.BlockSpec((1, tm, D), lambda b, j, src: (b, j, 0)),
            pl.BlockSpec((1, tm, LANES), lambda b, j, src: (b, j, 0)),
            pl.BlockSpec((1, 6, D), lambda b, j, src: (b, 0, 0)),
            pl.BlockSpec((1, D), lambda b, j, src: (0, 0)),
        ],
        out_specs=pl.BlockSpec((1, tm, D), lambda b, j, src: (b, j, 0)),
        scratch_shapes=[pltpu.VMEM((2, LOCAL_ROWS, D), BF16), pltpu.SemaphoreType.DMA((2,))],
    )
    return pl.pallas_call(
        _combine_kernel,
        out_shape=jax.ShapeDtypeStruct((B, S, D), F32),
        grid_spec=grid_spec,
        compiler_params=_cparams(("arbitrary", "arbitrary")),
        name="combine",
    )(chunk_src, y_sorted, x1, route_i, mod3, g_post)


def _rope_angles(positions):
    def cos_sin(dim):
        half = dim // 2
        inv_freq = jnp.exp(-math.log(ROPE_THETA) * jnp.arange(half, dtype=F32) * (2.0 / dim))
        ang = positions.astype(F32)[:, None, :] * inv_freq[None, :, None]
        return [jnp.cos(ang), jnp.sin(ang)]
    B, S = positions.shape
    parts = cos_sin(MOBA_ROT_DIMS) + cos_sin(MLA_ROPE_DIM) + [jnp.ones((B, 1, S), F32)]
    used = MOBA_ROT_DIMS + MLA_ROPE_DIM + 1
    return jnp.concatenate(parts + [jnp.zeros((B, LANES - used, S), F32)], axis=1)


def _routing_metadata(counts, tm):
    i32 = jnp.int32
    n_tt = counts.shape[0]
    n_tiles = (2 * n_tt * TM_IN + n_tt * N_EXPERTS * (CHUNK - 1)) // tm + N_EXPERTS + 1
    seg = (counts + CHUNK - 1) // CHUNK * CHUNK
    loc_start = jnp.cumsum(seg, axis=1) - seg
    loc_total = jnp.sum(seg, axis=1)
    run = jnp.cumsum(seg, axis=0)
    rows_e = run[-1]
    tiles_per = (rows_e + tm - 1) // tm
    tile_end = jnp.cumsum(tiles_per)
    n_used = tile_end[-1]
    first_row = (tile_end - tiles_per) * tm
    tile_ids = jnp.arange(n_tiles, dtype=i32)
    te = jnp.minimum(jnp.sum((tile_ids[:, None] >= tile_end[None, :]).astype(i32), axis=1), N_EXPERTS - 1)
    last_e = jnp.sum(jnp.where(tile_ids == n_used - 1, te, 0))
    tile_expert = jnp.where(tile_ids < n_used, te, last_e).astype(i32)

    cpt = tm // CHUNK
    is_e = (tile_expert[:, None] == jnp.arange(N_EXPERTS, dtype=i32)[None, :]).astype(i32)
    of_tile = lambda a: jnp.sum(is_e[:, None, :] * a[None, :, :], axis=2)
    run_t, seg_t, loc_t = of_tile(run), of_tile(seg), of_tile(loc_start)
    first_t = jnp.sum(is_e * first_row[None, :], axis=1)
    rows_t = jnp.sum(is_e * rows_e[None, :], axis=1)
    off = (tile_ids * tm - first_t)[:, None] + jnp.arange(cpt, dtype=i32)[None, :] * CHUNK
    tau = jnp.minimum(jnp.sum((off[:, :, None] >= run_t[:, None, :]).astype(i32), axis=2), n_tt - 1)
    is_tau = (tau[:, :, None] == jnp.arange(n_tt, dtype=i32)[None, None, :]).astype(i32)
    at_tau = lambda a_t: jnp.sum(is_tau * a_t[:, None, :], axis=2)
    src = tau * LOCAL_ROWS + at_tau(loc_t) + off - (at_tau(run_t) - at_tau(seg_t))
    real = (tile_ids < n_used)[:, None] & (off < rows_t[:, None])
    zero_local = LOCAL_ROWS - CHUNK
    moe_src = jnp.where(real, src, zero_local).astype(i32).reshape(-1)

    lrow = jnp.arange(LOCAL_ROWS // CHUNK, dtype=i32) * CHUNK
    loc_end = loc_start + seg
    e_v = jnp.minimum(jnp.sum((lrow[None, :, None] >= loc_end[:, None, :]).astype(i32), axis=2), N_EXPERTS - 1)
    is_ev = (e_v[:, :, None] == jnp.arange(N_EXPERTS, dtype=i32)[None, None, :]).astype(i32)
    seg_first = first_row[None, :] + run - seg
    src_v = jnp.sum(is_ev * (seg_first - loc_start)[:, None, :], axis=2) + lrow[None, :]
    zero_sorted = (n_tiles - 1) * tm
    cmb_src = jnp.where(lrow[None, :] < loc_total[:, None], src_v, zero_sorted).astype(i32).reshape(-1)
    return tile_expert, n_used.reshape(1).astype(i32), moe_src, cmb_src


def kernel(x, c, positions, w_ada, b_ada, g_pre_attn, g_post_attn, g_pre_ffn, g_post_ffn, w_in, g_cq, w_uq,
           g_ckv, w_ukv, g_out_moba, g_out_mla, w_o, w_group, b_group, w_expert, b_expert, w_e_gate, w_e_up,
           w_e_down):
    B, S, D = x.shape
    depth = w_ada.shape[0]
    rope_cs = _rope_angles(positions)
    rope_exp = jnp.asarray(_rope_expansion(), BF16)
    row = lambda v: v.reshape(1, -1)

    for l in range(depth):
        n_main = 3 * MOBA_WIDTH + MLA_Q_RANK + MLA_KV_RANK
        w_in_p = jnp.concatenate([w_in[l][:, :n_main], jnp.tile(w_in[l][:, n_main:], (1, LANES // MLA_ROPE_DIM))],
                                 axis=1).astype(BF16)
        wq = w_uq[l].reshape(MLA_Q_RANK, MLA_HEADS, MLA_NOPE_DIM + MLA_ROPE_DIM)
        wq_n = wq[:, :, :MLA_NOPE_DIM]
        zero_n = jnp.zeros_like(wq_n)
        even = (jnp.arange(MLA_HEADS) % 2 == 0)[None, :, None]
        w_uq_n = jnp.concatenate([jnp.where(even, wq_n, zero_n), jnp.where(even, zero_n, wq_n)], axis=-1)
        w_uq_n = w_uq_n.reshape(MLA_Q_RANK, MLA_HEADS * LANES).astype(BF16)
        w_uq_p = wq[:, :, MLA_NOPE_DIM:].reshape(MLA_Q_RANK, MLA_HEADS * MLA_ROPE_DIM).astype(BF16)
        wkv = w_ukv[l].reshape(MLA_KV_RANK, MLA_HEADS, MLA_NOPE_DIM + MLA_V_DIM)
        w_ukv_p = jnp.concatenate([wkv[:, :, :MLA_NOPE_DIM].reshape(MLA_KV_RANK, -1),
                                   wkv[:, :, MLA_NOPE_DIM:].reshape(MLA_KV_RANK, -1)], axis=1).astype(BF16)
        w_r = jnp.concatenate([w_expert[l], w_group[l],
                               jnp.zeros((D, LANES - N_EXPERTS - N_GROUPS), F32)], axis=1)
        w_r_hi = w_r.astype(BF16)
        w_r = jnp.stack([w_r_hi, (w_r - w_r_hi.astype(F32)).astype(BF16)])
        b_r = jnp.concatenate([b_expert[l], b_group[l], jnp.zeros((LANES - N_EXPERTS - N_GROUPS,), F32)]).reshape(1, LANES)
        w_gu = jnp.concatenate([w_e_gate[l], w_e_up[l]], axis=-1).astype(BF16)
        w_d = w_e_down[l].astype(BF16)

        mod3 = _adaln(c, w_ada[l], b_ada[l]).reshape(B, 6, D)

        qa, ka, vat, kmean, qcat, kcat, vmt = _inproj(
            x, mod3, row(g_pre_attn[l]), w_in_p, rope_cs, rope_exp, row(g_cq[l]), w_uq_n, w_uq_p,
            row(g_ckv[l]), w_ukv_p)
        o_moba = _moba(qa, ka, vat, kmean)
        o_mla = _mla(qcat, kcat, vmt)
        x1, x_local, g_local, counts, route_i = _postattn(
            x, o_moba, o_mla, mod3, row(g_out_moba[l]), row(g_out_mla[l]), w_o[l].astype(BF16),
            row(g_post_attn[l]), row(g_pre_ffn[l]), w_r, b_r)

        counts = counts.reshape(-1, LANES)[:, :N_EXPERTS].astype(jnp.int32)
        tile_expert, n_used, moe_src, cmb_src = _routing_metadata(counts, TM_MOE)
        y_sorted = _moe(tile_expert, n_used, moe_src, x_local.reshape(-1, D), g_local.reshape(-1, LANES), w_gu, w_d)
        x = _combine(cmb_src, y_sorted, x1, route_i, mod3, row(g_post_ffn[l]))
    return x
```

```python
import functools
import math

import jax
import jax.numpy as jnp
import numpy as np
from jax import lax
from jax.experimental import pallas as pl
from jax.experimental.pallas import tpu as pltpu

F32 = jnp.float32
BF16 = jnp.bfloat16

D_MODEL = 1024
MOBA_HEADS = 8
MOBA_HEAD_DIM = 64
MOBA_WIDTH = MOBA_HEADS * MOBA_HEAD_DIM
MOBA_BLOCK = 256
MOBA_TOPK = 3
MOBA_ROT_DIMS = MOBA_HEAD_DIM // 4
MLA_HEADS = 8
MLA_Q_RANK = 256
MLA_KV_RANK = 128
MLA_NOPE_DIM = 64
MLA_ROPE_DIM = 32
MLA_V_DIM = 64
MLA_WIDTH = MLA_HEADS * MLA_V_DIM
ROPE_THETA = 500000.0
N_GROUPS = 4
EXPERTS_PER_GROUP = 8
N_EXPERTS = N_GROUPS * EXPERTS_PER_GROUP
D_EXPERT = 256
EPS = 1e-6
NEG = -1e30
LOG2E = math.log2(math.e)

LANES = 128
SUBLANES = 8
HEADS_PER_STEP = LANES // MOBA_HEAD_DIM
N_PAIRS = MOBA_HEADS // HEADS_PER_STEP
VMEM_LIMIT = 56 * 1024 * 1024

ADALN_COLS = 1536
TM_IN = 512
KV_GROUP = 4
STAGES_PER_TRIP = 4
TM_MOE = 512
CHUNK = 16
LOCAL_ROWS = 2 * TM_IN + N_EXPERTS * CHUNK


def _cparams(sem):
    return pltpu.CompilerParams(dimension_semantics=sem, vmem_limit_bytes=VMEM_LIMIT)


def _rms(x, g):
    return x * lax.rsqrt(jnp.mean(x * x, axis=-1, keepdims=True) + EPS) * g


def _adaln_kernel(c_ref, w_ref, b_ref, o_ref):
    c = c_ref[...]
    ca = c * jax.nn.sigmoid(c)
    o_ref[...] = jnp.dot(ca, w_ref[...], preferred_element_type=F32,
                         precision=lax.Precision.HIGHEST) + b_ref[...]


def _adaln(c, w, b):
    B, D = c.shape
    N = w.shape[1]
    tn = ADALN_COLS if N % ADALN_COLS == 0 else N
    return pl.pallas_call(
        _adaln_kernel,
        out_shape=jax.ShapeDtypeStruct((B, N), F32),
        grid=(N // tn,),
        in_specs=[pl.BlockSpec((B, D), lambda j: (0, 0)),
                  pl.BlockSpec((D, tn), lambda j: (0, j)),
                  pl.BlockSpec((1, tn), lambda j: (0, j))],
        out_specs=pl.BlockSpec((B, tn), lambda j: (0, j)),
        compiler_params=_cparams(("arbitrary",)),
        name="adaln",
    )(c, w, b.reshape(1, N))


def _rope128(xc, tab, shift):
    c = tab[:, 0:LANES]
    s1 = tab[:, LANES:2 * LANES]
    s2 = tab[:, 2 * LANES:3 * LANES]
    return xc * c + pltpu.roll(xc, LANES - shift, axis=1) * s1 + pltpu.roll(xc, shift, axis=1) * s2


def _rope_expansion():
    e = np.zeros((LANES, 6 * LANES), np.float32)
    ha, hm = MOBA_ROT_DIMS // 2, MLA_ROPE_DIM // 2
    cos_a, sin_a, cos_m, sin_m, one = 0, ha, 2 * ha, 2 * ha + hm, 2 * ha + 2 * hm
    for lane in range(LANES):
        d = lane % MOBA_HEAD_DIM
        if d < ha:
            e[cos_a + d, lane] = 1.0
            e[sin_a + d, LANES + lane] = -1.0
        elif d < 2 * ha:
            e[cos_a + d - ha, lane] = 1.0
            e[sin_a + d - ha, 2 * LANES + lane] = 1.0
        else:
            e[one, lane] = 1.0
        d = lane % MLA_ROPE_DIM
        e[cos_m + d % hm, 3 * LANES + lane] = 1.0
        if d < hm:
            e[sin_m + d, 4 * LANES + lane] = -1.0
        else:
            e[sin_m + d - hm, 5 * LANES + lane] = 1.0
    return e


def _expand_exact(v, e_ref):
    hi = v.astype(BF16)
    r1 = v - hi.astype(F32)
    mid = r1.astype(BF16)
    lo = (r1 - mid.astype(F32)).astype(BF16)
    e = e_ref[...]
    return (jnp.dot(hi, e, preferred_element_type=F32) + jnp.dot(mid, e, preferred_element_type=F32)
            + jnp.dot(lo, e, preferred_element_type=F32))


def _store_vt(src, col0, vt_ref):
    for p in range(N_PAIRS):
        for r in range(src.shape[0] // MOBA_BLOCK):
            blk = src[r * MOBA_BLOCK:(r + 1) * MOBA_BLOCK, col0 + p * LANES:col0 + (p + 1) * LANES]
            vt_ref[0, p, r] = blk.T.astype(BF16)


def _inproj_kernel(x_ref, mod_ref, gpre_ref, win_ref, cs_ref, rexp_ref, gcq_ref, wuqn_ref, wuqp_ref,
                   gckv_ref, wukv_ref,
                   qa_ref, ka_ref, vat_ref, kmean_ref, qcat_ref, kcat_ref, vmt_ref):
    tm = x_ref.shape[1]
    rope_tab = _expand_exact(cs_ref[0].T, rexp_ref)
    tab_a = rope_tab[:, 0:3 * LANES]
    tab_m = rope_tab[:, 3 * LANES:6 * LANES]
    x = x_ref[0]
    sh = mod_ref[0, 0:1, :]
    sc = mod_ref[0, 1:2, :]
    h = _rms(x, gpre_ref[...]) * (1.0 + sc) + sh
    z = jnp.dot(h.astype(BF16), win_ref[...], preferred_element_type=F32)

    moba_scale = MOBA_HEAD_DIM ** -0.5 * LOG2E
    half_a = MOBA_ROT_DIMS // 2
    n_chunks = MOBA_WIDTH // LANES
    for cidx in range(n_chunks):
        lo = cidx * LANES
        q = _rope128(z[:, lo:lo + LANES], tab_a, half_a)
        qa_ref[0, :, lo:lo + LANES] = (q * moba_scale).astype(BF16)
        k = _rope128(z[:, MOBA_WIDTH + lo:MOBA_WIDTH + lo + LANES], tab_a, half_a)
        ka_ref[0, :, lo:lo + LANES] = k.astype(BF16)
        for r in range(tm // MOBA_BLOCK):
            kmean_ref[0, r, :, lo:lo + LANES] = jnp.mean(
                k[r * MOBA_BLOCK:(r + 1) * MOBA_BLOCK], axis=0, keepdims=True)
    _store_vt(z, 2 * MOBA_WIDTH, vat_ref)

    o_cq = 3 * MOBA_WIDTH
    o_ckv = o_cq + MLA_Q_RANK
    o_kpe = o_ckv + MLA_KV_RANK
    mla_scale = (MLA_NOPE_DIM + MLA_ROPE_DIM) ** -0.5 * LOG2E
    half_m = MLA_ROPE_DIM // 2

    cqn = _rms(z[:, o_cq:o_cq + MLA_Q_RANK], gcq_ref[...]).astype(BF16)
    qn = jnp.dot(cqn, wuqn_ref[...], preferred_element_type=F32) * mla_scale
    qp = jnp.dot(cqn, wuqp_ref[...], preferred_element_type=F32)
    lane = lax.broadcasted_iota(jnp.int32, (tm, LANES), 1)
    heads_per_chunk = LANES // MLA_ROPE_DIM
    qp_chunks = [_rope128(qp[:, g * LANES:(g + 1) * LANES], tab_m, half_m) * mla_scale
                 for g in range(MLA_HEADS // heads_per_chunk)]
    for hd in range(MLA_HEADS):
        qcat_ref[0, hd, :, 0:LANES] = qn[:, hd * LANES:(hd + 1) * LANES].astype(BF16)
        sub = hd % heads_per_chunk
        keep = (lane >= sub * MLA_ROPE_DIM) & (lane < (sub + 1) * MLA_ROPE_DIM)
        qcat_ref[0, hd, :, LANES:2 * LANES] = jnp.where(keep, qp_chunks[hd // heads_per_chunk], 0.0).astype(BF16)

    ckvn = _rms(z[:, o_ckv:o_ckv + MLA_KV_RANK], gckv_ref[...]).astype(BF16)
    kv = jnp.dot(ckvn, wukv_ref[...], preferred_element_type=F32)
    kpe = _rope128(z[:, o_kpe:o_kpe + LANES], tab_m, half_m).astype(BF16)
    for p in range(N_PAIRS):
        kcat_ref[0, p, :, 0:LANES] = kv[:, p * LANES:(p + 1) * LANES].astype(BF16)
        kcat_ref[0, p, :, LANES:2 * LANES] = kpe
    _store_vt(kv, MLA_WIDTH, vmt_ref)


def _inproj(x, mod3, g_pre, w_in_p, rope_cs, rope_exp, g_cq, w_uq_n, w_uq_p, g_ckv, w_ukv_p):
    B, S, D = x.shape
    tm = min(TM_IN, S)
    nb = S // MOBA_BLOCK
    rb = tm // MOBA_BLOCK
    const2 = lambda b, i: (0, 0)
    tok3 = lambda b, i: (b, i, 0)
    outs = pl.pallas_call(
        _inproj_kernel,
        out_shape=[
            jax.ShapeDtypeStruct((B, S, MOBA_WIDTH), BF16),
            jax.ShapeDtypeStruct((B, S, MOBA_WIDTH), BF16),
            jax.ShapeDtypeStruct((B, N_PAIRS, nb, LANES, MOBA_BLOCK), BF16),
            jax.ShapeDtypeStruct((B, nb, 1, MOBA_WIDTH), F32),
            jax.ShapeDtypeStruct((B, MLA_HEADS, S, 2 * LANES), BF16),
            jax.ShapeDtypeStruct((B, N_PAIRS, S, 2 * LANES), BF16),
            jax.ShapeDtypeStruct((B, N_PAIRS, nb, LANES, MOBA_BLOCK), BF16),
        ],
        grid=(B, S // tm),
        in_specs=[
            pl.BlockSpec((1, tm, D), tok3),
            pl.BlockSpec((1, 6, D), lambda b, i: (b, 0, 0)),
            pl.BlockSpec((1, D), const2),
            pl.BlockSpec(w_in_p.shape, const2),
            pl.BlockSpec((1, LANES, tm), lambda b, i: (b, 0, i)),
            pl.BlockSpec(rope_exp.shape, const2),
            pl.BlockSpec((1, MLA_Q_RANK), const2),
            pl.BlockSpec(w_uq_n.shape, const2),
            pl.BlockSpec(w_uq_p.shape, const2),
            pl.BlockSpec((1, MLA_KV_RANK), const2),
            pl.BlockSpec(w_ukv_p.shape, const2),
        ],
        out_specs=[
            pl.BlockSpec((1, tm, MOBA_WIDTH), tok3),
            pl.BlockSpec((1, tm, MOBA_WIDTH), tok3),
            pl.BlockSpec((1, N_PAIRS, rb, LANES, MOBA_BLOCK), lambda b, i: (b, 0, i, 0, 0)),
            pl.BlockSpec((1, rb, 1, MOBA_WIDTH), lambda b, i: (b, i, 0, 0)),
            pl.BlockSpec((1, MLA_HEADS, tm, 2 * LANES), lambda b, i: (b, 0, i, 0)),
            pl.BlockSpec((1, N_PAIRS, tm, 2 * LANES), lambda b, i: (b, 0, i, 0)),
            pl.BlockSpec((1, N_PAIRS, rb, LANES, MOBA_BLOCK), lambda b, i: (b, 0, i, 0, 0)),
        ],
        compiler_params=_cparams(("arbitrary", "arbitrary")),
        name="inproj",
    )(x, mod3, g_pre, w_in_p, rope_cs, rope_exp, g_cq, w_uq_n, w_uq_p, g_ckv, w_ukv_p)
    return outs


def _nt_dot(a, b, **kw):
    return lax.dot_general(a, b, (((1,), (1,)), ((), ())), preferred_element_type=F32, **kw)


def _group_blocks(i, t):
    out = []
    for g in range(KV_GROUP):
        j = i - KV_GROUP * t - (KV_GROUP - 1 - g)
        out.append((jnp.maximum(j, 0), j >= 0))
    return out


def _stage_tables(nb):
    qi, grp = [], []
    for i in range(nb):
        for t in range((i + KV_GROUP) // KV_GROUP):
            qi.append(i)
            grp.append(t)
    n = len(qi)
    qi.append(qi[-1])
    grp.append(grp[-1])
    return np.asarray(qi, np.int32), np.asarray(grp, np.int32), n


def _flash_sweep(tables, n_stages, q_tile, k_tile, vt_tile, tile_bias, write_out, scratch):
    qi_ref, grp_ref = tables
    st_a, cm_a, st_b, cm_b, m_ref, l_ref, acc_ref, ot_ref = scratch
    L = MOBA_BLOCK
    key_minus_qry = (lax.broadcasted_iota(jnp.int32, (L, L), 0) - lax.broadcasted_iota(jnp.int32, (L, L), 1))

    def score_tile(i, blocks, g, hd, st_buf, cm_buf):
        j, _ = blocks[g]
        st = _nt_dot(k_tile(j), q_tile(i, hd))
        if g == KV_GROUP - 1:
            st = jnp.where(key_minus_qry <= (i - j) * L, st, NEG)
        st_buf[g, hd] = st
        cm_buf[g, hd] = jnp.max(st, axis=0, keepdims=True)

    def stage(s, cur, nxt):
        i, t = qi_ref[s], grp_ref[s]
        i_nxt = qi_ref[s + 1]
        blocks = _group_blocks(i, t)
        nxt_blocks = _group_blocks(i_nxt, grp_ref[s + 1])

        fresh = t == 0
        for hd in range(HEADS_PER_STEP):
            biases = [tile_bias(i, hd, j, real) for j, real in blocks]
            m_old = jnp.where(fresh, -jnp.inf, m_ref[hd])
            m_new = m_old
            for g in range(KV_GROUP):
                m_new = jnp.maximum(m_new, cur[1][g, hd] + biases[g])
            alpha = jnp.exp2(m_old - m_new)
            l_new = alpha * jnp.where(fresh, 0.0, l_ref[hd])
            pv = None
            for g, (j, _) in enumerate(blocks):
                score_tile(i_nxt, nxt_blocks, g, hd, *nxt)
                p = jnp.exp2(cur[0][g, hd] - (m_new - biases[g]))
                l_new = l_new + jnp.sum(p, axis=0, keepdims=True)
                d = jnp.dot(vt_tile(j, hd), p.astype(BF16), preferred_element_type=F32)
                pv = d if pv is None else pv + d
            acc_new = alpha * jnp.where(fresh, 0.0, acc_ref[hd]) + pv
            m_ref[hd] = m_new
            l_ref[hd] = l_new
            acc_ref[hd] = acc_new
            dv = acc_new.shape[0]
            ot_ref[i, hd * dv:(hd + 1) * dv, :] = acc_new / l_new

    bufs = ((st_a, cm_a), (st_b, cm_b))
    first = _group_blocks(qi_ref[0], grp_ref[0])
    for g in range(KV_GROUP):
        for hd in range(HEADS_PER_STEP):
            score_tile(qi_ref[0], first, g, hd, *bufs[0])

    def body(ss, carry):
        for k in range(STAGES_PER_TRIP):
            stage(STAGES_PER_TRIP * ss + k, bufs[k % 2], bufs[(k + 1) % 2])
        return carry

    n_trips = n_stages // STAGES_PER_TRIP
    lax.fori_loop(0, n_trips, body, 0)
    for s in range(n_trips * STAGES_PER_TRIP, n_stages):
        stage(s, bufs[s % 2], bufs[(s + 1) % 2])

    def emit(i, carry):
        write_out(i, ot_ref[i].T)
        return carry

    lax.fori_loop(0, ot_ref.shape[0], emit, 0)


def _attn_scratch(dv, nb):
    L = MOBA_BLOCK
    st = pltpu.VMEM((KV_GROUP, HEADS_PER_STEP, L, L), F32)
    cm = pltpu.VMEM((KV_GROUP, HEADS_PER_STEP, 1, L), F32)
    vec = pltpu.VMEM((HEADS_PER_STEP, 1, L), F32)
    return [st, cm, st, cm, vec, vec, pltpu.VMEM((HEADS_PER_STEP, dv, L), F32),
            pltpu.VMEM((nb, HEADS_PER_STEP * dv, L), F32)]


def _moba_kernel(qi_ref, grp_ref, q_ref, k_ref, vt_ref, kmean_ref, o_ref, qh_ref, bias_ref, *scratch,
                 n_stages):
    L = MOBA_BLOCK
    nb = kmean_ref.shape[1]
    lane = lax.broadcasted_iota(jnp.int32, (L, LANES), 1)
    kmean = kmean_ref[0, :, 0, :]
    blk = lax.broadcasted_iota(jnp.int32, (nb, L), 0)

    def select_blocks(i, carry):
        rows = pl.ds(pl.multiple_of(i * L, L), L)
        q = q_ref[0, rows, :]
        past = blk < i
        for hd in range(HEADS_PER_STEP):
            own = (lane >= hd * MOBA_HEAD_DIM) & (lane < (hd + 1) * MOBA_HEAD_DIM)
            qh = jnp.where(own, q, jnp.zeros_like(q))
            qh_ref[hd, rows, :] = qh
            gate = _nt_dot(kmean, qh.astype(F32), precision=lax.Precision.HIGHEST)
            gate = jnp.where(past, gate, -jnp.inf)
            bias = jnp.where(blk == i, 0.0, NEG)
            for _ in range(MOBA_TOPK):
                top = jnp.max(gate, axis=0, keepdims=True)
                at = blk == jnp.min(jnp.where(gate == top, blk, nb), axis=0, keepdims=True)
                bias = jnp.where(at & (top > -jnp.inf), 0.0, bias)
                gate = jnp.where(at, -jnp.inf, gate)
            bias_ref[hd, i] = bias
        return carry

    lax.fori_loop(0, nb, select_blocks, 0, unroll=4 if nb % 4 == 0 else 1)

    def write_out(i, o):
        o_ref[0, pl.ds(pl.multiple_of(i * L, L), L), :] = o

    _flash_sweep(
        (qi_ref, grp_ref), n_stages,
        q_tile=lambda i, hd: qh_ref[hd, pl.ds(pl.multiple_of(i * L, L), L), :],
        k_tile=lambda j: k_ref[0, pl.ds(pl.multiple_of(j * L, L), L), :],
        vt_tile=lambda j, hd: vt_ref[0, 0, j, pl.ds(hd * MOBA_HEAD_DIM, MOBA_HEAD_DIM), :],
        tile_bias=lambda i, hd, j, real: jnp.where(real, bias_ref[hd, i, pl.ds(j, 1), :], NEG),
        write_out=write_out, scratch=scratch)


def _moba(qa, ka, vat, kmean):
    B, S, _ = qa.shape
    L = MOBA_BLOCK
    nb = S // L
    qi, grp, n_stages = _stage_tables(nb)
    pair3 = lambda b, p, *_: (b, 0, p)
    grid_spec = pltpu.PrefetchScalarGridSpec(
        num_scalar_prefetch=2,
        grid=(B, N_PAIRS),
        in_specs=[
            pl.BlockSpec((1, S, LANES), pair3),
            pl.BlockSpec((1, S, LANES), pair3),
            pl.BlockSpec((1, 1, nb, LANES, L), lambda b, p, *_: (b, p, 0, 0, 0)),
            pl.BlockSpec((1, nb, 1, LANES), lambda b, p, *_: (b, 0, 0, p)),
        ],
        out_specs=pl.BlockSpec((1, S, LANES), pair3),
        scratch_shapes=[
            pltpu.VMEM((HEADS_PER_STEP, S, LANES), BF16),
            pltpu.VMEM((HEADS_PER_STEP, nb, nb, L), F32),
        ] + _attn_scratch(MOBA_HEAD_DIM, nb),
    )
    return pl.pallas_call(
        functools.partial(_moba_kernel, n_stages=n_stages),
        out_shape=jax.ShapeDtypeStruct((B, S, MOBA_WIDTH), F32),
        grid_spec=grid_spec,
        compiler_params=_cparams(("arbitrary", "arbitrary")),
        name="moba",
    )(jnp.asarray(qi), jnp.asarray(grp), qa, ka, vat, kmean)


def _mla_kernel(qi_ref, grp_ref, q_ref, k_ref, vt_ref, o_ref, *scratch, n_stages):
    L = MOBA_BLOCK

    def write_out(i, o):
        o_ref[0, pl.ds(pl.multiple_of(i * L, L), L), :] = o

    _flash_sweep(
        (qi_ref, grp_ref), n_stages,
        q_tile=lambda i, hd: q_ref[0, hd, pl.ds(pl.multiple_of(i * L, L), L), :],
        k_tile=lambda j: k_ref[0, 0, pl.ds(pl.multiple_of(j * L, L), L), :],
        vt_tile=lambda j, hd: vt_ref[0, 0, j, pl.ds(hd * MLA_V_DIM, MLA_V_DIM), :],
        tile_bias=lambda i, hd, j, real: jnp.where(real, 0.0, NEG),
        write_out=write_out, scratch=scratch)


def _mla(qcat, kcat, vmt):
    B, _, S, W = qcat.shape
    L = MOBA_BLOCK
    nb = S // L
    qi, grp, n_stages = _stage_tables(nb)
    grid_spec = pltpu.PrefetchScalarGridSpec(
        num_scalar_prefetch=2,
        grid=(B, N_PAIRS),
        in_specs=[
            pl.BlockSpec((1, HEADS_PER_STEP, S, W), lambda b, p, *_: (b, p, 0, 0)),
            pl.BlockSpec((1, 1, S, W), lambda b, p, *_: (b, p, 0, 0)),
            pl.BlockSpec((1, 1, nb, LANES, L), lambda b, p, *_: (b, p, 0, 0, 0)),
        ],
        out_specs=pl.BlockSpec((1, S, LANES), lambda b, p, *_: (b, 0, p)),
        scratch_shapes=_attn_scratch(MLA_V_DIM, nb),
    )
    return pl.pallas_call(
        functools.partial(_mla_kernel, n_stages=n_stages),
        out_shape=jax.ShapeDtypeStruct((B, S, MLA_WIDTH), F32),
        grid_spec=grid_spec,
        compiler_params=_cparams(("arbitrary", "arbitrary")),
        name="mla",
    )(jnp.asarray(qi), jnp.asarray(grp), qcat, kcat, vmt)


def _postattn_kernel(x_ref, om_ref, ol_ref, mod_ref, gom_ref, gol_ref, wo_ref, gpost_ref, gpre_ref,
                     wr_ref, br_ref, x1_ref, xl_ref, gl_ref, cnt_ref, ri_ref):
    tm = x_ref.shape[1]
    gt_a = mod_ref[0, 2:3, :]
    sh_f = mod_ref[0, 3:4, :]
    sc_f = mod_ref[0, 4:5, :]
    a = jnp.concatenate([_rms(om_ref[0], gom_ref[...]), _rms(ol_ref[0], gol_ref[...])], axis=-1)
    o = jnp.dot(a.astype(BF16), wo_ref[...], preferred_element_type=F32)
    x1 = x_ref[0] + gt_a * _rms(o, gpost_ref[...])
    x1_ref[0] = x1
    h2 = _rms(x1, gpre_ref[...]) * (1.0 + sc_f) + sh_f

    h2_hi = h2.astype(BF16)
    h2_lo = (h2 - h2_hi.astype(F32)).astype(BF16)
    logit = (jnp.dot(h2_hi, wr_ref[0], preferred_element_type=F32)
             + jnp.dot(h2_lo, wr_ref[0], preferred_element_type=F32)
             + jnp.dot(h2_hi, wr_ref[1], preferred_element_type=F32))
    biased = logit + br_ref[...]
    lane = lax.broadcasted_iota(jnp.int32, (tm, LANES), 1)
    big = jnp.int32(LANES)

    def first_argmax(v):
        mx = jnp.max(v, axis=-1, keepdims=True)
        return jnp.min(jnp.where(v == mx, lane, big), axis=-1, keepdims=True)

    def pick(v, idx):
        return jnp.sum(jnp.where(lane == idx, v, 0.0), axis=-1, keepdims=True)

    gmask = (lane >= N_EXPERTS) & (lane < N_EXPERTS + N_GROUPS)
    gl = jnp.where(gmask, logit, -jnp.inf)
    pg = jnp.exp(gl - jnp.max(gl, axis=-1, keepdims=True))
    g_lane = first_argmax(jnp.where(gmask, biased, -jnp.inf))
    g_prob = pick(pg, g_lane) / jnp.sum(pg, axis=-1, keepdims=True)
    e_lo = (g_lane - N_EXPERTS) * EXPERTS_PER_GROUP
    emask = (lane >= e_lo) & (lane < e_lo + EXPERTS_PER_GROUP)
    el = jnp.where(emask, logit, -jnp.inf)
    pe = jnp.exp(el - jnp.max(el, axis=-1, keepdims=True))
    eb = jnp.where(emask, biased, -jnp.inf)
    i1 = first_argmax(eb)
    i2 = first_argmax(jnp.where(lane == i1, -jnp.inf, eb))
    p1 = pick(pe, i1)
    p2 = pick(pe, i2)
    denom = p1 + p2
    w1 = p1 / denom * g_prob
    w2 = p2 / denom * g_prob

    hit1 = lane == i1
    hit2 = lane == i2
    onehot = jnp.where(hit1 | hit2, 1.0, 0.0).astype(BF16)
    tok_r = lax.broadcasted_iota(jnp.int32, (tm, tm), 0)
    tok_c = lax.broadcasted_iota(jnp.int32, (tm, tm), 1)
    csum = jnp.dot(jnp.where(tok_c <= tok_r, 1.0, 0.0).astype(BF16), onehot,
                   preferred_element_type=F32)
    cnt = csum[tm - 1:tm, :]
    seg = jnp.floor((cnt + (CHUNK - 1.0)) * (1.0 / CHUNK)) * CHUNK
    ex_r = lax.broadcasted_iota(jnp.int32, (LANES, LANES), 0)
    ex_c = lax.broadcasted_iota(jnp.int32, (LANES, LANES), 1)
    seg_start = jnp.dot(jnp.broadcast_to(seg, (SUBLANES, LANES)).astype(BF16),
                        jnp.where(ex_r < ex_c, 1.0, 0.0).astype(BF16),
                        preferred_element_type=F32)[0:1, :]
    place = seg_start + csum - 1.0
    slot1 = jnp.sum(jnp.where(hit1, place, 0.0), axis=-1, keepdims=True)
    slot2 = jnp.sum(jnp.where(hit2, place, 0.0), axis=-1, keepdims=True)
    ri_ref[0] = jnp.where(lane == 0, i1, jnp.where(lane == 1, i2, jnp.where(
        lane == 2, slot1.astype(jnp.int32), jnp.where(lane == 3, slot2.astype(jnp.int32), 0))))
    cnt_ref[0, 0] = cnt

    packed = jnp.where(lane == 0, slot1, jnp.where(lane == 1, slot2, jnp.where(
        lane == 2, w1, jnp.where(lane == 3, w2, 0.0))))
    packed_t = packed.T
    srow = lax.broadcasted_iota(jnp.int32, (LOCAL_ROWS, tm), 0).astype(F32)
    at1 = srow == packed_t[0:1, :]
    at2 = srow == packed_t[1:2, :]
    perm = jnp.where(at1 | at2, 1.0, 0.0).astype(BF16)
    xl_ref[0] = jnp.dot(perm, h2_hi, preferred_element_type=F32).astype(BF16)
    gate = jnp.sum(jnp.where(at1, packed_t[2:3, :], 0.0) + jnp.where(at2, packed_t[3:4, :], 0.0),
                   axis=-1, keepdims=True)
    gl_ref[0] = jnp.broadcast_to(gate, (LOCAL_ROWS, LANES))


def _postattn(x, o_moba, o_mla, mod3, g_om, g_ol, w_o, g_post, g_pre, w_r, b_r):
    B, S, D = x.shape
    tm = TM_IN
    assert S % tm == 0 and D == D_MODEL
    const2 = lambda b, i: (0, 0)
    tok3 = lambda b, i: (b, i, 0)
    n_j = S // tm
    return pl.pallas_call(
        _postattn_kernel,
        out_shape=[
            jax.ShapeDtypeStruct((B, S, D), F32),
            jax.ShapeDtypeStruct((B * n_j, LOCAL_ROWS, D), BF16),
            jax.ShapeDtypeStruct((B * n_j, LOCAL_ROWS, LANES), F32),
            jax.ShapeDtypeStruct((B * n_j, 1, 1, LANES), F32),
            jax.ShapeDtypeStruct((B, S, LANES), jnp.int32),
        ],
        grid=(B, S // tm),
        in_specs=[
            pl.BlockSpec((1, tm, D), tok3),
            pl.BlockSpec((1, tm, MOBA_WIDTH), tok3),
            pl.BlockSpec((1, tm, MLA_WIDTH), tok3),
            pl.BlockSpec((1, 6, D), lambda b, i: (b, 0, 0)),
            pl.BlockSpec((1, MOBA_WIDTH), const2),
            pl.BlockSpec((1, MLA_WIDTH), const2),
            pl.BlockSpec(w_o.shape, const2),
            pl.BlockSpec((1, D), const2),
            pl.BlockSpec((1, D), const2),
            pl.BlockSpec(w_r.shape, lambda b, i: (0, 0, 0)),
            pl.BlockSpec((1, LANES), const2),
        ],
        out_specs=[
            pl.BlockSpec((1, tm, D), tok3),
            pl.BlockSpec((1, LOCAL_ROWS, D), lambda b, i: (b * n_j + i, 0, 0)),
            pl.BlockSpec((1, LOCAL_ROWS, LANES), lambda b, i: (b * n_j + i, 0, 0)),
            pl.BlockSpec((1, 1, 1, LANES), lambda b, i: (b * n_j + i, 0, 0, 0)),
            pl.BlockSpec((1, tm, LANES), tok3),
        ],
        compiler_params=_cparams(("arbitrary", "arbitrary")),
        name="postattn",
    )(x, o_moba, o_mla, mod3, g_om, g_ol, w_o, g_post, g_pre, w_r, b_r)


def _start_chunk_gather(idx_ref, base, n_chunks, src_hbm, dst_ref, sem):
    for k in range(n_chunks):
        row = pl.multiple_of(idx_ref[base + k], CHUNK)
        pltpu.make_async_copy(src_hbm.at[pl.ds(row, CHUNK), :], dst_ref.at[pl.ds(k * CHUNK, CHUNK), :], sem).start()


def _wait_chunk_gather(n_chunks, src_hbm, dst_ref, sem):
    pltpu.make_async_copy(src_hbm.at[pl.ds(0, n_chunks * CHUNK), :], dst_ref, sem).wait()


def _moe_kernel(te_ref, nu_ref, src_ref, xl_hbm, gl_hbm, wgu_ref, wd_ref, y_ref, xbuf, gbuf, xsem, gsem):
    tm = y_ref.shape[0]
    n_chunks = tm // CHUNK
    i = pl.program_id(0)
    n_used = nu_ref[0]
    slot = i % 2

    def start(tile, s):
        _start_chunk_gather(src_ref, tile * n_chunks, n_chunks, xl_hbm, xbuf.at[s], xsem.at[s])
        _start_chunk_gather(src_ref, tile * n_chunks, n_chunks, gl_hbm, gbuf.at[s], gsem.at[s])

    @pl.when((i == 0) & (n_used > 0))
    def _():
        start(0, 0)

    @pl.when(i + 1 < n_used)
    def _():
        start(i + 1, 1 - slot)

    @pl.when(i < n_used)
    def _():
        _wait_chunk_gather(n_chunks, xl_hbm, xbuf.at[slot], xsem.at[slot])
        _wait_chunk_gather(n_chunks, gl_hbm, gbuf.at[slot], gsem.at[slot])
        gate = gbuf[slot]
        gu = jnp.dot(xbuf[slot], wgu_ref[0], preferred_element_type=F32)
        g = gu[:, :D_EXPERT]
        u = gu[:, D_EXPERT:]
        hid = g * jax.nn.sigmoid(g) * u
        y = jnp.dot(hid.astype(BF16), wd_ref[0], preferred_element_type=F32)
        y_ref[...] = (jnp.concatenate([gate] * (D_MODEL // LANES), axis=1) * y).astype(BF16)

    @pl.when(i >= n_used)
    def _():
        y_ref[...] = jnp.zeros_like(y_ref)


def _moe(tile_expert, n_used, chunk_src, x_local, g_local, w_gu, w_d):
    D = D_MODEL
    n_tiles = tile_expert.shape[0]
    tm = TM_MOE
    grid_spec = pltpu.PrefetchScalarGridSpec(
        num_scalar_prefetch=3,
        grid=(n_tiles,),
        in_specs=[
            pl.BlockSpec(memory_space=pl.ANY),
            pl.BlockSpec(memory_space=pl.ANY),
            pl.BlockSpec((1, D, 2 * D_EXPERT), lambda i, te, nu, src: (te[i], 0, 0)),
            pl.BlockSpec((1, D_EXPERT, D), lambda i, te, nu, src: (te[i], 0, 0)),
        ],
        out_specs=pl.BlockSpec((tm, D), lambda i, te, nu, src: (i, 0)),
        scratch_shapes=[pltpu.VMEM((2, tm, D), BF16), pltpu.VMEM((2, tm, LANES), F32),
                        pltpu.SemaphoreType.DMA((2,)), pltpu.SemaphoreType.DMA((2,))],
    )
    return pl.pallas_call(
        _moe_kernel,
        out_shape=jax.ShapeDtypeStruct((n_tiles * tm, D), BF16),
        grid_spec=grid_spec,
        compiler_params=_cparams(("arbitrary",)),
        name="moe",
    )(tile_expert, n_used, chunk_src, x_local, g_local, w_gu, w_d)


def _combine_kernel(src_ref, y_hbm, x1_ref, ri_ref, mod_ref, g_ref, o_ref, ybuf, sem):
    tm = x1_ref.shape[1]
    n_chunks = LOCAL_ROWS // CHUNK
    n_j = pl.num_programs(1)
    step = pl.program_id(0) * n_j + pl.program_id(1)
    n_steps = pl.num_programs(0) * n_j
    slot = step % 2

    @pl.when(step == 0)
    def _():
        _start_chunk_gather(src_ref, 0, n_chunks, y_hbm, ybuf.at[0], sem.at[0])

    @pl.when(step + 1 < n_steps)
    def _():
        _start_chunk_gather(src_ref, (step + 1) * n_chunks, n_chunks, y_hbm, ybuf.at[1 - slot], sem.at[1 - slot])

    _wait_chunk_gather(n_chunks, y_hbm, ybuf.at[slot], sem.at[slot])
    yl = ybuf[slot]
    col = lax.broadcasted_iota(jnp.int32, (tm, LOCAL_ROWS), 1)
    pick = jnp.where((col == ri_ref[0, :, 2:3]) | (col == ri_ref[0, :, 3:4]), 1.0, 0.0).astype(BF16)
    y = jnp.dot(pick, yl, preferred_element_type=F32)
    gt_f = mod_ref[0, 5:6, :]
    o_ref[0] = x1_ref[0] + gt_f * _rms(y, g_ref[...])


def _combine(chunk_src, y_sorted, x1, route_i, mod3, g_post):
    B, S, D = x1.shape
    tm = TM_IN
    grid_spec = pltpu.PrefetchScalarGridSpec(
        num_scalar_prefetch=1,
        grid=(B, S // tm),
        in_specs=[
            pl.BlockSpec(memory_space=pl.ANY),
            pl.BlockSpec((1, tm, D), lambda b, j, src: (b, j, 0)),
            pl.BlockSpec((1, tm, LANES), lambda b, j, src: (b, j, 0)),
            pl.BlockSpec((1, 6, D), lambda b, j, src: (b, 0, 0)),
            pl.BlockSpec((1, D), lambda b, j, src: (0, 0)),
        ],
        out_specs=pl.BlockSpec((1, tm, D), lambda b, j, src: (b, j, 0)),
        scratch_shapes=[pltpu.VMEM((2, LOCAL_ROWS, D), BF16), pltpu.SemaphoreType.DMA((2,))],
    )
    return pl.pallas_call(
        _combine_kernel,
        out_shape=jax.ShapeDtypeStruct((B, S, D), F32),
        grid_spec=grid_spec,
        compiler_params=_cparams(("arbitrary", "arbitrary")),
        name="combine",
    )(chunk_src, y_sorted, x1, route_i, mod3, g_post)


def _rope_angles(positions):
    def cos_sin(dim):
        half = dim // 2
        inv_freq = jnp.exp(-math.log(ROPE_THETA) * jnp.arange(half, dtype=F32) * (2.0 / dim))
        ang = positions.astype(F32)[:, None, :] * inv_freq[None, :, None]
        return [jnp.cos(ang), jnp.sin(ang)]
    B, S = positions.shape
    parts = cos_sin(MOBA_ROT_DIMS) + cos_sin(MLA_ROPE_DIM) + [jnp.ones((B, 1, S), F32)]
    used = MOBA_ROT_DIMS + MLA_ROPE_DIM + 1
    return jnp.concatenate(parts + [jnp.zeros((B, LANES - used, S), F32)], axis=1)


def _routing_metadata(counts, tm):
    i32 = jnp.int32
    n_tt = counts.shape[0]
    n_tiles = (2 * n_tt * TM_IN + n_tt * N_EXPERTS * (CHUNK - 1)) // tm + N_EXPERTS + 1
    seg = (counts + CHUNK - 1) // CHUNK * CHUNK
    loc_start = jnp.cumsum(seg, axis=1) - seg
    loc_total = jnp.sum(seg, axis=1)
    run = jnp.cumsum(seg, axis=0)
    rows_e = run[-1]
    tiles_per = (rows_e + tm - 1) // tm
    tile_end = jnp.cumsum(tiles_per)
    n_used = tile_end[-1]
    first_row = (tile_end - tiles_per) * tm
    tile_ids = jnp.arange(n_tiles, dtype=i32)
    te = jnp.minimum(jnp.sum((tile_ids[:, None] >= tile_end[None, :]).astype(i32), axis=1), N_EXPERTS - 1)
    last_e = jnp.sum(jnp.where(tile_ids == n_used - 1, te, 0))
    tile_expert = jnp.where(tile_ids < n_used, te, last_e).astype(i32)

    cpt = tm // CHUNK
    is_e = (tile_expert[:, None] == jnp.arange(N_EXPERTS, dtype=i32)[None, :]).astype(i32)
    of_tile = lambda a: jnp.sum(is_e[:, None, :] * a[None, :, :], axis=2)
    run_t, seg_t, loc_t = of_tile(run), of_tile(seg), of_tile(loc_start)
    first_t = jnp.sum(is_e * first_row[None, :], axis=1)
    rows_t = jnp.sum(is_e * rows_e[None, :], axis=1)
    off = (tile_ids * tm - first_t)[:, None] + jnp.arange(cpt, dtype=i32)[None, :] * CHUNK
    tau = jnp.minimum(jnp.sum((off[:, :, None] >= run_t[:, None, :]).astype(i32), axis=2), n_tt - 1)
    is_tau = (tau[:, :, None] == jnp.arange(n_tt, dtype=i32)[None, None, :]).astype(i32)
    at_tau = lambda a_t: jnp.sum(is_tau * a_t[:, None, :], axis=2)
    src = tau * LOCAL_ROWS + at_tau(loc_t) + off - (at_tau(run_t) - at_tau(seg_t))
    real = (tile_ids < n_used)[:, None] & (off < rows_t[:, None])
    zero_local = LOCAL_ROWS - CHUNK
    moe_src = jnp.where(real, src, zero_local).astype(i32).reshape(-1)

    lrow = jnp.arange(LOCAL_ROWS // CHUNK, dtype=i32) * CHUNK
    loc_end = loc_start + seg
    e_v = jnp.minimum(jnp.sum((lrow[None, :, None] >= loc_end[:, None, :]).astype(i32), axis=2), N_EXPERTS - 1)
    is_ev = (e_v[:, :, None] == jnp.arange(N_EXPERTS, dtype=i32)[None, None, :]).astype(i32)
    seg_first = first_row[None, :] + run - seg
    src_v = jnp.sum(is_ev * (seg_first - loc_start)[:, None, :], axis=2) + lrow[None, :]
    zero_sorted = (n_tiles - 1) * tm
    cmb_src = jnp.where(lrow[None, :] < loc_total[:, None], src_v, zero_sorted).astype(i32).reshape(-1)
    return tile_expert, n_used.reshape(1).astype(i32), moe_src, cmb_src


def kernel(x, c, positions, w_ada, b_ada, g_pre_attn, g_post_attn, g_pre_ffn, g_post_ffn, w_in, g_cq, w_uq,
           g_ckv, w_ukv, g_out_moba, g_out_mla, w_o, w_group, b_group, w_expert, b_expert, w_e_gate, w_e_up,
           w_e_down):
    B, S, D = x.shape
    depth = w_ada.shape[0]
    rope_cs = _rope_angles(positions)
    rope_exp = jnp.asarray(_rope_expansion(), BF16)
    row = lambda v: v.reshape(1, -1)

    for l in range(depth):
        n_main = 3 * MOBA_WIDTH + MLA_Q_RANK + MLA_KV_RANK
        w_in_p = jnp.concatenate([w_in[l][:, :n_main], jnp.tile(w_in[l][:, n_main:], (1, LANES // MLA_ROPE_DIM))],
                                 axis=1).astype(BF16)
        wq = w_uq[l].reshape(MLA_Q_RANK, MLA_HEADS, MLA_NOPE_DIM + MLA_ROPE_DIM)
        wq_n = wq[:, :, :MLA_NOPE_DIM]
        zero_n = jnp.zeros_like(wq_n)
        even = (jnp.arange(MLA_HEADS) % 2 == 0)[None, :, None]
        w_uq_n = jnp.concatenate([jnp.where(even, wq_n, zero_n), jnp.where(even, zero_n, wq_n)], axis=-1)
        w_uq_n = w_uq_n.reshape(MLA_Q_RANK, MLA_HEADS * LANES).astype(BF16)
        w_uq_p = wq[:, :, MLA_NOPE_DIM:].reshape(MLA_Q_RANK, MLA_HEADS * MLA_ROPE_DIM).astype(BF16)
        wkv = w_ukv[l].reshape(MLA_KV_RANK, MLA_HEADS, MLA_NOPE_DIM + MLA_V_DIM)
        w_ukv_p = jnp.concatenate([wkv[:, :, :MLA_NOPE_DIM].reshape(MLA_KV_RANK, -1),
                                   wkv[:, :, MLA_NOPE_DIM:].reshape(MLA_KV_RANK, -1)], axis=1).astype(BF16)
        w_r = jnp.concatenate([w_expert[l], w_group[l],
                               jnp.zeros((D, LANES - N_EXPERTS - N_GROUPS), F32)], axis=1)
        w_r_hi = w_r.astype(BF16)
        w_r = jnp.stack([w_r_hi, (w_r - w_r_hi.astype(F32)).astype(BF16)])
        b_r = jnp.concatenate([b_expert[l], b_group[l], jnp.zeros((LANES - N_EXPERTS - N_GROUPS,), F32)]).reshape(1, LANES)
        w_gu = jnp.concatenate([w_e_gate[l], w_e_up[l]], axis=-1).astype(BF16)
        w_d = w_e_down[l].astype(BF16)

        mod3 = _adaln(c, w_ada[l], b_ada[l]).reshape(B, 6, D)

        qa, ka, vat, kmean, qcat, kcat, vmt = _inproj(
            x, mod3, row(g_pre_attn[l]), w_in_p, rope_cs, rope_exp, row(g_cq[l]), w_uq_n, w_uq_p,
            row(g_ckv[l]), w_ukv_p)
        o_moba = _moba(qa, ka, vat, kmean)
        o_mla = _mla(qcat, kcat, vmt)
        x1, x_local, g_local, counts, route_i = _postattn(
            x, o_moba, o_mla, mod3, row(g_out_moba[l]), row(g_out_mla[l]), w_o[l].astype(BF16),
            row(g_post_attn[l]), row(g_pre_ffn[l]), w_r, b_r)

        counts = counts.reshape(-1, LANES)[:, :N_EXPERTS].astype(jnp.int32)
        tile_expert, n_used, moe_src, cmb_src = _routing_metadata(counts, TM_MOE)
        y_sorted = _moe(tile_expert, n_used, moe_src, x_local.reshape(-1, D), g_local.reshape(-1, LANES), w_gu, w_d)
        x = _combine(cmb_src, y_sorted, x1, route_i, mod3, row(g_post_ffn[l]))
    return x
```

```python
import functools
import math

import jax
import jax.numpy as jnp
import numpy as np
from jax import lax
from jax.experimental import pallas as pl
from jax.experimental.pallas import tpu as pltpu

F32 = jnp.float32
BF16 = jnp.bfloat16

D_MODEL = 1024
MOBA_HEADS = 8
MOBA_HEAD_DIM = 64
MOBA_WIDTH = MOBA_HEADS * MOBA_HEAD_DIM
MOBA_BLOCK = 256
MOBA_TOPK = 3
MOBA_ROT_DIMS = MOBA_HEAD_DIM // 4
MLA_HEADS = 8
MLA_Q_RANK = 256
MLA_KV_RANK = 128
MLA_NOPE_DIM = 64
MLA_ROPE_DIM = 32
MLA_V_DIM = 64
MLA_WIDTH = MLA_HEADS * MLA_V_DIM
ROPE_THETA = 500000.0
N_GROUPS = 4
EXPERTS_PER_GROUP = 8
N_EXPERTS = N_GROUPS * EXPERTS_PER_GROUP
D_EXPERT = 256
EPS = 1e-6
NEG = -1e30
LOG2E = math.log2(math.e)

LANES = 128
SUBLANES = 8
HEADS_PER_STEP = LANES // MOBA_HEAD_DIM
N_PAIRS = MOBA_HEADS // HEADS_PER_STEP
VMEM_LIMIT = 56 * 1024 * 1024

ADALN_COLS = 1536
TM_IN = 512
KV_GROUP = 4
STAGES_PER_TRIP = 2
TM_MOE = 512
CHUNK = 16
LOCAL_ROWS = 2 * TM_IN + N_EXPERTS * CHUNK


def _cparams(sem):
    return pltpu.CompilerParams(dimension_semantics=sem, vmem_limit_bytes=VMEM_LIMIT)


def _rms(x, g):
    return x * lax.rsqrt(jnp.mean(x * x, axis=-1, keepdims=True) + EPS) * g


def _adaln_kernel(c_ref, w_ref, b_ref, o_ref):
    c = c_ref[...]
    ca = c * jax.nn.sigmoid(c)
    o_ref[...] = jnp.dot(ca, w_ref[...], preferred_element_type=F32,
                         precision=lax.Precision.HIGHEST) + b_ref[...]


def _adaln(c, w, b):
    B, D = c.shape
    N = w.shape[1]
    tn = ADALN_COLS if N % ADALN_COLS == 0 else N
    return pl.pallas_call(
        _adaln_kernel,
        out_shape=jax.ShapeDtypeStruct((B, N), F32),
        grid=(N // tn,),
        in_specs=[pl.BlockSpec((B, D), lambda j: (0, 0)),
                  pl.BlockSpec((D, tn), lambda j: (0, j)),
                  pl.BlockSpec((1, tn), lambda j: (0, j))],
        out_specs=pl.BlockSpec((B, tn), lambda j: (0, j)),
        compiler_params=_cparams(("arbitrary",)),
        name="adaln",
    )(c, w, b.reshape(1, N))


def _rope128(xc, tab, shift):
    c = tab[:, 0:LANES]
    s1 = tab[:, LANES:2 * LANES]
    s2 = tab[:, 2 * LANES:3 * LANES]
    return xc * c + pltpu.roll(xc, LANES - shift, axis=1) * s1 + pltpu.roll(xc, shift, axis=1) * s2


def _rope_expansion():
    e = np.zeros((LANES, 6 * LANES), np.float32)
    ha, hm = MOBA_ROT_DIMS // 2, MLA_ROPE_DIM // 2
    cos_a, sin_a, cos_m, sin_m, one = 0, ha, 2 * ha, 2 * ha + hm, 2 * ha + 2 * hm
    for lane in range(LANES):
        d = lane % MOBA_HEAD_DIM
        if d < ha:
            e[cos_a + d, lane] = 1.0
            e[sin_a + d, LANES + lane] = -1.0
        elif d < 2 * ha:
            e[cos_a + d - ha, lane] = 1.0
            e[sin_a + d - ha, 2 * LANES + lane] = 1.0
        else:
            e[one, lane] = 1.0
        d = lane % MLA_ROPE_DIM
        e[cos_m + d % hm, 3 * LANES + lane] = 1.0
        if d < hm:
            e[sin_m + d, 4 * LANES + lane] = -1.0
        else:
            e[sin_m + d - hm, 5 * LANES + lane] = 1.0
    return e


def _expand_exact(v, e_ref):
    hi = v.astype(BF16)
    r1 = v - hi.astype(F32)
    mid = r1.astype(BF16)
    lo = (r1 - mid.astype(F32)).astype(BF16)
    e = e_ref[...]
    return (jnp.dot(hi, e, preferred_element_type=F32) + jnp.dot(mid, e, preferred_element_type=F32)
            + jnp.dot(lo, e, preferred_element_type=F32))


def _store_vt(src, col0, vt_ref):
    for p in range(N_PAIRS):
        for r in range(src.shape[0] // MOBA_BLOCK):
            blk = src[r * MOBA_BLOCK:(r + 1) * MOBA_BLOCK, col0 + p * LANES:col0 + (p + 1) * LANES]
            vt_ref[0, p, r] = blk.T.astype(BF16)


def _inproj_kernel(x_ref, mod_ref, gpre_ref, win_ref, cs_ref, rexp_ref, gcq_ref, wuqn_ref, wuqp_ref,
                   gckv_ref, wukv_ref,
                   qa_ref, ka_ref, vat_ref, kmean_ref, qcat_ref, kcat_ref, vmt_ref):
    tm = x_ref.shape[1]
    rope_tab = _expand_exact(cs_ref[0].T, rexp_ref)
    tab_a = rope_tab[:, 0:3 * LANES]
    tab_m = rope_tab[:, 3 * LANES:6 * LANES]
    x = x_ref[0]
    sh = mod_ref[0, 0:1, :]
    sc = mod_ref[0, 1:2, :]
    h = _rms(x, gpre_ref[...]) * (1.0 + sc) + sh
    z = jnp.dot(h.astype(BF16), win_ref[...], preferred_element_type=F32)

    moba_scale = MOBA_HEAD_DIM ** -0.5 * LOG2E
    half_a = MOBA_ROT_DIMS // 2
    n_chunks = MOBA_WIDTH // LANES
    for cidx in range(n_chunks):
        lo = cidx * LANES
        q = _rope128(z[:, lo:lo + LANES], tab_a, half_a)
        qa_ref[0, :, lo:lo + LANES] = (q * moba_scale).astype(BF16)
        k = _rope128(z[:, MOBA_WIDTH + lo:MOBA_WIDTH + lo + LANES], tab_a, half_a)
        ka_ref[0, :, lo:lo + LANES] = k.astype(BF16)
        for r in range(tm // MOBA_BLOCK):
            kmean_ref[0, r, :, lo:lo + LANES] = jnp.mean(
                k[r * MOBA_BLOCK:(r + 1) * MOBA_BLOCK], axis=0, keepdims=True)
    _store_vt(z, 2 * MOBA_WIDTH, vat_ref)

    o_cq = 3 * MOBA_WIDTH
    o_ckv = o_cq + MLA_Q_RANK
    o_kpe = o_ckv + MLA_KV_RANK
    mla_scale = (MLA_NOPE_DIM + MLA_ROPE_DIM) ** -0.5 * LOG2E
    half_m = MLA_ROPE_DIM // 2

    cqn = _rms(z[:, o_cq:o_cq + MLA_Q_RANK], gcq_ref[...]).astype(BF16)
    qn = jnp.dot(cqn, wuqn_ref[...], preferred_element_type=F32) * mla_scale
    qp = jnp.dot(cqn, wuqp_ref[...], preferred_element_type=F32)
    lane = lax.broadcasted_iota(jnp.int32, (tm, LANES), 1)
    heads_per_chunk = LANES // MLA_ROPE_DIM
    qp_chunks = [_rope128(qp[:, g * LANES:(g + 1) * LANES], tab_m, half_m) * mla_scale
                 for g in range(MLA_HEADS // heads_per_chunk)]
    for hd in range(MLA_HEADS):
        qcat_ref[0, hd, :, 0:LANES] = qn[:, hd * LANES:(hd + 1) * LANES].astype(BF16)
        sub = hd % heads_per_chunk
        keep = (lane >= sub * MLA_ROPE_DIM) & (lane < (sub + 1) * MLA_ROPE_DIM)
        qcat_ref[0, hd, :, LANES:2 * LANES] = jnp.where(keep, qp_chunks[hd // heads_per_chunk], 0.0).astype(BF16)

    ckvn = _rms(z[:, o_ckv:o_ckv + MLA_KV_RANK], gckv_ref[...]).astype(BF16)
    kv = jnp.dot(ckvn, wukv_ref[...], preferred_element_type=F32)
    kpe = _rope128(z[:, o_kpe:o_kpe + LANES], tab_m, half_m).astype(BF16)
    for p in range(N_PAIRS):
        kcat_ref[0, p, :, 0:LANES] = kv[:, p * LANES:(p + 1) * LANES].astype(BF16)
        kcat_ref[0, p, :, LANES:2 * LANES] = kpe
    _store_vt(kv, MLA_WIDTH, vmt_ref)


def _inproj(x, mod3, g_pre, w_in_p, rope_cs, rope_exp, g_cq, w_uq_n, w_uq_p, g_ckv, w_ukv_p):
    B, S, D = x.shape
    tm = min(TM_IN, S)
    nb = S // MOBA_BLOCK
    rb = tm // MOBA_BLOCK
    const2 = lambda b, i: (0, 0)
    tok3 = lambda b, i: (b, i, 0)
    outs = pl.pallas_call(
        _inproj_kernel,
        out_shape=[
            jax.ShapeDtypeStruct((B, S, MOBA_WIDTH), BF16),
            jax.ShapeDtypeStruct((B, S, MOBA_WIDTH), BF16),
            jax.ShapeDtypeStruct((B, N_PAIRS, nb, LANES, MOBA_BLOCK), BF16),
            jax.ShapeDtypeStruct((B, nb, 1, MOBA_WIDTH), F32),
            jax.ShapeDtypeStruct((B, MLA_HEADS, S, 2 * LANES), BF16),
            jax.ShapeDtypeStruct((B, N_PAIRS, S, 2 * LANES), BF16),
            jax.ShapeDtypeStruct((B, N_PAIRS, nb, LANES, MOBA_BLOCK), BF16),
        ],
        grid=(B, S // tm),
        in_specs=[
            pl.BlockSpec((1, tm, D), tok3),
            pl.BlockSpec((1, 6, D), lambda b, i: (b, 0, 0)),
            pl.BlockSpec((1, D), const2),
            pl.BlockSpec(w_in_p.shape, const2),
            pl.BlockSpec((1, LANES, tm), lambda b, i: (b, 0, i)),
            pl.BlockSpec(rope_exp.shape, const2),
            pl.BlockSpec((1, MLA_Q_RANK), const2),
            pl.BlockSpec(w_uq_n.shape, const2),
            pl.BlockSpec(w_uq_p.shape, const2),
            pl.BlockSpec((1, MLA_KV_RANK), const2),
            pl.BlockSpec(w_ukv_p.shape, const2),
        ],
        out_specs=[
            pl.BlockSpec((1, tm, MOBA_WIDTH), tok3),
            pl.BlockSpec((1, tm, MOBA_WIDTH), tok3),
            pl.BlockSpec((1, N_PAIRS, rb, LANES, MOBA_BLOCK), lambda b, i: (b, 0, i, 0, 0)),
            pl.BlockSpec((1, rb, 1, MOBA_WIDTH), lambda b, i: (b, i, 0, 0)),
            pl.BlockSpec((1, MLA_HEADS, tm, 2 * LANES), lambda b, i: (b, 0, i, 0)),
            pl.BlockSpec((1, N_PAIRS, tm, 2 * LANES), lambda b, i: (b, 0, i, 0)),
            pl.BlockSpec((1, N_PAIRS, rb, LANES, MOBA_BLOCK), lambda b, i: (b, 0, i, 0, 0)),
        ],
        compiler_params=_cparams(("arbitrary", "arbitrary")),
        name="inproj",
    )(x, mod3, g_pre, w_in_p, rope_cs, rope_exp, g_cq, w_uq_n, w_uq_p, g_ckv, w_ukv_p)
    return outs


def _nt_dot(a, b, **kw):
    return lax.dot_general(a, b, (((1,), (1,)), ((), ())), preferred_element_type=F32, **kw)


def _group_blocks(i, t):
    out = []
    for g in range(KV_GROUP):
        j = i - KV_GROUP * t - (KV_GROUP - 1 - g)
        out.append((jnp.maximum(j, 0), j >= 0))
    return out


def _stage_tables(nb):
    qi, grp = [], []
    for i in range(nb):
        for t in range((i + KV_GROUP) // KV_GROUP):
            qi.append(i)
            grp.append(t)
    n = len(qi)
    qi.append(qi[-1])
    grp.append(grp[-1])
    return np.asarray(qi, np.int32), np.asarray(grp, np.int32), n


def _flash_sweep(tables, n_stages, q_tile, k_tile, vt_tile, tile_bias, write_out, scratch):
    qi_ref, grp_ref = tables
    st_a, cm_a, st_b, cm_b, m_ref, l_ref, acc_ref, ot_ref = scratch
    L = MOBA_BLOCK
    key_minus_qry = (lax.broadcasted_iota(jnp.int32, (L, L), 0) - lax.broadcasted_iota(jnp.int32, (L, L), 1))

    def score_tile(i, blocks, g, hd, st_buf, cm_buf):
        j, _ = blocks[g]
        st = _nt_dot(k_tile(j), q_tile(i, hd))
        if g == KV_GROUP - 1:
            st = jnp.where(key_minus_qry <= (i - j) * L, st, NEG)
        st_buf[g, hd] = st
        cm_buf[g, hd] = jnp.max(st, axis=0, keepdims=True)

    def stage(s, cur, nxt):
        i, t = qi_ref[s], grp_ref[s]
        i_nxt = qi_ref[s + 1]
        blocks = _group_blocks(i, t)
        nxt_blocks = _group_blocks(i_nxt, grp_ref[s + 1])

        fresh = t == 0
        for hd in range(HEADS_PER_STEP):
            biases = [tile_bias(i, hd, j, real) for j, real in blocks]
            m_old = jnp.where(fresh, -jnp.inf, m_ref[hd])
            m_new = m_old
            for g in range(KV_GROUP):
                m_new = jnp.maximum(m_new, cur[1][g, hd] + biases[g])
            alpha = jnp.exp2(m_old - m_new)
            l_new = alpha * jnp.where(fresh, 0.0, l_ref[hd])
            pv = None
            for g, (j, _) in enumerate(blocks):
                score_tile(i_nxt, nxt_blocks, g, hd, *nxt)
                p = jnp.exp2(cur[0][g, hd] - (m_new - biases[g]))
                l_new = l_new + jnp.sum(p, axis=0, keepdims=True)
                d = jnp.dot(vt_tile(j, hd), p.astype(BF16), preferred_element_type=F32)
                pv = d if pv is None else pv + d
            acc_new = alpha * jnp.where(fresh, 0.0, acc_ref[hd]) + pv
            m_ref[hd] = m_new
            l_ref[hd] = l_new
            acc_ref[hd] = acc_new
            dv = acc_new.shape[0]
            ot_ref[i, hd * dv:(hd + 1) * dv, :] = acc_new / l_new

    m_ref[...] = jnp.full(m_ref.shape, -jnp.inf, F32)
    l_ref[...] = jnp.zeros(l_ref.shape, F32)
    acc_ref[...] = jnp.zeros(acc_ref.shape, F32)
    bufs = ((st_a, cm_a), (st_b, cm_b))
    first = _group_blocks(qi_ref[0], grp_ref[0])
    for g in range(KV_GROUP):
        for hd in range(HEADS_PER_STEP):
            score_tile(qi_ref[0], first, g, hd, *bufs[0])

    def body(ss, carry):
        for k in range(STAGES_PER_TRIP):
            stage(STAGES_PER_TRIP * ss + k, bufs[k % 2], bufs[(k + 1) % 2])
        return carry

    n_trips = n_stages // STAGES_PER_TRIP
    lax.fori_loop(0, n_trips, body, 0)
    for s in range(n_trips * STAGES_PER_TRIP, n_stages):
        stage(s, bufs[s % 2], bufs[(s + 1) % 2])

    def emit(i, carry):
        write_out(i, ot_ref[i].T)
        return carry

    lax.fori_loop(0, ot_ref.shape[0], emit, 0)


def _attn_scratch(dv, nb):
    L = MOBA_BLOCK
    st = pltpu.VMEM((KV_GROUP, HEADS_PER_STEP, L, L), F32)
    cm = pltpu.VMEM((KV_GROUP, HEADS_PER_STEP, 1, L), F32)
    vec = pltpu.VMEM((HEADS_PER_STEP, 1, L), F32)
    return [st, cm, st, cm, vec, vec, pltpu.VMEM((HEADS_PER_STEP, dv, L), F32),
            pltpu.VMEM((nb, HEADS_PER_STEP * dv, L), F32)]


def _moba_kernel(qi_ref, grp_ref, q_ref, k_ref, vt_ref, kmean_ref, o_ref, qh_ref, bias_ref, *scratch,
                 n_stages):
    L = MOBA_BLOCK
    nb = kmean_ref.shape[1]
    lane = lax.broadcasted_iota(jnp.int32, (L, LANES), 1)
    kmean = kmean_ref[0, :, 0, :]
    blk = lax.broadcasted_iota(jnp.int32, (nb, L), 0)

    def select_blocks(i, carry):
        rows = pl.ds(pl.multiple_of(i * L, L), L)
        q = q_ref[0, rows, :]
        past = blk < i
        for hd in range(HEADS_PER_STEP):
            own = (lane >= hd * MOBA_HEAD_DIM) & (lane < (hd + 1) * MOBA_HEAD_DIM)
            qh = jnp.where(own, q, jnp.zeros_like(q))
            qh_ref[hd, rows, :] = qh
            gate = _nt_dot(kmean, qh.astype(F32), precision=lax.Precision.HIGHEST)
            gate = jnp.where(past, gate, -jnp.inf)
            bias = jnp.where(blk == i, 0.0, NEG)
            for _ in range(MOBA_TOPK):
                top = jnp.max(gate, axis=0, keepdims=True)
                at = blk == jnp.min(jnp.where(gate == top, blk, nb), axis=0, keepdims=True)
                bias = jnp.where(at & (top > -jnp.inf), 0.0, bias)
                gate = jnp.where(at, -jnp.inf, gate)
            bias_ref[hd, i] = bias
        return carry

    lax.fori_loop(0, nb, select_blocks, 0, unroll=4 if nb % 4 == 0 else 1)

    def write_out(i, o):
        o_ref[0, pl.ds(pl.multiple_of(i * L, L), L), :] = o

    _flash_sweep(
        (qi_ref, grp_ref), n_stages,
        q_tile=lambda i, hd: qh_ref[hd, pl.ds(pl.multiple_of(i * L, L), L), :],
        k_tile=lambda j: k_ref[0, pl.ds(pl.multiple_of(j * L, L), L), :],
        vt_tile=lambda j, hd: vt_ref[0, 0, j, pl.ds(hd * MOBA_HEAD_DIM, MOBA_HEAD_DIM), :],
        tile_bias=lambda i, hd, j, real: jnp.where(real, bias_ref[hd, i, pl.ds(j, 1), :], NEG),
        write_out=write_out, scratch=scratch)


def _moba(qa, ka, vat, kmean):
    B, S, _ = qa.shape
    L = MOBA_BLOCK
    nb = S // L
    qi, grp, n_stages = _stage_tables(nb)
    pair3 = lambda b, p, *_: (b, 0, p)
    grid_spec = pltpu.PrefetchScalarGridSpec(
        num_scalar_prefetch=2,
        grid=(B, N_PAIRS),
        in_specs=[
            pl.BlockSpec((1, S, LANES), pair3),
            pl.BlockSpec((1, S, LANES), pair3),
            pl.BlockSpec((1, 1, nb, LANES, L), lambda b, p, *_: (b, p, 0, 0, 0)),
            pl.BlockSpec((1, nb, 1, LANES), lambda b, p, *_: (b, 0, 0, p)),
        ],
        out_specs=pl.BlockSpec((1, S, LANES), pair3),
        scratch_shapes=[
            pltpu.VMEM((HEADS_PER_STEP, S, LANES), BF16),
            pltpu.VMEM((HEADS_PER_STEP, nb, nb, L), F32),
        ] + _attn_scratch(MOBA_HEAD_DIM, nb),
    )
    return pl.pallas_call(
        functools.partial(_moba_kernel, n_stages=n_stages),
        out_shape=jax.ShapeDtypeStruct((B, S, MOBA_WIDTH), F32),
        grid_spec=grid_spec,
        compiler_params=_cparams(("arbitrary", "arbitrary")),
        name="moba",
    )(jnp.asarray(qi), jnp.asarray(grp), qa, ka, vat, kmean)


def _mla_kernel(qi_ref, grp_ref, q_ref, k_ref, vt_ref, o_ref, *scratch, n_stages):
    L = MOBA_BLOCK

    def write_out(i, o):
        o_ref[0, pl.ds(pl.multiple_of(i * L, L), L), :] = o

    _flash_sweep(
        (qi_ref, grp_ref), n_stages,
        q_tile=lambda i, hd: q_ref[0, hd, pl.ds(pl.multiple_of(i * L, L), L), :],
        k_tile=lambda j: k_ref[0, 0, pl.ds(pl.multiple_of(j * L, L), L), :],
        vt_tile=lambda j, hd: vt_ref[0, 0, j, pl.ds(hd * MLA_V_DIM, MLA_V_DIM), :],
        tile_bias=lambda i, hd, j, real: jnp.where(real, 0.0, NEG),
        write_out=write_out, scratch=scratch)


def _mla(qcat, kcat, vmt):
    B, _, S, W = qcat.shape
    L = MOBA_BLOCK
    nb = S // L
    qi, grp, n_stages = _stage_tables(nb)
    grid_spec = pltpu.PrefetchScalarGridSpec(
        num_scalar_prefetch=2,
        grid=(B, N_PAIRS),
        in_specs=[
            pl.BlockSpec((1, HEADS_PER_STEP, S, W), lambda b, p, *_: (b, p, 0, 0)),
            pl.BlockSpec((1, 1, S, W), lambda b, p, *_: (b, p, 0, 0)),
            pl.BlockSpec((1, 1, nb, LANES, L), lambda b, p, *_: (b, p, 0, 0, 0)),
        ],
        out_specs=pl.BlockSpec((1, S, LANES), lambda b, p, *_: (b, 0, p)),
        scratch_shapes=_attn_scratch(MLA_V_DIM, nb),
    )
    return pl.pallas_call(
        functools.partial(_mla_kernel, n_stages=n_stages),
        out_shape=jax.ShapeDtypeStruct((B, S, MLA_WIDTH), F32),
        grid_spec=grid_spec,
        compiler_params=_cparams(("arbitrary", "arbitrary")),
        name="mla",
    )(jnp.asarray(qi), jnp.asarray(grp), qcat, kcat, vmt)


def _postattn_kernel(x_ref, om_ref, ol_ref, mod_ref, gom_ref, gol_ref, wo_ref, gpost_ref, gpre_ref,
                     wr_ref, br_ref, x1_ref, xl_ref, gl_ref, cnt_ref, ri_ref):
    tm = x_ref.shape[1]
    gt_a = mod_ref[0, 2:3, :]
    sh_f = mod_ref[0, 3:4, :]
    sc_f = mod_ref[0, 4:5, :]
    a = jnp.concatenate([_rms(om_ref[0], gom_ref[...]), _rms(ol_ref[0], gol_ref[...])], axis=-1)
    o = jnp.dot(a.astype(BF16), wo_ref[...], preferred_element_type=F32)
    x1 = x_ref[0] + gt_a * _rms(o, gpost_ref[...])
    x1_ref[0] = x1
    h2 = _rms(x1, gpre_ref[...]) * (1.0 + sc_f) + sh_f

    h2_hi = h2.astype(BF16)
    h2_lo = (h2 - h2_hi.astype(F32)).astype(BF16)
    logit = (jnp.dot(h2_hi, wr_ref[0], preferred_element_type=F32)
             + jnp.dot(h2_lo, wr_ref[0], preferred_element_type=F32)
             + jnp.dot(h2_hi, wr_ref[1], preferred_element_type=F32))
    biased = logit + br_ref[...]
    lane = lax.broadcasted_iota(jnp.int32, (tm, LANES), 1)
    big = jnp.int32(LANES)

    def first_argmax(v):
        mx = jnp.max(v, axis=-1, keepdims=True)
        return jnp.min(jnp.where(v == mx, lane, big), axis=-1, keepdims=True)

    def pick(v, idx):
        return jnp.sum(jnp.where(lane == idx, v, 0.0), axis=-1, keepdims=True)

    gmask = (lane >= N_EXPERTS) & (lane < N_EXPERTS + N_GROUPS)
    gl = jnp.where(gmask, logit, -jnp.inf)
    pg = jnp.exp(gl - jnp.max(gl, axis=-1, keepdims=True))
    g_lane = first_argmax(jnp.where(gmask, biased, -jnp.inf))
    g_prob = pick(pg, g_lane) / jnp.sum(pg, axis=-1, keepdims=True)
    e_lo = (g_lane - N_EXPERTS) * EXPERTS_PER_GROUP
    emask = (lane >= e_lo) & (lane < e_lo + EXPERTS_PER_GROUP)
    el = jnp.where(emask, logit, -jnp.inf)
    pe = jnp.exp(el - jnp.max(el, axis=-1, keepdims=True))
    eb = jnp.where(emask, biased, -jnp.inf)
    i1 = first_argmax(eb)
    i2 = first_argmax(jnp.where(lane == i1, -jnp.inf, eb))
    p1 = pick(pe, i1)
    p2 = pick(pe, i2)
    denom = p1 + p2
    w1 = p1 / denom * g_prob
    w2 = p2 / denom * g_prob

    hit1 = lane == i1
    hit2 = lane == i2
    onehot = jnp.where(hit1 | hit2, 1.0, 0.0).astype(BF16)
    tok_r = lax.broadcasted_iota(jnp.int32, (tm, tm), 0)
    tok_c = lax.broadcasted_iota(jnp.int32, (tm, tm), 1)
    csum = jnp.dot(jnp.where(tok_c <= tok_r, 1.0, 0.0).astype(BF16), onehot,
                   preferred_element_type=F32)
    cnt = csum[tm - 1:tm, :]
    seg = jnp.floor((cnt + (CHUNK - 1.0)) * (1.0 / CHUNK)) * CHUNK
    ex_r = lax.broadcasted_iota(jnp.int32, (LANES, LANES), 0)
    ex_c = lax.broadcasted_iota(jnp.int32, (LANES, LANES), 1)
    seg_start = jnp.dot(jnp.broadcast_to(seg, (SUBLANES, LANES)).astype(BF16),
                        jnp.where(ex_r < ex_c, 1.0, 0.0).astype(BF16),
                        preferred_element_type=F32)[0:1, :]
    place = seg_start + csum - 1.0
    slot1 = jnp.sum(jnp.where(hit1, place, 0.0), axis=-1, keepdims=True)
    slot2 = jnp.sum(jnp.where(hit2, place, 0.0), axis=-1, keepdims=True)
    ri_ref[0] = jnp.where(lane == 0, i1, jnp.where(lane == 1, i2, jnp.where(
        lane == 2, slot1.astype(jnp.int32), jnp.where(lane == 3, slot2.astype(jnp.int32), 0))))
    cnt_ref[0, 0] = cnt

    packed = jnp.where(lane == 0, slot1, jnp.where(lane == 1, slot2, jnp.where(
        lane == 2, w1, jnp.where(lane == 3, w2, 0.0))))
    packed_t = packed.T
    srow = lax.broadcasted_iota(jnp.int32, (LOCAL_ROWS, tm), 0).astype(F32)
    at1 = srow == packed_t[0:1, :]
    at2 = srow == packed_t[1:2, :]
    perm = jnp.where(at1 | at2, 1.0, 0.0).astype(BF16)
    xl_ref[0] = jnp.dot(perm, h2_hi, preferred_element_type=F32).astype(BF16)
    gate = jnp.sum(jnp.where(at1, packed_t[2:3, :], 0.0) + jnp.where(at2, packed_t[3:4, :], 0.0),
                   axis=-1, keepdims=True)
    gl_ref[0] = jnp.broadcast_to(gate, (LOCAL_ROWS, LANES))


def _postattn(x, o_moba, o_mla, mod3, g_om, g_ol, w_o, g_post, g_pre, w_r, b_r):
    B, S, D = x.shape
    tm = TM_IN
    assert S % tm == 0 and D == D_MODEL
    const2 = lambda b, i: (0, 0)
    tok3 = lambda b, i: (b, i, 0)
    n_j = S // tm
    return pl.pallas_call(
        _postattn_kernel,
        out_shape=[
            jax.ShapeDtypeStruct((B, S, D), F32),
            jax.ShapeDtypeStruct((B * n_j, LOCAL_ROWS, D), BF16),
            jax.ShapeDtypeStruct((B * n_j, LOCAL_ROWS, LANES), F32),
            jax.ShapeDtypeStruct((B * n_j, 1, 1, LANES), F32),
            jax.ShapeDtypeStruct((B, S, LANES), jnp.int32),
        ],
        grid=(B, S // tm),
        in_specs=[
            pl.BlockSpec((1, tm, D), tok3),
            pl.BlockSpec((1, tm, MOBA_WIDTH), tok3),
            pl.BlockSpec((1, tm, MLA_WIDTH), tok3),
            pl.BlockSpec((1, 6, D), lambda b, i: (b, 0, 0)),
            pl.BlockSpec((1, MOBA_WIDTH), const2),
            pl.BlockSpec((1, MLA_WIDTH), const2),
            pl.BlockSpec(w_o.shape, const2),
            pl.BlockSpec((1, D), const2),
            pl.BlockSpec((1, D), const2),
            pl.BlockSpec(w_r.shape, lambda b, i: (0, 0, 0)),
            pl.BlockSpec((1, LANES), const2),
        ],
        out_specs=[
            pl.BlockSpec((1, tm, D), tok3),
            pl.BlockSpec((1, LOCAL_ROWS, D), lambda b, i: (b * n_j + i, 0, 0)),
            pl.BlockSpec((1, LOCAL_ROWS, LANES), lambda b, i: (b * n_j + i, 0, 0)),
            pl.BlockSpec((1, 1, 1, LANES), lambda b, i: (b * n_j + i, 0, 0, 0)),
            pl.BlockSpec((1, tm, LANES), tok3),
        ],
        compiler_params=_cparams(("arbitrary", "arbitrary")),
        name="postattn",
    )(x, o_moba, o_mla, mod3, g_om, g_ol, w_o, g_post, g_pre, w_r, b_r)


def _start_chunk_gather(idx_ref, base, n_chunks, src_hbm, dst_ref, sem):
    for k in range(n_chunks):
        row = pl.multiple_of(idx_ref[base + k], CHUNK)
        pltpu.make_async_copy(src_hbm.at[pl.ds(row, CHUNK), :], dst_ref.at[pl.ds(k * CHUNK, CHUNK), :], sem).start()


def _wait_chunk_gather(n_chunks, src_hbm, dst_ref, sem):
    pltpu.make_async_copy(src_hbm.at[pl.ds(0, n_chunks * CHUNK), :], dst_ref, sem).wait()


def _moe_kernel(te_ref, nu_ref, src_ref, xl_hbm, gl_hbm, wgu_ref, wd_ref, y_ref, xbuf, gbuf, xsem, gsem):
    tm = y_ref.shape[0]
    n_chunks = tm // CHUNK
    i = pl.program_id(0)
    n_used = nu_ref[0]
    slot = i % 2

    def start(tile, s):
        _start_chunk_gather(src_ref, tile * n_chunks, n_chunks, xl_hbm, xbuf.at[s], xsem.at[s])
        _start_chunk_gather(src_ref, tile * n_chunks, n_chunks, gl_hbm, gbuf.at[s], gsem.at[s])

    @pl.when((i == 0) & (n_used > 0))
    def _():
        start(0, 0)

    @pl.when(i + 1 < n_used)
    def _():
        start(i + 1, 1 - slot)

    @pl.when(i < n_used)
    def _():
        _wait_chunk_gather(n_chunks, xl_hbm, xbuf.at[slot], xsem.at[slot])
        _wait_chunk_gather(n_chunks, gl_hbm, gbuf.at[slot], gsem.at[slot])
        gate = gbuf[slot]
        gu = jnp.dot(xbuf[slot], wgu_ref[0], preferred_element_type=F32)
        g = gu[:, :D_EXPERT]
        u = gu[:, D_EXPERT:]
        hid = g * jax.nn.sigmoid(g) * u
        y = jnp.dot(hid.astype(BF16), wd_ref[0], preferred_element_type=F32)
        y_ref[...] = (jnp.concatenate([gate] * (D_MODEL // LANES), axis=1) * y).astype(BF16)

    @pl.when(i >= n_used)
    def _():
        y_ref[...] = jnp.zeros_like(y_ref)


def _moe(tile_expert, n_used, chunk_src, x_local, g_local, w_gu, w_d):
    D = D_MODEL
    n_tiles = tile_expert.shape[0]
    tm = TM_MOE
    grid_spec = pltpu.PrefetchScalarGridSpec(
        num_scalar_prefetch=3,
        grid=(n_tiles,),
        in_specs=[
            pl.BlockSpec(memory_space=pl.ANY),
            pl.BlockSpec(memory_space=pl.ANY),
            pl.BlockSpec((1, D, 2 * D_EXPERT), lambda i, te, nu, src: (te[i], 0, 0)),
            pl.BlockSpec((1, D_EXPERT, D), lambda i, te, nu, src: (te[i], 0, 0)),
        ],
        out_specs=pl.BlockSpec((tm, D), lambda i, te, nu, src: (i, 0)),
        scratch_shapes=[pltpu.VMEM((2, tm, D), BF16), pltpu.VMEM((2, tm, LANES), F32),
                        pltpu.SemaphoreType.DMA((2,)), pltpu.SemaphoreType.DMA((2,))],
    )
    return pl.pallas_call(
        _moe_kernel,
        out_shape=jax.ShapeDtypeStruct((n_tiles * tm, D), BF16),
        grid_spec=grid_spec,
        compiler_params=_cparams(("arbitrary",)),
        name="moe",
    )(tile_expert, n_used, chunk_src, x_local, g_local, w_gu, w_d)


def _combine_kernel(src_ref, y_hbm, x1_ref, ri_ref, mod_ref, g_ref, o_ref, ybuf, sem):
    tm = x1_ref.shape[1]
    n_chunks = LOCAL_ROWS // CHUNK
    n_j = pl.num_programs(1)
    step = pl.program_id(0) * n_j + pl.program_id(1)
    n_steps = pl.num_programs(0) * n_j
    slot = step % 2

    @pl.when(step == 0)
    def _():
        _start_chunk_gather(src_ref, 0, n_chunks, y_hbm, ybuf.at[0], sem.at[0])

    @pl.when(step + 1 < n_steps)
    def _():
        _start_chunk_gather(src_ref, (step + 1) * n_chunks, n_chunks, y_hbm, ybuf.at[1 - slot], sem.at[1 - slot])

    _wait_chunk_gather(n_chunks, y_hbm, ybuf.at[slot], sem.at[slot])
    yl = ybuf[slot]
    col = lax.broadcasted_iota(jnp.int32, (tm, LOCAL_ROWS), 1)
    pick = jnp.where((col == ri_ref[0, :, 2:3]) | (col == ri_ref[0, :, 3:4]), 1.0, 0.0).astype(BF16)
    y = jnp.dot(pick, yl, preferred_element_type=F32)
    gt_f = mod_ref[0, 5:6, :]
    o_ref[0] = x1_ref[0] + gt_f * _rms(y, g_ref[...])


def _combine(chunk_src, y_sorted, x1, route_i, mod3, g_post):
    B, S, D = x1.shape
    tm = TM_IN
    grid_spec = pltpu.PrefetchScalarGridSpec(
        num_scalar_prefetch=1,
        grid=(B, S // tm),
        in_specs=[
            pl.BlockSpec(memory_space=pl.ANY),
            pl.BlockSpec((1, tm, D), lambda b, j, src: (b, j, 0)),
            pl.BlockSpec((1, tm, LANES), lambda b, j, src: (b, j, 0)),
            pl.BlockSpec((1, 6, D), lambda b, j, src: (b, 0, 0)),
            pl.BlockSpec((1, D), lambda b, j, src: (0, 0)),
        ],
        out_specs=pl.BlockSpec((1, tm, D), lambda b, j, src: (b, j, 0)),
        scratch_shapes=[pltpu.VMEM((2, LOCAL_ROWS, D), BF16), pltpu.SemaphoreType.DMA((2,))],
    )
    return pl.pallas_call(
        _combine_kernel,
        out_shape=jax.ShapeDtypeStruct((B, S, D), F32),
        grid_spec=grid_spec,
        compiler_params=_cparams(("arbitrary", "arbitrary")),
        name="combine",
    )(chunk_src, y_sorted, x1, route_i, mod3, g_post)


def _rope_angles(positions):
    def cos_sin(dim):
        half = dim // 2
        inv_freq = jnp.exp(-math.log(ROPE_THETA) * jnp.arange(half, dtype=F32) * (2.0 / dim))
        ang = positions.astype(F32)[:, None, :] * inv_freq[None, :, None]
        return [jnp.cos(ang), jnp.sin(ang)]
    B, S = positions.shape
    parts = cos_sin(MOBA_ROT_DIMS) + cos_sin(MLA_ROPE_DIM) + [jnp.ones((B, 1, S), F32)]
    used = MOBA_ROT_DIMS + MLA_ROPE_DIM + 1
    return jnp.concatenate(parts + [jnp.zeros((B, LANES - used, S), F32)], axis=1)


def _routing_metadata(counts, tm):
    i32 = jnp.int32
    n_tt = counts.shape[0]
    n_tiles = (2 * n_tt * TM_IN + n_tt * N_EXPERTS * (CHUNK - 1)) // tm + N_EXPERTS + 1
    seg = (counts + CHUNK - 1) // CHUNK * CHUNK
    loc_start = jnp.cumsum(seg, axis=1) - seg
    loc_total = jnp.sum(seg, axis=1)
    run = jnp.cumsum(seg, axis=0)
    rows_e = run[-1]
    tiles_per = (rows_e + tm - 1) // tm
    tile_end = jnp.cumsum(tiles_per)
    n_used = tile_end[-1]
    first_row = (tile_end - tiles_per) * tm
    tile_ids = jnp.arange(n_tiles, dtype=i32)
    te = jnp.minimum(jnp.sum((tile_ids[:, None] >= tile_end[None, :]).astype(i32), axis=1), N_EXPERTS - 1)
    last_e = jnp.sum(jnp.where(tile_ids == n_used - 1, te, 0))
    tile_expert = jnp.where(tile_ids < n_used, te, last_e).astype(i32)

    cpt = tm // CHUNK
    is_e = (tile_expert[:, None] == jnp.arange(N_EXPERTS, dtype=i32)[None, :]).astype(i32)
    of_tile = lambda a: jnp.sum(is_e[:, None, :] * a[None, :, :], axis=2)
    run_t, seg_t, loc_t = of_tile(run), of_tile(seg), of_tile(loc_start)
    first_t = jnp.sum(is_e * first_row[None, :], axis=1)
    rows_t = jnp.sum(is_e * rows_e[None, :], axis=1)
    off = (tile_ids * tm - first_t)[:, None] + jnp.arange(cpt, dtype=i32)[None, :] * CHUNK
    tau = jnp.minimum(jnp.sum((off[:, :, None] >= run_t[:, None, :]).astype(i32), axis=2), n_tt - 1)
    is_tau = (tau[:, :, None] == jnp.arange(n_tt, dtype=i32)[None, None, :]).astype(i32)
    at_tau = lambda a_t: jnp.sum(is_tau * a_t[:, None, :], axis=2)
    src = tau * LOCAL_ROWS + at_tau(loc_t) + off - (at_tau(run_t) - at_tau(seg_t))
    real = (tile_ids < n_used)[:, None] & (off < rows_t[:, None])
    zero_local = LOCAL_ROWS - CHUNK
    moe_src = jnp.where(real, src, zero_local).astype(i32).reshape(-1)

    lrow = jnp.arange(LOCAL_ROWS // CHUNK, dtype=i32) * CHUNK
    loc_end = loc_start + seg
    e_v = jnp.minimum(jnp.sum((lrow[None, :, None] >= loc_end[:, None, :]).astype(i32), axis=2), N_EXPERTS - 1)
    is_ev = (e_v[:, :, None] == jnp.arange(N_EXPERTS, dtype=i32)[None, None, :]).astype(i32)
    seg_first = first_row[None, :] + run - seg
    src_v = jnp.sum(is_ev * (seg_first - loc_start)[:, None, :], axis=2) + lrow[None, :]
    zero_sorted = (n_tiles - 1) * tm
    cmb_src = jnp.where(lrow[None, :] < loc_total[:, None], src_v, zero_sorted).astype(i32).reshape(-1)
    return tile_expert, n_used.reshape(1).astype(i32), moe_src, cmb_src


def kernel(x, c, positions, w_ada, b_ada, g_pre_attn, g_post_attn, g_pre_ffn, g_post_ffn, w_in, g_cq, w_uq,
           g_ckv, w_ukv, g_out_moba, g_out_mla, w_o, w_group, b_group, w_expert, b_expert, w_e_gate, w_e_up,
           w_e_down):
    B, S, D = x.shape
    depth = w_ada.shape[0]
    rope_cs = _rope_angles(positions)
    rope_exp = jnp.asarray(_rope_expansion(), BF16)
    row = lambda v: v.reshape(1, -1)

    for l in range(depth):
        n_main = 3 * MOBA_WIDTH + MLA_Q_RANK + MLA_KV_RANK
        w_in_p = jnp.concatenate([w_in[l][:, :n_main], jnp.tile(w_in[l][:, n_main:], (1, LANES // MLA_ROPE_DIM))],
                                 axis=1).astype(BF16)
        wq = w_uq[l].reshape(MLA_Q_RANK, MLA_HEADS, MLA_NOPE_DIM + MLA_ROPE_DIM)
        wq_n = wq[:, :, :MLA_NOPE_DIM]
        zero_n = jnp.zeros_like(wq_n)
        even = (jnp.arange(MLA_HEADS) % 2 == 0)[None, :, None]
        w_uq_n = jnp.concatenate([jnp.where(even, wq_n, zero_n), jnp.where(even, zero_n, wq_n)], axis=-1)
        w_uq_n = w_uq_n.reshape(MLA_Q_RANK, MLA_HEADS * LANES).astype(BF16)
        w_uq_p = wq[:, :, MLA_NOPE_DIM:].reshape(MLA_Q_RANK, MLA_HEADS * MLA_ROPE_DIM).astype(BF16)
        wkv = w_ukv[l].reshape(MLA_KV_RANK, MLA_HEADS, MLA_NOPE_DIM + MLA_V_DIM)
        w_ukv_p = jnp.concatenate([wkv[:, :, :MLA_NOPE_DIM].reshape(MLA_KV_RANK, -1),
                                   wkv[:, :, MLA_NOPE_DIM:].reshape(MLA_KV_RANK, -1)], axis=1).astype(BF16)
        w_r = jnp.concatenate([w_expert[l], w_group[l],
                               jnp.zeros((D, LANES - N_EXPERTS - N_GROUPS), F32)], axis=1)
        w_r_hi = w_r.astype(BF16)
        w_r = jnp.stack([w_r_hi, (w_r - w_r_hi.astype(F32)).astype(BF16)])
        b_r = jnp.concatenate([b_expert[l], b_group[l], jnp.zeros((LANES - N_EXPERTS - N_GROUPS,), F32)]).reshape(1, LANES)
        w_gu = jnp.concatenate([w_e_gate[l], w_e_up[l]], axis=-1).astype(BF16)
        w_d = w_e_down[l].astype(BF16)

        mod3 = _adaln(c, w_ada[l], b_ada[l]).reshape(B, 6, D)

        qa, ka, vat, kmean, qcat, kcat, vmt = _inproj(
            x, mod3, row(g_pre_attn[l]), w_in_p, rope_cs, rope_exp, row(g_cq[l]), w_uq_n, w_uq_p,
            row(g_ckv[l]), w_ukv_p)
        o_moba = _moba(qa, ka, vat, kmean)
        o_mla = _mla(qcat, kcat, vmt)
        x1, x_local, g_local, counts, route_i = _postattn(
            x, o_moba, o_mla, mod3, row(g_out_moba[l]), row(g_out_mla[l]), w_o[l].astype(BF16),
            row(g_post_attn[l]), row(g_pre_ffn[l]), w_r, b_r)

        counts = counts.reshape(-1, LANES)[:, :N_EXPERTS].astype(jnp.int32)
        tile_expert, n_used, moe_src, cmb_src = _routing_metadata(counts, TM_MOE)
        y_sorted = _moe(tile_expert, n_used, moe_src, x_local.reshape(-1, D), g_local.reshape(-1, LANES), w_gu, w_d)
        x = _combine(cmb_src, y_sorted, x1, route_i, mod3, row(g_post_ffn[l]))
    return x
```
